```python
import math
import jax
import jax.numpy as jnp
from jax import lax
import numpy as np

D_MODEL = 1024
BATCH = 2
SEQ = 8192
DEPTH = 2

N_META = 16
BLOCK = 128
META_PAD = BLOCK - N_META
ROPE_THETA = 10000.0
LN_EPS = 1e-5
NEG = -1e30

DA_HEADS = 4
DA_DIM = 64
WB_HEADS = 8
WB_KV = 2
WB_DIM = 64
WINDOW = 128
MC_HEADS = 4
MC_QK = 128
MC_V = 128
CONV_W = 3
N_BRANCH = 3
BRANCH_W = 512
N_GROUPS = 4
EXP_PER_GROUP = 8
N_EXPERTS = N_GROUPS * EXP_PER_GROUP
TOP_K = 2
D_EXPERT = 512

A_Q = DA_HEADS * 2 * DA_DIM
A_K = DA_HEADS * 2 * DA_DIM
A_V = DA_HEADS * 2 * DA_DIM
B_Q = WB_HEADS * WB_DIM
B_K = WB_KV * WB_DIM
B_V = WB_KV * WB_DIM
C_Q = MC_HEADS * MC_QK
C_K = MC_HEADS * MC_QK
C_V = MC_HEADS * MC_V
C_O = MC_HEADS * MC_V
C_G = 4 * MC_HEADS
GATE_W = N_BRANCH * D_MODEL
SPLITS = (A_Q, A_K, A_V, B_Q, B_K, B_V, C_Q, C_K, C_V, C_O, C_G, GATE_W)
D_IN = sum(SPLITS)

kernel_name = "hybrid_diffattn_swa_mlstm_hmoe_encoder"

F32 = jnp.float32


def layer_norm(x, g, b):
    x32 = x.astype(F32)
    mu = x32.mean(-1, keepdims=True)
    var = jnp.square(x32 - mu).mean(-1, keepdims=True)
    return ((x32 - mu) * lax.rsqrt(var + LN_EPS) * g.astype(F32) + b.astype(F32)).astype(x.dtype)


def rope_tables(length, dim):
    pos = jnp.arange(length, dtype=F32)
    inv = 1.0 / (ROPE_THETA ** (jnp.arange(0, dim, 2, dtype=F32) / dim))
    ang = pos[:, None] * inv[None, :]
    return jnp.cos(ang), jnp.sin(ang)


def apply_rope(x, cos, sin):
    shape = (cos.shape[0],) + (1,) * (x.ndim - 3) + (cos.shape[1],)
    c = cos.reshape(shape).astype(x.dtype)
    s = sin.reshape(shape).astype(x.dtype)
    x1, x2 = jnp.split(x, 2, axis=-1)
    return jnp.concatenate([x1 * c - x2 * s, x2 * c + x1 * s], axis=-1)


def pad_front(x, n, value=0.0):
    return jnp.pad(x, [(0, 0), (n, 0)] + [(0, 0)] * (x.ndim - 2), constant_values=value)


def diff_attention(q, k, v, lam, lam_init, norm_g):
    bsz, length = q.shape[:2]
    qp = pad_front(q, META_PAD)
    nb = qp.shape[1] // BLOCK
    qb = qp.reshape(bsz, nb, BLOCK, DA_HEADS, 2, DA_DIM).transpose(1, 0, 2, 3, 4, 5)
    scale = DA_DIM ** -0.5

    def one_block(qblk):
        s = jnp.einsum('bqhcd,bkhcd->bhcqk', qblk, k).astype(F32) * scale
        p = jax.nn.softmax(s, axis=-1)
        a = p[:, :, 0] - lam * p[:, :, 1]
        return jnp.einsum('bhqk,bkhe->bqhe', a.astype(v.dtype), v)

    o = lax.map(one_block, qb)
    o = o.transpose(1, 0, 2, 3, 4).reshape(bsz, nb * BLOCK, DA_HEADS, 2 * DA_DIM)[:, META_PAD:]
    o32 = o.astype(F32)
    o32 = o32 * lax.rsqrt(jnp.mean(jnp.square(o32), -1, keepdims=True) + LN_EPS) * norm_g.astype(F32) * (1.0 - lam_init)
    return o32.astype(v.dtype).reshape(bsz, length, A_V)


def windowed_gqa(q, k, v, sink):
    bsz, length = q.shape[:2]
    lp = length + META_PAD
    nb = lp // BLOCK
    grp = WB_HEADS // WB_KV
    qb = pad_front(q, META_PAD).reshape(bsz, nb, BLOCK, WB_KV, grp, WB_DIM)
    k_meta, v_meta = k[:, :N_META], v[:, :N_META]

    def band(t):
        tp = jnp.pad(t, [(0, 0), (META_PAD + BLOCK, BLOCK), (0, 0), (0, 0)])
        tp = tp.reshape(bsz, nb + 2, BLOCK, WB_KV, WB_DIM)
        return jnp.concatenate([tp[:, :-2], tp[:, 1:-1], tp[:, 2:]], axis=2)

    kb, vb = band(k), band(v)
    scale = WB_DIM ** -0.5
    s_band = jnp.einsum('bnqkgd,bnskd->bnkgqs', qb, kb).astype(F32) * scale
    s_meta = jnp.einsum('bnqkgd,bmkd->bnkgqm', qb, k_meta).astype(F32) * scale
    qidx = jnp.arange(nb)[:, None] * BLOCK + jnp.arange(BLOCK)[None, :]
    kidx = (jnp.arange(nb)[:, None] - 1) * BLOCK + jnp.arange(3 * BLOCK)[None, :]
    valid = ((kidx[:, None, :] >= BLOCK) & (kidx[:, None, :] < lp)
             & (jnp.abs(qidx[:, :, None] - kidx[:, None, :]) <= WINDOW))
    s_band = jnp.where(valid[None, :, None, None], s_band, NEG)
    s_sink = jnp.broadcast_to(sink.astype(F32).reshape(1, 1, WB_KV, grp, 1, 1), s_meta.shape[:-1] + (1,))
    p = jax.nn.softmax(jnp.concatenate([s_band, s_meta, s_sink], axis=-1), axis=-1)
    p_band = p[..., :3 * BLOCK].astype(v.dtype)
    p_meta = p[..., 3 * BLOCK:3 * BLOCK + N_META].astype(v.dtype)
    o = (jnp.einsum('bnkgqs,bnskd->bnqkgd', p_band, vb)
         + jnp.einsum('bnkgqm,bmkd->bnqkgd', p_meta, v_meta))
    return o.reshape(bsz, lp, B_Q)[:, META_PAD:]


def mlstm_direction(q, k, v, li, lf):
    bsz, lp, heads, dk = q.shape
    dv = v.shape[-1]
    nch = lp // BLOCK

    def chunks(t):
        return jnp.moveaxis(t.reshape((bsz, nch, BLOCK, heads) + t.shape[3:]), 3, 1)

    qc, kc, vc, lic, lfc = chunks(q), chunks(k), chunks(v), chunks(li), chunks(lf)
    b = jnp.cumsum(lfc, axis=-1)
    g = b[..., -1]
    a = g[..., None] - b + lic
    tril = jnp.tril(jnp.ones((BLOCK, BLOCK), dtype=bool))
    dmat = jnp.where(tril, b[..., :, None] - b[..., None, :] + lic[..., None, :], NEG)

    def step(carry, inp):
        c_st, n_st, m_st = carry
        k_i, v_i, a_i, g_i = inp
        m_new = jnp.maximum(g_i + m_st, a_i.max(-1))
        decay = jnp.exp(g_i + m_st - m_new)
        w = jnp.exp(a_i - m_new[..., None])
        c_new = decay[..., None, None] * c_st + jnp.einsum('bhc,bhcd,bhce->bhde', w, k_i, v_i)
        n_new = decay[..., None] * n_st + jnp.einsum('bhc,bhcd->bhd', w, k_i)
        return (c_new, n_new, m_new), (c_st, n_st, m_st)

    init = (jnp.zeros((bsz, heads, dk, dv), F32), jnp.zeros((bsz, heads, dk), F32), jnp.zeros((bsz, heads), F32))
    xs = (jnp.moveaxis(kc, 2, 0), jnp.moveaxis(vc, 2, 0), jnp.moveaxis(a, 2, 0), jnp.moveaxis(g, 2, 0))
    _, (c_prev, n_prev, m_prev) = lax.scan(step, init, xs)
    c_prev = jnp.moveaxis(c_prev, 0, 2)
    n_prev = jnp.moveaxis(n_prev, 0, 2)
    m_prev = jnp.moveaxis(m_prev, 0, 2)
    inter_log = b + m_prev[..., None]
    m_t = jnp.maximum(inter_log, dmat.max(-1))
    inter = jnp.exp(inter_log - m_t)
    s = jnp.einsum('bhncd,bhnsd->bhncs', qc, kc) * jnp.exp(dmat - m_t[..., None])
    num = inter[..., None] * jnp.einsum('bhncd,bhnde->bhnce', qc, c_prev) + jnp.einsum('bhncs,bhnse->bhnce', s, vc)
    den = inter * jnp.einsum('bhncd,bhnd->bhnc', qc, n_prev) + s.sum(-1)
    h = num / jnp.maximum(jnp.abs(den), jnp.exp(-m_t))[..., None]
    return jnp.moveaxis(h, 1, 3).reshape(bsz, lp, heads, dv)


def mlstm_mixer(cq, ck, cv, co, cg, conv_w, conv_b, gate_b, norm_g):
    bsz, length, _ = cq.shape
    qk = jnp.concatenate([cq, ck], axis=-1)
    qk = lax.conv_general_dilated(qk, conv_w[:, None, :].astype(qk.dtype), (1,), [(CONV_W // 2, CONV_W // 2)],
                                  dimension_numbers=('NWC', 'WIO', 'NWC'), feature_group_count=C_Q + C_K)
    qk = jax.nn.silu(qk + conv_b.astype(qk.dtype)).astype(F32)
    q = qk[..., :C_Q].reshape(bsz, length, MC_HEADS, MC_QK)
    k = qk[..., C_Q:].reshape(bsz, length, MC_HEADS, MC_QK) * (MC_QK ** -0.5)
    v = cv.astype(F32).reshape(bsz, length, MC_HEADS, MC_V)
    gates = (cg.astype(F32) + gate_b.astype(F32)).reshape(bsz, length, 4, MC_HEADS)
    li_f = gates[:, :, 0]
    lf_f = jax.nn.log_sigmoid(gates[:, :, 1])
    li_b = gates[:, :, 2]
    lf_b = jax.nn.log_sigmoid(gates[:, :, 3])
    qp, kp, vp = pad_front(q, META_PAD), pad_front(k, META_PAD), pad_front(v, META_PAD)
    h_f = mlstm_direction(qp, kp, vp, pad_front(li_f, META_PAD, NEG), pad_front(lf_f, META_PAD))

    def flip(t):
        return jnp.flip(t, axis=1)

    h_b = flip(mlstm_direction(flip(qp), flip(kp), flip(vp),
                               flip(pad_front(li_b, META_PAD, NEG)), flip(pad_front(lf_b, META_PAD))))
    h = h_f[:, META_PAD:] + h_b[:, META_PAD:]
    mu = h.mean(-1, keepdims=True)
    var = jnp.square(h - mu).mean(-1, keepdims=True)
    hn = (h - mu) * lax.rsqrt(var + LN_EPS) * norm_g.astype(F32).reshape(MC_HEADS, MC_V)
    return (hn.reshape(bsz, length, C_V) * jax.nn.sigmoid(co.astype(F32))).astype(cq.dtype)


def token_mixer(h, cos, sin, lam_init, w_in, conv_w, conv_b, gate_b, lam_q1, lam_k1, lam_q2, lam_k2,
                diff_g, sink, mlstm_g, w_branch, w_out):
    bsz, length, _ = h.shape
    z = h @ w_in
    cuts = np.cumsum(SPLITS)[:-1].tolist()
    aq, ak, av, bq, bk, bv, cq, ck, cv, co, cg, gz = jnp.split(z, cuts, axis=-1)
    qa = apply_rope(aq.reshape(bsz, length, DA_HEADS, 2, DA_DIM), cos, sin)
    ka = apply_rope(ak.reshape(bsz, length, DA_HEADS, 2, DA_DIM), cos, sin)
    va = av.reshape(bsz, length, DA_HEADS, 2 * DA_DIM)
    lam = (jnp.exp(jnp.sum(lam_q1.astype(F32) * lam_k1.astype(F32)))
           - jnp.exp(jnp.sum(lam_q2.astype(F32) * lam_k2.astype(F32))) + lam_init)
    out_a = diff_attention(qa, ka, va, lam, lam_init, diff_g)
    qb = apply_rope(bq.reshape(bsz, length, WB_HEADS, WB_DIM), cos, sin)
    kb = apply_rope(bk.reshape(bsz, length, WB_KV, WB_DIM), cos, sin)
    vb = bv.reshape(bsz, length, WB_KV, WB_DIM)
    out_b = windowed_gqa(qb, kb, vb, sink)
    out_c = mlstm_mixer(cq, ck, cv, co, cg, conv_w, conv_b, gate_b, mlstm_g)
    gates = jax.nn.sigmoid(gz.reshape(bsz, length, N_BRANCH, D_MODEL))
    merged = (gates[:, :, 0] * (out_a @ w_branch[0])
              + gates[:, :, 1] * (out_b @ w_branch[1])
              + gates[:, :, 2] * (out_c @ w_branch[2]))
    return merged @ w_out


def hier_moe(h, w_rg, b_rg, w_re, b_re, w_gate, w_up, w_down):
    bsz, length, dm = h.shape
    n_tok = bsz * length
    xf = h.reshape(n_tok, dm)
    g_logits = (xf @ w_rg).astype(F32) + b_rg.astype(F32)
    g_prob = jax.nn.softmax(g_logits, axis=-1)
    g_sel = jnp.argmax(g_logits, axis=-1)
    p_grp = jnp.max(g_prob, axis=-1)
    e_logits = ((xf @ w_re).astype(F32) + b_re.astype(F32)).reshape(n_tok, N_GROUPS, EXP_PER_GROUP)
    e_logits = e_logits[jnp.arange(n_tok), g_sel]
    top_v, top_i = lax.top_k(e_logits, TOP_K)
    w_tok = jax.nn.softmax(top_v, axis=-1) * p_grp[:, None]
    expert = g_sel[:, None].astype(jnp.int32) * EXP_PER_GROUP + top_i.astype(jnp.int32)
    n_assign = n_tok * TOP_K
    n_blocks = -(-(n_assign + N_EXPERTS * (BLOCK - 1)) // BLOCK)
    e_flat = expert.reshape(-1)
    t_flat = jnp.repeat(jnp.arange(n_tok, dtype=jnp.int32), TOP_K)
    w_flat = w_tok.reshape(-1)
    order = jnp.argsort(e_flat)
    e_s, t_s, w_s = e_flat[order], t_flat[order], w_flat[order]
    counts = jnp.bincount(e_flat, length=N_EXPERTS)
    start = jnp.cumsum(counts) - counts
    pcounts = (counts + BLOCK - 1) // BLOCK * BLOCK
    pend = jnp.cumsum(pcounts)
    pstart = pend - pcounts
    dest = pstart[e_s] + jnp.arange(n_assign, dtype=jnp.int32) - start[e_s]
    slot_tok = jnp.zeros((n_blocks * BLOCK,), jnp.int32).at[dest].set(t_s)
    slot_w = jnp.zeros((n_blocks * BLOCK,), F32).at[dest].set(w_s)
    block_e = jnp.minimum(jnp.searchsorted(pend, jnp.arange(n_blocks) * BLOCK, side='right'), N_EXPERTS - 1)
    xs = xf[slot_tok].reshape(n_blocks, BLOCK, dm)

    def run_expert(args):
        xb, e = args
        return (jax.nn.silu(xb @ w_gate[e]) * (xb @ w_up[e])) @ w_down[e]

    ys = lax.map(run_expert, (xs, block_e)).reshape(n_blocks * BLOCK, dm)
    y = jnp.zeros_like(xf).at[slot_tok].add(ys * slot_w[:, None].astype(ys.dtype))
    return y.reshape(bsz, length, dm)


def setup_inputs(seed: int = 0) -> dict:
    key = jax.random.key(seed)
    ks = jax.random.split(key, 32)
    beta = (8.0 * DEPTH) ** -0.25

    def nrm(k, shape, scale):
        return jax.random.normal(k, shape, F32) * scale

    offs = np.cumsum((0,) + SPLITS)
    col_scale = np.ones((D_IN,), np.float32)
    for i in (2, 5, 8):
        col_scale[offs[i]:offs[i + 1]] = beta
    f_bias = jnp.linspace(3.0, 6.0, MC_HEADS, dtype=F32)
    zeros_h = jnp.zeros((MC_HEADS,), F32)
    gate_base = jnp.stack([zeros_h, f_bias, zeros_h, f_bias])
    return {
        "x": nrm(ks[0], (BATCH, SEQ, D_MODEL), 1.0),
        "meta": nrm(ks[1], (N_META, D_MODEL), 1.0),
        "ln_in_g": 1.0 + nrm(ks[2], (D_MODEL,), 0.02),
        "ln_in_b": nrm(ks[3], (D_MODEL,), 0.02),
        "w_in": nrm(ks[4], (DEPTH, D_MODEL, D_IN), D_MODEL ** -0.5) * jnp.asarray(col_scale),
        "conv_w": nrm(ks[5], (DEPTH, CONV_W, C_Q + C_K), CONV_W ** -0.5),
        "conv_b": nrm(ks[6], (DEPTH, C_Q + C_K), 0.02),
        "gate_b": (gate_base[None] + nrm(ks[7], (DEPTH, 4, MC_HEADS), 0.1)).reshape(DEPTH, C_G),
        "lam_q1": nrm(ks[8], (DEPTH, DA_DIM), 0.1),
        "lam_k1": nrm(ks[9], (DEPTH, DA_DIM), 0.1),
        "lam_q2": nrm(ks[10], (DEPTH, DA_DIM), 0.1),
        "lam_k2": nrm(ks[11], (DEPTH, DA_DIM), 0.1),
        "diff_g": 1.0 + nrm(ks[12], (DEPTH, 2 * DA_DIM), 0.02),
        "sink": nrm(ks[13], (DEPTH, WB_HEADS), 0.5),
        "mlstm_g": 1.0 + nrm(ks[14], (DEPTH, C_V), 0.02),
        "w_branch": nrm(ks[15], (DEPTH, N_BRANCH, BRANCH_W, D_MODEL), BRANCH_W ** -0.5),
        "w_out": nrm(ks[16], (DEPTH, D_MODEL, D_MODEL), D_MODEL ** -0.5 * beta),
        "ln1_g": 1.0 + nrm(ks[17], (DEPTH, D_MODEL), 0.02),
        "ln1_b": nrm(ks[18], (DEPTH, D_MODEL), 0.02),
        "ln2_g": 1.0 + nrm(ks[19], (DEPTH, D_MODEL), 0.02),
        "ln2_b": nrm(ks[20], (DEPTH, D_MODEL), 0.02),
        "w_rg": nrm(ks[21], (DEPTH, D_MODEL, N_GROUPS), D_MODEL ** -0.5),
        "b_rg": nrm(ks[22], (DEPTH, N_GROUPS), 0.01),
        "w_re": nrm(ks[23], (DEPTH, D_MODEL, N_EXPERTS), D_MODEL ** -0.5),
        "b_re": nrm(ks[24], (DEPTH, N_EXPERTS), 0.01),
        "w_gate": nrm(ks[25], (DEPTH, N_EXPERTS, D_MODEL, D_EXPERT), D_MODEL ** -0.5 * beta),
        "w_up": nrm(ks[26], (DEPTH, N_EXPERTS, D_MODEL, D_EXPERT), D_MODEL ** -0.5 * beta),
        "w_down": nrm(ks[27], (DEPTH, N_EXPERTS, D_EXPERT, D_MODEL), D_EXPERT ** -0.5 * beta),
    }


def reference(x, meta, ln_in_g, ln_in_b, w_in, conv_w, conv_b, gate_b, lam_q1, lam_k1, lam_q2, lam_k2,
              diff_g, sink, mlstm_g, w_branch, w_out, ln1_g, ln1_b, ln2_g, ln2_b,
              w_rg, b_rg, w_re, b_re, w_gate, w_up, w_down):
    bsz = x.shape[0]
    alpha = (2.0 * DEPTH) ** 0.25
    h = jnp.concatenate([jnp.broadcast_to(meta.astype(x.dtype)[None], (bsz, N_META, D_MODEL)), x], axis=1)
    h = layer_norm(h, ln_in_g, ln_in_b)
    cos, sin = rope_tables(h.shape[1], DA_DIM)
    for l in range(DEPTH):
        lam_init = 0.8 - 0.6 * math.exp(-0.3 * l)
        y = token_mixer(h, cos, sin, lam_init, w_in[l], conv_w[l], conv_b[l], gate_b[l],
                        lam_q1[l], lam_k1[l], lam_q2[l], lam_k2[l], diff_g[l], sink[l], mlstm_g[l],
                        w_branch[l], w_out[l])
        h = layer_norm(alpha * h + y, ln1_g[l], ln1_b[l])
        y = hier_moe(h, w_rg[l], b_rg[l], w_re[l], b_re[l], w_gate[l], w_up[l], w_down[l])
        h = layer_norm(alpha * h + y, ln2_g[l], ln2_b[l])
    return h[:, N_META:]
```

```python
import functools
import math

import numpy as np
import jax
import jax.numpy as jnp
from jax import lax
from jax.experimental import pallas as pl
from jax.experimental.pallas import tpu as pltpu

D_MODEL = 1024
N_META_TOK = 16
BLOCK = 128
ROW_PAD = BLOCK - N_META_TOK
ROPE_THETA = 10000.0
LN_EPS = 1e-5
NEG = -1e30

DA_HEADS = 4
DA_DIM = 64
WB_HEADS = 8
WB_KV = 2
WB_DIM = 64
WINDOW = 128
MC_HEADS = 4
MC_QK = 128
MC_V = 128
N_BRANCH = 3
BRANCH_W = 512
N_GROUPS = 4
EXP_PER_GROUP = 8
N_EXPERTS = N_GROUPS * EXP_PER_GROUP
D_EXPERT = 512

A_Q = DA_HEADS * 2 * DA_DIM
A_K = A_Q
A_V = A_Q
B_Q = WB_HEADS * WB_DIM
B_K = WB_KV * WB_DIM
B_V = B_K
C_Q = MC_HEADS * MC_QK
C_K = C_Q
C_V = MC_HEADS * MC_V
C_O = C_V
C_G = 4 * MC_HEADS
GATE_W = N_BRANCH * D_MODEL
SPLITS = (A_Q, A_K, A_V, B_Q, B_K, B_V, C_Q, C_K, C_V, C_O, C_G, GATE_W)
OFFS = tuple(int(v) for v in np.cumsum((0,) + SPLITS))

LANES = 128
EXPERT_ROWS = 256
VMEM_LIMIT = 56 * 1024 * 1024

F32 = jnp.float32
MXU_DTYPE = jnp.bfloat16


def _dot(a, b):
    return jnp.dot(a, b, preferred_element_type=F32)


def _dot_nt(a, b):
    return lax.dot_general(a, b, (((1,), (1,)), ((), ())), preferred_element_type=F32)


def _dot_tn(a, b):
    return lax.dot_general(a, b, (((0,), (0,)), ((), ())), preferred_element_type=F32)


def _params(n_axes):
    return pltpu.CompilerParams(dimension_semantics=("arbitrary",) * n_axes,
                                vmem_limit_bytes=VMEM_LIMIT)


def _row_tile(n_rows, target):
    best = BLOCK
    for t in range(BLOCK, target + 1, BLOCK):
        if n_rows % t == 0:
            best = t
    return best


def _token_tile(n_rows, target):
    best = 8
    for t in range(8, target + 1, 8):
        if n_rows % t == 0:
            best = t
    return best


def _layer_norm(x, g, b):
    mu = jnp.mean(x, axis=-1, keepdims=True)
    xc = x - mu
    var = jnp.mean(xc * xc, axis=-1, keepdims=True)
    return xc * lax.rsqrt(var + LN_EPS) * g + b


def _ln_kernel(x_ref, g_ref, b_ref, o_ref):
    o_ref[...] = _layer_norm(x_ref[...], g_ref[...], b_ref[...])


def _ln_call(x, g, b, tm):
    m, d = x.shape
    return pl.pallas_call(
        _ln_kernel,
        grid=(m // tm,),
        in_specs=[pl.BlockSpec((tm, d), lambda i: (i, 0)),
                  pl.BlockSpec((1, d), lambda i: (0, 0)),
                  pl.BlockSpec((1, d), lambda i: (0, 0))],
        out_specs=pl.BlockSpec((tm, d), lambda i: (i, 0)),
        out_shape=jax.ShapeDtypeStruct((m, d), F32),
        compiler_params=_params(1),
        name="ln_in",
    )(x, g.reshape(1, d), b.reshape(1, d))


def _proj_rope_kernel(x_ref, w_ref, cos_ref, sin_ref, scale_ref, o_ref):
    z = _dot(x_ref[...].astype(MXU_DTYPE), w_ref[...])
    cos = cos_ref[...]
    sin = sin_ref[...]
    lane = lax.broadcasted_iota(jnp.int32, cos.shape, 1)
    first_half = (lane % DA_DIM) < (DA_DIM // 2)
    for c in range(z.shape[1] // LANES):
        sl = slice(c * LANES, (c + 1) * LANES)
        zc = z[:, sl]
        partner = jnp.where(first_half, pltpu.roll(zc, LANES - DA_DIM // 2, 1),
                            pltpu.roll(zc, DA_DIM // 2, 1))
        o_ref[:, sl] = ((zc * cos + partner * sin) * scale_ref[:, sl]).astype(o_ref.dtype)


def _proj_rope_call(h, w, cos, sin, scale, tm, lp):
    m, d = h.shape
    n = w.shape[1]
    per_batch = lp // tm
    return pl.pallas_call(
        _proj_rope_kernel,
        grid=(m // tm,),
        in_specs=[pl.BlockSpec((tm, d), lambda i: (i, 0)),
                  pl.BlockSpec((d, n), lambda i: (0, 0)),
                  pl.BlockSpec((tm, LANES), lambda i: (i % per_batch, 0)),
                  pl.BlockSpec((tm, LANES), lambda i: (i % per_batch, 0)),
                  pl.BlockSpec((1, n), lambda i: (0, 0))],
        out_specs=pl.BlockSpec((tm, n), lambda i: (i, 0)),
        out_shape=jax.ShapeDtypeStruct((m, n), MXU_DTYPE),
        compiler_params=_params(1),
        name="proj_rope",
    )(h, w, cos, sin, scale)


def _proj_kernel(x_ref, w_ref, o_ref, *, act):
    z = _dot(x_ref[...].astype(MXU_DTYPE), w_ref[...])
    if act == "sigmoid":
        z = jax.nn.sigmoid(z)
    o_ref[...] = z.astype(o_ref.dtype)


def _proj_call(h, w, tm, out_dtype, act, name):
    m, d = h.shape
    n = w.shape[1]
    return pl.pallas_call(
        functools.partial(_proj_kernel, act=act),
        grid=(m // tm,),
        in_specs=[pl.BlockSpec((tm, d), lambda i: (i, 0)),
                  pl.BlockSpec((d, n), lambda i: (0, 0))],
        out_specs=pl.BlockSpec((tm, n), lambda i: (i, 0)),
        out_shape=jax.ShapeDtypeStruct((m, n), out_dtype),
        compiler_params=_params(1),
        name=name,
    )(h, w)


def _gates_kernel(x_ref, w_ref, b_ref, gc_ref, gr_ref, *, tm, lp):
    z = _dot(x_ref[...].astype(MXU_DTYPE), w_ref[...]) + b_ref[...]
    lane = lax.broadcasted_iota(jnp.int32, (tm, LANES), 1)
    kind = lane // MC_HEADS
    row = lax.broadcasted_iota(jnp.int32, (tm, LANES), 0) + pl.program_id(0) * tm
    unused = (row % lp) < ROW_PAD
    log_f = jnp.minimum(z, 0.0) - jnp.log1p(jnp.exp(-jnp.abs(z)))
    is_forget = (kind % 2) == 1
    base = jnp.where(is_forget, jnp.where(unused, 0.0, log_f), jnp.where(unused, NEG, z))
    r128 = lax.broadcasted_iota(jnp.int32, (BLOCK, LANES), 0)
    fwd_lane = lax.broadcasted_iota(jnp.int32, (BLOCK, LANES), 1) // MC_HEADS == 1
    forget128 = (lax.broadcasted_iota(jnp.int32, (BLOCK, LANES), 1) // MC_HEADS) % 2 == 1
    for c in range(tm // BLOCK):
        x = base[c * BLOCK:(c + 1) * BLOCK]
        pre = x
        suf = x
        s = 1
        while s < BLOCK:
            pre = pre + jnp.where(r128 >= s, pltpu.roll(pre, s, 0), 0.0)
            suf = suf + jnp.where(r128 < BLOCK - s, pltpu.roll(suf, BLOCK - s, 0), 0.0)
            s *= 2
        out = jnp.where(forget128, jnp.where(fwd_lane, pre, suf), x)
        gc_ref[c * BLOCK:(c + 1) * BLOCK, :] = out
        gr_ref[:, c * BLOCK:(c + 1) * BLOCK] = out.T[0:C_G, :]


def _gates_call(h, w, b, tm, lp):
    m, d = h.shape
    return pl.pallas_call(
        functools.partial(_gates_kernel, tm=tm, lp=lp),
        grid=(m // tm,),
        in_specs=[pl.BlockSpec((tm, d), lambda i: (i, 0)),
                  pl.BlockSpec((d, LANES), lambda i: (0, 0)),
                  pl.BlockSpec((1, LANES), lambda i: (0, 0))],
        out_specs=[pl.BlockSpec((tm, LANES), lambda i: (i, 0)),
                   pl.BlockSpec((C_G, tm), lambda i: (0, i))],
        out_shape=[jax.ShapeDtypeStruct((m, LANES), F32),
                   jax.ShapeDtypeStruct((C_G, m), F32)],
        compiler_params=_params(1),
        name="mlstm_gates",
    )(h, w, b)


def _conv_kernel(z_ref, w_ref, b_ref, scale_ref, o_ref, *, lp, tr):
    w0 = w_ref[0:1, :]
    w1 = w_ref[1:2, :]
    w2 = w_ref[2:3, :]
    row = lax.broadcasted_iota(jnp.int32, (tr, LANES), 0)
    for c in range(lp // tr):
        r0 = c * tr
        zc = z_ref[0, r0:r0 + tr, :]
        before = jnp.zeros((1, LANES), F32) if r0 == 0 else z_ref[0, r0 - 1:r0, :]
        after = jnp.zeros((1, LANES), F32) if r0 + tr == lp else z_ref[0, r0 + tr:r0 + tr + 1, :]
        prev = jnp.where(row == 0, before, pltpu.roll(zc, 1, 0))
        nxt = jnp.where(row == tr - 1, after, pltpu.roll(zc, tr - 1, 0))
        if r0 <= ROW_PAD < r0 + tr:
            prev = jnp.where(row == ROW_PAD - r0, 0.0, prev)
        y = prev * w0 + zc * w1 + nxt * w2 + b_ref[...]
        o_ref[0, r0:r0 + tr, :] = (jax.nn.silu(y) * scale_ref[...]).astype(o_ref.dtype)


def _conv_call(z, w, b, scale, tr):
    bsz, lp, n = z.shape
    return pl.pallas_call(
        functools.partial(_conv_kernel, lp=lp, tr=tr),
        grid=(bsz, n // LANES),
        in_specs=[pl.BlockSpec((1, lp, LANES), lambda b_, j: (b_, 0, j)),
                  pl.BlockSpec((3, LANES), lambda b_, j: (0, j)),
                  pl.BlockSpec((1, LANES), lambda b_, j: (0, j)),
                  pl.BlockSpec((1, LANES), lambda b_, j: (0, j))],
        out_specs=pl.BlockSpec((1, lp, LANES), lambda b_, j: (b_, 0, j)),
        out_shape=jax.ShapeDtypeStruct((bsz, lp, n), MXU_DTYPE),
        compiler_params=_params(2),
        name="mlstm_conv",
    )(z, w, b, scale)


def _attn_a_kernel(lamv_ref, g_ref, q_ref, k_ref, v_ref, o_ref, *, tk, n_chunks, lam_init):
    q = q_ref[0]
    tq = q.shape[0]
    lane = lax.broadcasted_iota(jnp.int32, q.shape, 1)
    key_row = lax.broadcasted_iota(jnp.int32, (tk, tq), 0)
    zero = jnp.zeros_like(q)

    def run_map(qz):
        def chunk(j, carry, masked):
            m, l, acc = carry
            start = pl.multiple_of(j * tk, tk)
            kj = k_ref[0, pl.ds(start, tk), :]
            vj = v_ref[0, pl.ds(start, tk), :]
            s = _dot_nt(kj, qz)
            if masked:
                s = jnp.where(key_row >= ROW_PAD, s, NEG)
            m_new = jnp.maximum(m, jnp.max(s, axis=0, keepdims=True))
            alpha = jnp.exp(m - m_new)
            p = jnp.exp(s - m_new)
            l = alpha * l + jnp.sum(p, axis=0, keepdims=True)
            acc = alpha * acc + _dot_tn(vj, p.astype(MXU_DTYPE))
            return m_new, l, acc

        init = (jnp.full((1, tq), NEG, F32), jnp.zeros((1, tq), F32),
                jnp.zeros((2 * DA_DIM, tq), F32))
        carry = chunk(0, init, True)
        m, l, acc = lax.fori_loop(1, n_chunks, lambda j, c: chunk(j, c, False), carry)
        return acc / l

    o0 = run_map(jnp.where(lane < DA_DIM, q, zero))
    o1 = run_map(jnp.where(lane >= DA_DIM, q, zero))
    lv = lamv_ref[...]
    lam = (jnp.exp(jnp.sum(lv[0:1] * lv[1:2], axis=-1, keepdims=True))
           - jnp.exp(jnp.sum(lv[2:3] * lv[3:4], axis=-1, keepdims=True)) + lam_init)
    o = o0 - lam * o1
    ms = jnp.mean(o * o, axis=0, keepdims=True)
    o = o * lax.rsqrt(ms + LN_EPS) * g_ref[...] * (1.0 - lam_init)
    o_ref[0] = o.T.astype(o_ref.dtype)


def _attn_a_call(rq, vv, lamv, g_col, lam_init, tq, tk):
    bsz, lp, _ = rq.shape
    k_blk = A_Q // LANES
    return pl.pallas_call(
        functools.partial(_attn_a_kernel, tk=tk, n_chunks=lp // tk, lam_init=lam_init),
        grid=(bsz, DA_HEADS, lp // tq),
        in_specs=[pl.BlockSpec((4, DA_DIM), lambda b, h, i: (0, 0)),
                  pl.BlockSpec((2 * DA_DIM, 1), lambda b, h, i: (0, 0)),
                  pl.BlockSpec((1, tq, LANES), lambda b, h, i: (b, i, h)),
                  pl.BlockSpec((1, lp, LANES), lambda b, h, i: (b, 0, k_blk + h)),
                  pl.BlockSpec((1, lp, LANES), lambda b, h, i: (b, 0, h))],
        out_specs=pl.BlockSpec((1, tq, LANES), lambda b, h, i: (b, i, h)),
        out_shape=jax.ShapeDtypeStruct((bsz, lp, A_V), MXU_DTYPE),
        compiler_params=_params(3),
        name="diff_attn",
    )(lamv, g_col, rq, rq, vv)


def _attn_b_kernel(sink_ref, q_ref, k0_ref, k1_ref, v0_ref, v1_ref, o_ref, *, lp):
    n = pl.program_id(1)
    nb = lp // BLOCK

    def blocks(ref):
        parts = [ref[0, 0:BLOCK, :]]
        for d in (-1, 0, 1):
            idx = jnp.clip(n + d, 0, nb - 1)
            parts.append(ref[0, pl.ds(pl.multiple_of(idx * BLOCK, BLOCK), BLOCK), :])
        return jnp.concatenate(parts, axis=0)

    keys = (blocks(k0_ref), blocks(k1_ref))
    vals = (blocks(v0_ref), blocks(v1_ref))
    shape = (BLOCK, 4 * BLOCK)
    col = lax.broadcasted_iota(jnp.int32, shape, 1)
    qpos = lax.broadcasted_iota(jnp.int32, shape, 0) + n * BLOCK
    kpos = (n - 2) * BLOCK + col
    band_ok = (kpos >= BLOCK) & (kpos < lp) & (jnp.abs(qpos - kpos) <= WINDOW)
    valid = ((col >= ROW_PAD) & (col < BLOCK)) | ((col >= BLOCK) & band_ok)
    lane = lax.broadcasted_iota(jnp.int32, (BLOCK, LANES), 1)
    grp = WB_HEADS // WB_KV
    for t in range(WB_HEADS // 2):
        qt = q_ref[0, :, t * LANES:(t + 1) * LANES]
        outs = []
        for half in range(2):
            h = 2 * t + half
            kv = h // grp
            keep = (lane >= WB_DIM) if half else (lane < WB_DIM)
            qz = jnp.where(keep, qt, jnp.zeros_like(qt))
            s = jnp.where(valid, _dot_nt(qz, keys[kv]), NEG)
            sink = sink_ref[:, h:h + 1]
            m = jnp.maximum(jnp.max(s, axis=-1, keepdims=True), sink)
            p = jnp.exp(s - m)
            den = jnp.sum(p, axis=-1, keepdims=True) + jnp.exp(sink - m)
            outs.append(_dot((p / den).astype(MXU_DTYPE), vals[kv]))
        o_ref[0, :, t * LANES:(t + 1) * LANES] = jnp.where(lane < WB_DIM, outs[0], outs[1]).astype(o_ref.dtype)


def _attn_b_call(rq, vv, sink):
    bsz, lp, _ = rq.shape
    q_blk = (A_Q + A_K) // B_Q
    k_blk = (A_Q + A_K + B_Q) // LANES
    v_blk = (A_V + C_V) // LANES
    seq = lambda c: pl.BlockSpec((1, lp, LANES), lambda b, n: (b, 0, c))
    return pl.pallas_call(
        functools.partial(_attn_b_kernel, lp=lp),
        grid=(bsz, lp // BLOCK),
        in_specs=[pl.BlockSpec((1, WB_HEADS), lambda b, n: (0, 0)),
                  pl.BlockSpec((1, BLOCK, B_Q), lambda b, n: (b, n, q_blk)),
                  seq(k_blk), seq(k_blk + 1), seq(v_blk), seq(v_blk + 1)],
        out_specs=pl.BlockSpec((1, BLOCK, B_Q), lambda b, n: (b, n, 0)),
        out_shape=jax.ShapeDtypeStruct((bsz, lp, B_Q), MXU_DTYPE),
        compiler_params=_params(2),
        name="window_attn",
    )(sink, rq, rq, rq, vv, vv)


def _mlstm_kernel(qkf_ref, vf_ref, gcf_ref, grf_ref, qkb_ref, vb_ref, gcb_ref, grb_ref,
                  hf_ref, hb_ref, c_scr, m_scr):
    t = pl.program_id(1)

    @pl.when(t == 0)
    def _():
        c_scr[...] = jnp.zeros_like(c_scr)
        m_scr[...] = jnp.zeros_like(m_scr)

    row = lax.broadcasted_iota(jnp.int32, (BLOCK, BLOCK), 0)
    col = lax.broadcasted_iota(jnp.int32, (BLOCK, BLOCK), 1)
    ones_col = jnp.where(col == 0, 1.0, 0.0).astype(MXU_DTYPE)
    dirs = ((qkf_ref, vf_ref, gcf_ref, grf_ref, hf_ref, row >= col, BLOCK - 1),
            (qkb_ref, vb_ref, gcb_ref, grb_ref, hb_ref, row <= col, 0))
    for d, (qk_ref, v_ref, gc_ref, gr_ref, h_ref, tri, last) in enumerate(dirs):
        for hd in range(MC_HEADS):
            ci = d * MC_HEADS + hd
            j_li = (2 * d) * MC_HEADS + hd
            j_b = (2 * d + 1) * MC_HEADS + hd
            q = qk_ref[0, :, hd * MC_QK:(hd + 1) * MC_QK]
            k = qk_ref[0, :, C_Q + hd * MC_QK:C_Q + (hd + 1) * MC_QK]
            v = v_ref[0, :, hd * MC_V:(hd + 1) * MC_V]
            vext = jnp.concatenate([v, ones_col], axis=1)
            li_col = gc_ref[:, j_li:j_li + 1]
            b_col = gc_ref[:, j_b:j_b + 1]
            li_row = gr_ref[j_li:j_li + 1, :]
            b_row = gr_ref[j_b:j_b + 1, :]
            g = b_row[:, last:last + 1]
            m_prev = m_scr[ci, 0:1, 0:1]
            c_prev = c_scr[ci]

            dmat = jnp.where(tri, b_col - b_row + li_row, NEG)
            m_t = jnp.maximum(b_col + m_prev, jnp.max(dmat, axis=-1, keepdims=True))
            inter = jnp.exp(b_col + m_prev - m_t)
            s = (_dot_nt(q, k) * jnp.exp(dmat - m_t)).astype(MXU_DTYPE)
            nd = inter * _dot(q, c_prev.astype(MXU_DTYPE)) + _dot(s, vext)
            den = nd[:, MC_V:MC_V + 1]
            h_ref[0, :, hd * MC_V:(hd + 1) * MC_V] = nd[:, 0:MC_V] / jnp.maximum(jnp.abs(den), jnp.exp(-m_t))

            a_col = g - b_col + li_col
            m_new = jnp.maximum(g + m_prev, jnp.max(a_col, axis=0, keepdims=True))
            decay = jnp.exp(g + m_prev - m_new)
            kw = (jnp.exp(a_col - m_new) * k.astype(F32)).astype(MXU_DTYPE)
            c_scr[ci] = decay * c_prev + _dot_tn(kw, vext)
            m_scr[ci] = jnp.broadcast_to(m_new, m_scr.shape[1:])


def _mlstm_call(qk, vv, gc, gr):
    bsz, lp, _ = qk.shape
    nch = lp // BLOCK
    v_blk = A_V // C_V
    fwd = lambda b, t: (b, t, 0)
    bwd = lambda b, t: (b, nch - 1 - t, 0)
    return pl.pallas_call(
        _mlstm_kernel,
        grid=(bsz, nch),
        in_specs=[pl.BlockSpec((1, BLOCK, C_Q + C_K), fwd),
                  pl.BlockSpec((1, BLOCK, C_V), lambda b, t: (b, t, v_blk)),
                  pl.BlockSpec((BLOCK, LANES), lambda b, t: (b * nch + t, 0)),
                  pl.BlockSpec((C_G, BLOCK), lambda b, t: (0, b * nch + t)),
                  pl.BlockSpec((1, BLOCK, C_Q + C_K), bwd),
                  pl.BlockSpec((1, BLOCK, C_V), lambda b, t: (b, nch - 1 - t, v_blk)),
                  pl.BlockSpec((BLOCK, LANES), lambda b, t: (b * nch + nch - 1 - t, 0)),
                  pl.BlockSpec((C_G, BLOCK), lambda b, t: (0, b * nch + nch - 1 - t))],
        out_specs=[pl.BlockSpec((1, BLOCK, C_V), fwd),
                   pl.BlockSpec((1, BLOCK, C_V), bwd)],
        out_shape=[jax.ShapeDtypeStruct((bsz, lp, C_V), F32),
                   jax.ShapeDtypeStruct((bsz, lp, C_V), F32)],
        scratch_shapes=[pltpu.VMEM((2 * MC_HEADS, MC_QK, 2 * MC_V), F32),
                        pltpu.VMEM((2 * MC_HEADS, 8, LANES), F32)],
        compiler_params=_params(2),
        name="mlstm_scan",
    )(qk, vv, gc, gr, qk, vv, gc, gr)


def _merge_kernel(h_ref, oa_ref, ob_ref, hf_ref, hb_ref, co_ref, mg_ref, wg_ref, wb_ref, wo_ref,
                  lg_ref, lb_ref, o_ref, *, alpha):
    h = h_ref[...]
    hx = h.astype(MXU_DTYPE)
    hc = hf_ref[...] + hb_ref[...]
    parts = []
    for hd in range(MC_HEADS):
        sl = slice(hd * MC_V, (hd + 1) * MC_V)
        x = hc[:, sl]
        mu = jnp.mean(x, axis=-1, keepdims=True)
        xc = x - mu
        var = jnp.mean(xc * xc, axis=-1, keepdims=True)
        parts.append(xc * lax.rsqrt(var + LN_EPS) * mg_ref[:, sl] * co_ref[:, sl])
    oc = jnp.concatenate(parts, axis=1).astype(MXU_DTYPE)
    branches = (oa_ref[...], ob_ref[...], oc)
    merged = None
    for br in range(N_BRANCH):
        gate = jax.nn.sigmoid(_dot(hx, wg_ref[:, br * D_MODEL:(br + 1) * D_MODEL]))
        term = gate * _dot(branches[br], wb_ref[br])
        merged = term if merged is None else merged + term
    y = _dot(merged.astype(MXU_DTYPE), wo_ref[...])
    o_ref[...] = _layer_norm(alpha * h + y, lg_ref[...], lb_ref[...])


def _merge_call(h, oa, ob, hf, hb, co, mg, wg, wb, wo, lg, lb, alpha, tm):
    m, d = h.shape
    rows = lambda n: pl.BlockSpec((tm, n), lambda i: (i, 0))
    full2 = lambda a: pl.BlockSpec(a.shape, lambda i: (0, 0))
    return pl.pallas_call(
        functools.partial(_merge_kernel, alpha=alpha),
        grid=(m // tm,),
        in_specs=[rows(d), rows(A_V), rows(B_Q), rows(C_V), rows(C_V), rows(C_O),
                  full2(mg), full2(wg), pl.BlockSpec(wb.shape, lambda i: (0, 0, 0)), full2(wo),
                  full2(lg), full2(lb)],
        out_specs=rows(d),
        out_shape=jax.ShapeDtypeStruct((m, d), F32),
        compiler_params=_params(1),
        name="merge_ln1",
    )(h, oa, ob, hf, hb, co, mg, wg, wb, wo, lg, lb)


def _split3(x):
    hi = x.astype(MXU_DTYPE)
    lo = (x - hi.astype(F32)).astype(MXU_DTYPE)
    return hi, lo


def _router_kernel(h_ref, w_ref, b_ref, rt_ref, rc_ref, *, tm, lp):
    x_hi, x_lo = _split3(h_ref[...])
    w_hi, w_lo = _split3(w_ref[...])
    logits = (_dot_nt(w_hi, x_hi) + _dot_nt(w_hi, x_lo) + _dot_nt(w_lo, x_hi)) + b_ref[...]
    none = float(N_EXPERTS)
    gl = logits[N_EXPERTS:N_EXPERTS + 8]
    grow = lax.broadcasted_iota(jnp.int32, gl.shape, 0).astype(F32)
    gmax = jnp.max(gl, axis=0, keepdims=True)
    g_sel = jnp.min(jnp.where(gl == gmax, grow, none), axis=0, keepdims=True)
    p_grp = 1.0 / jnp.sum(jnp.exp(gl - gmax), axis=0, keepdims=True)
    el = logits[0:N_EXPERTS]
    erow_i = lax.broadcasted_iota(jnp.int32, el.shape, 0)
    erow = erow_i.astype(F32)
    cand = jnp.where((erow_i // EXP_PER_GROUP).astype(F32) == g_sel, el, -jnp.inf)
    top1 = jnp.max(cand, axis=0, keepdims=True)
    i1 = jnp.min(jnp.where(cand == top1, erow, none), axis=0, keepdims=True)
    cand2 = jnp.where(erow == i1, -jnp.inf, cand)
    top2 = jnp.max(cand2, axis=0, keepdims=True)
    i2 = jnp.min(jnp.where(cand2 == top2, erow, none), axis=0, keepdims=True)
    e = jnp.exp(top2 - top1)
    w1 = (1.0 / (1.0 + e)) * p_grp
    w2 = (e / (1.0 + e)) * p_grp
    pos = lax.broadcasted_iota(jnp.int32, (1, tm), 1) + pl.program_id(0) * tm
    real = (pos % lp) >= ROW_PAD
    e1 = jnp.where(real, i1, none)
    e2 = jnp.where(real, i2, none)
    r = lax.broadcasted_iota(jnp.int32, (LANES, tm), 0)
    table = jnp.where(r == 0, e1, jnp.where(r == 1, e2, jnp.where(r == 2, w1, jnp.where(r == 3, w2, 0.0))))
    rt_ref[...] = table[0:8]
    rc_ref[...] = table.T


def _router_call(h, w, b, tm, lp):
    m, d = h.shape
    return pl.pallas_call(
        functools.partial(_router_kernel, tm=tm, lp=lp),
        grid=(m // tm,),
        in_specs=[pl.BlockSpec((tm, d), lambda i: (i, 0)),
                  pl.BlockSpec((LANES, d), lambda i: (0, 0)),
                  pl.BlockSpec((LANES, 1), lambda i: (0, 0))],
        out_specs=[pl.BlockSpec((8, tm), lambda i: (0, i)),
                   pl.BlockSpec((tm, LANES), lambda i: (i, 0))],
        out_shape=[jax.ShapeDtypeStruct((8, m), F32),
                   jax.ShapeDtypeStruct((m, LANES), F32)],
        compiler_params=_params(1),
        name="moe_router",
    )(h, w, b)


def _rank_kernel(rt_ref, rk_ref, cnt_ref, carry, *, tm):
    @pl.when(pl.program_id(0) == 0)
    def _():
        carry[...] = jnp.zeros_like(carry)

    erow = lax.broadcasted_iota(jnp.int32, (N_EXPERTS, tm), 0).astype(F32)
    oh1 = jnp.where(erow == rt_ref[0:1, :], 1.0, 0.0)
    oh2 = jnp.where(erow == rt_ref[1:2, :], 1.0, 0.0)
    oh = oh1 + oh2
    earlier = (lax.broadcasted_iota(jnp.int32, (tm, tm), 0)
               < lax.broadcasted_iota(jnp.int32, (tm, tm), 1))
    before = _dot(oh.astype(MXU_DTYPE), jnp.where(earlier, 1.0, 0.0).astype(MXU_DTYPE)) + carry[:, 0:1]
    r1 = jnp.sum(oh1 * before, axis=0, keepdims=True)
    r2 = jnp.sum(oh2 * before, axis=0, keepdims=True)
    r = lax.broadcasted_iota(jnp.int32, (8, tm), 0)
    rk_ref[...] = jnp.where(r == 0, r1, jnp.where(r == 1, r2, 0.0))
    total = carry[...] + jnp.sum(oh, axis=1, keepdims=True)
    carry[...] = total
    cnt_ref[...] = total


def _rank_call(rt, tm):
    m = rt.shape[1]
    return pl.pallas_call(
        functools.partial(_rank_kernel, tm=tm),
        grid=(m // tm,),
        in_specs=[pl.BlockSpec((8, tm), lambda i: (0, i))],
        out_specs=[pl.BlockSpec((8, tm), lambda i: (0, i)),
                   pl.BlockSpec((N_EXPERTS, LANES), lambda i: (0, 0))],
        out_shape=[jax.ShapeDtypeStruct((8, m), F32),
                   jax.ShapeDtypeStruct((N_EXPERTS, LANES), F32)],
        scratch_shapes=[pltpu.VMEM((N_EXPERTS, LANES), F32)],
        compiler_params=_params(1),
        name="moe_rank",
    )(rt)


def _dispatch_kernel(dest_ref, h_ref, xs_in_ref, xs_ref, sem, *, tm):
    del xs_in_ref
    base = pl.program_id(0) * tm

    def body(r, c):
        for k in range(2):
            d = dest_ref[2 * (base + r) + k]
            pltpu.make_async_copy(h_ref.at[pl.ds(r, 1)], xs_ref.at[pl.ds(d, 1)], sem).start()
        return c

    lax.fori_loop(0, tm, body, 0)
    for _ in range(2):
        pltpu.make_async_copy(h_ref, xs_ref.at[pl.ds(0, tm)], sem).wait()


def _dispatch_call(dest, h, xs0, tm):
    m, d = h.shape
    return pl.pallas_call(
        functools.partial(_dispatch_kernel, tm=tm),
        grid_spec=pltpu.PrefetchScalarGridSpec(
            num_scalar_prefetch=1,
            grid=(m // tm,),
            in_specs=[pl.BlockSpec((tm, d), lambda i, dest_: (i, 0)),
                      pl.BlockSpec(memory_space=pl.ANY)],
            out_specs=pl.BlockSpec(memory_space=pl.ANY),
            scratch_shapes=[pltpu.SemaphoreType.DMA(())]),
        out_shape=jax.ShapeDtypeStruct(xs0.shape, xs0.dtype),
        input_output_aliases={2: 0},
        compiler_params=_params(1),
        name="moe_dispatch",
    )(dest, h, xs0)


def _ffn_kernel(be_ref, nu_ref, xs_ref, wg_ref, wu_ref, wd_ref, ys_ref, wg_s, wu_s, wd_s):
    i = pl.program_id(0)
    new_expert = jnp.logical_or(i == 0, be_ref[i] != be_ref[jnp.maximum(i - 1, 0)])

    @pl.when(new_expert)
    def _():
        wg_s[...] = wg_ref[0].astype(MXU_DTYPE)
        wu_s[...] = wu_ref[0].astype(MXU_DTYPE)
        wd_s[...] = wd_ref[0].astype(MXU_DTYPE)

    @pl.when(i < nu_ref[0])
    def _():
        xb = xs_ref[...].astype(MXU_DTYPE)
        act = jax.nn.silu(_dot(xb, wg_s[...])) * _dot(xb, wu_s[...])
        ys_ref[...] = _dot(act.astype(MXU_DTYPE), wd_s[...])

    @pl.when(i >= nu_ref[0])
    def _():
        ys_ref[...] = jnp.zeros_like(ys_ref)


def _ffn_call(block_e, n_used, xs, wg, wu, wd, n_blocks):
    d = xs.shape[1]
    br = EXPERT_ROWS
    return pl.pallas_call(
        _ffn_kernel,
        grid_spec=pltpu.PrefetchScalarGridSpec(
            num_scalar_prefetch=2,
            grid=(n_blocks,),
            in_specs=[pl.BlockSpec((br, d), lambda i, be, nu: (i, 0)),
                      pl.BlockSpec((1, d, D_EXPERT), lambda i, be, nu: (be[i], 0, 0)),
                      pl.BlockSpec((1, d, D_EXPERT), lambda i, be, nu: (be[i], 0, 0)),
                      pl.BlockSpec((1, D_EXPERT, d), lambda i, be, nu: (be[i], 0, 0))],
            out_specs=pl.BlockSpec((br, d), lambda i, be, nu: (i, 0)),
            scratch_shapes=[pltpu.VMEM((d, D_EXPERT), MXU_DTYPE),
                            pltpu.VMEM((d, D_EXPERT), MXU_DTYPE),
                            pltpu.VMEM((D_EXPERT, d), MXU_DTYPE)]),
        out_shape=jax.ShapeDtypeStruct((n_blocks * br, d), F32),
        compiler_params=_params(1),
        name="moe_experts",
    )(block_e, n_used, xs, wg, wu, wd)


def _combine_kernel(src_ref, h_ref, rc_ref, lg_ref, lb_ref, ys_ref, o_ref, buf, sem, *, tm, n_tiles, alpha):
    i = pl.program_id(0)

    def issue(tile, slot):
        base = tile * tm

        def body(r, c):
            for k in range(2):
                s = src_ref[2 * (base + r) + k]
                pltpu.make_async_copy(ys_ref.at[pl.ds(s, 1)], buf.at[slot, k, pl.ds(r, 1)], sem.at[slot]).start()
            return c

        lax.fori_loop(0, tm, body, 0)

    @pl.when(i == 0)
    def _():
        issue(0, 0)

    @pl.when(i + 1 < n_tiles)
    def _():
        issue(i + 1, (i + 1) % 2)

    slot = i % 2
    for k in range(2):
        pltpu.make_async_copy(ys_ref.at[pl.ds(0, tm)], buf.at[slot, k], sem.at[slot]).wait()
    rc = rc_ref[...]
    real = rc[:, 0:1] < float(N_EXPERTS)
    y = jnp.where(real, rc[:, 2:3] * buf[slot, 0] + rc[:, 3:4] * buf[slot, 1], 0.0)
    o_ref[...] = _layer_norm(alpha * h_ref[...] + y, lg_ref[...], lb_ref[...])


def _combine_call(src, h, rc, lg, lb, ys, alpha, tm):
    m, d = h.shape
    n_tiles = m // tm
    return pl.pallas_call(
        functools.partial(_combine_kernel, tm=tm, n_tiles=n_tiles, alpha=alpha),
        grid_spec=pltpu.PrefetchScalarGridSpec(
            num_scalar_prefetch=1,
            grid=(n_tiles,),
            in_specs=[pl.BlockSpec((tm, d), lambda i, s: (i, 0)),
                      pl.BlockSpec((tm, LANES), lambda i, s: (i, 0)),
                      pl.BlockSpec((1, d), lambda i, s: (0, 0)),
                      pl.BlockSpec((1, d), lambda i, s: (0, 0)),
                      pl.BlockSpec(memory_space=pl.ANY)],
            out_specs=pl.BlockSpec((tm, d), lambda i, s: (i, 0)),
            scratch_shapes=[pltpu.VMEM((2, 2, tm, d), F32),
                            pltpu.SemaphoreType.DMA((2,))]),
        out_shape=jax.ShapeDtypeStruct((m, d), F32),
        compiler_params=_params(1),
        name="moe_combine_ln2",
    )(src, h, rc, lg, lb, ys)


def _moe(h1, w_rg, b_rg, w_re, b_re, w_gate, w_up, w_down, lg, lb, alpha, bsz, lp, tm):
    m, d = h1.shape
    wr = jnp.zeros((LANES, d), F32).at[0:N_EXPERTS].set(w_re.T).at[N_EXPERTS:N_EXPERTS + N_GROUPS].set(w_rg.T)
    br_ = jnp.zeros((LANES,), F32).at[0:N_EXPERTS].set(b_re).at[N_EXPERTS:N_EXPERTS + N_GROUPS].set(b_rg)
    br_ = br_.at[N_EXPERTS + N_GROUPS:N_EXPERTS + 8].set(NEG).reshape(LANES, 1)
    rt, rc = _router_call(h1, wr, br_, tm, lp)
    rk, cnt = _rank_call(rt, tm)

    rows = EXPERT_ROWS
    n_assign = 2 * bsz * (lp - ROW_PAD)
    n_blocks = -(-(n_assign + N_EXPERTS * (rows - 1)) // rows)
    n_slots = n_blocks * rows
    counts = cnt[:, 0].astype(jnp.int32)
    pcounts = (counts + rows - 1) // rows * rows
    pend = jnp.cumsum(pcounts)
    pstart = pend - pcounts
    e = rt[0:2].astype(jnp.int32).T
    rank = rk[0:2].astype(jnp.int32).T
    real = e < N_EXPERTS
    slot = pstart[jnp.minimum(e, N_EXPERTS - 1)] + rank
    tok = jnp.arange(m, dtype=jnp.int32)
    spare = n_slots + 2 * ((tok // lp) * ROW_PAD + tok % lp)[:, None] + jnp.arange(2, dtype=jnp.int32)[None, :]
    dest = jnp.where(real, slot, spare).reshape(-1)
    src = jnp.where(real, slot, 0).reshape(-1)
    block_e = jnp.minimum(jnp.searchsorted(pend, jnp.arange(n_blocks, dtype=jnp.int32) * rows, side="right"),
                          N_EXPERTS - 1).astype(jnp.int32)
    n_used = (pend[-1:] // rows).astype(jnp.int32)

    n_spare = -(-(2 * bsz * ROW_PAD) // rows) * rows
    xs = _dispatch_call(dest, h1, jnp.zeros((n_slots + n_spare, d), F32), tm)
    ys = _ffn_call(block_e, n_used, xs, w_gate, w_up, w_down, n_blocks)
    return _combine_call(src, h1, rc, lg, lb, ys, alpha, _token_tile(m, 384))


def _rope_tables(lp):
    pos = jnp.arange(lp, dtype=F32) - float(ROW_PAD)
    inv = 1.0 / (ROPE_THETA ** (jnp.arange(0, DA_DIM, 2, dtype=F32) / DA_DIM))
    ang = pos[:, None] * inv[None, :]
    reps = LANES // (DA_DIM // 2)
    sign = jnp.tile(jnp.concatenate([-jnp.ones((DA_DIM // 2,), F32), jnp.ones((DA_DIM // 2,), F32)]), LANES // DA_DIM)
    return jnp.tile(jnp.cos(ang), (1, reps)), jnp.tile(jnp.sin(ang), (1, reps)) * sign[None, :]


def _dup_heads(w, n_heads, dim):
    d = w.shape[0]
    return jnp.broadcast_to(w.reshape(d, n_heads, 1, dim), (d, n_heads, 2, dim)).reshape(d, n_heads * 2 * dim)


def kernel(x, meta, ln_in_g, ln_in_b, w_in, conv_w, conv_b, gate_b, lam_q1, lam_k1, lam_q2, lam_k2, diff_g, sink, mlstm_g, w_branch, w_out, ln1_g, ln1_b, ln2_g, ln2_b, w_rg, b_rg, w_re, b_re, w_gate, w_up, w_down):
    bsz, seq, d = x.shape
    depth = w_in.shape[0]
    lp = seq + BLOCK
    m = bsz * lp
    alpha = (2.0 * depth) ** 0.25
    tm = _row_tile(lp, 640)
    t_attn = _row_tile(lp, 640)

    hp = jnp.concatenate([jnp.zeros((bsz, ROW_PAD, d), x.dtype),
                          jnp.broadcast_to(meta.astype(x.dtype)[None], (bsz, N_META_TOK, d)), x], axis=1)
    h = _ln_call(hp.reshape(m, d), ln_in_g, ln_in_b, tm)
    cos, sin = _rope_tables(lp)
    q_scale = DA_DIM ** -0.5
    rope_scale = jnp.concatenate([jnp.full((A_Q,), q_scale, F32), jnp.ones((A_K,), F32),
                                  jnp.full((B_Q,), q_scale, F32), jnp.ones((2 * B_K,), F32)]).reshape(1, -1)
    conv_scale = jnp.concatenate([jnp.ones((C_Q,), F32), jnp.full((C_K,), MC_QK ** -0.5, F32)]).reshape(1, -1)

    for l in range(depth):
        lam_init = 0.8 - 0.6 * math.exp(-0.3 * l)
        wl = w_in[l]
        col = lambda i: wl[:, OFFS[i]:OFFS[i + 1]]
        w_rope = jnp.concatenate([col(0), col(1), col(3), _dup_heads(col(4), WB_KV, WB_DIM)], axis=1).astype(MXU_DTYPE)
        w_val = jnp.concatenate([col(2), col(8), _dup_heads(col(5), WB_KV, WB_DIM)], axis=1).astype(MXU_DTYPE)
        w_conv = jnp.concatenate([col(6), col(7)], axis=1).astype(MXU_DTYPE)
        w_o = col(9).astype(MXU_DTYPE)
        w_g = jnp.pad(col(10), ((0, 0), (0, LANES - C_G))).astype(MXU_DTYPE)
        b_g = jnp.pad(gate_b[l], (0, LANES - C_G)).reshape(1, LANES)
        w_mg = col(11).astype(MXU_DTYPE)

        rq = _proj_rope_call(h, w_rope, cos, sin, rope_scale, tm, lp).reshape(bsz, lp, -1)
        vv = _proj_call(h, w_val, tm, MXU_DTYPE, None, "proj_val").reshape(bsz, lp, -1)
        zc = _proj_call(h, w_conv, tm, F32, None, "proj_conv").reshape(bsz, lp, -1)
        co = _proj_call(h, w_o, tm, F32, "sigmoid", "proj_ogate")
        gc, gr = _gates_call(h, w_g, b_g, tm, lp)

        lamv = jnp.stack([lam_q1[l], lam_k1[l], lam_q2[l], lam_k2[l]])
        out_a = _attn_a_call(rq, vv, lamv, diff_g[l].reshape(-1, 1), lam_init, t_attn, t_attn)
        out_b = _attn_b_call(rq, vv, sink[l].reshape(1, -1))
        qk = _conv_call(zc, conv_w[l], conv_b[l].reshape(1, -1), conv_scale, _row_tile(lp, 640))
        h_f, h_b = _mlstm_call(qk, vv, gc, gr)

        h = _merge_call(h, out_a.reshape(m, -1), out_b.reshape(m, -1), h_f.reshape(m, -1), h_b.reshape(m, -1),
                        co, mlstm_g[l].reshape(1, -1), w_mg, w_branch[l].astype(MXU_DTYPE),
                        w_out[l].astype(MXU_DTYPE), ln1_g[l].reshape(1, -1), ln1_b[l].reshape(1, -1),
                        alpha, _token_tile(m, 384))
        h = _moe(h, w_rg[l], b_rg[l], w_re[l], b_re[l], w_gate[l], w_up[l], w_down[l],
                 ln2_g[l].reshape(1, -1), ln2_b[l].reshape(1, -1), alpha, bsz, lp, tm)
    return h.reshape(bsz, lp, d)[:, BLOCK:]
```

```python
import functools
import math

import numpy as np
import jax
import jax.numpy as jnp
from jax import lax
from jax.experimental import pallas as pl
from jax.experimental.pallas import tpu as pltpu

D_MODEL = 1024
N_META_TOK = 16
BLOCK = 128
ROW_PAD = BLOCK - N_META_TOK
ROPE_THETA = 10000.0
LN_EPS = 1e-5
NEG = -1e30

DA_HEADS = 4
DA_DIM = 64
WB_HEADS = 8
WB_KV = 2
WB_DIM = 64
WINDOW = 128
MC_HEADS = 4
MC_QK = 128
MC_V = 128
N_BRANCH = 3
BRANCH_W = 512
N_GROUPS = 4
EXP_PER_GROUP = 8
N_EXPERTS = N_GROUPS * EXP_PER_GROUP
D_EXPERT = 512

A_Q = DA_HEADS * 2 * DA_DIM
A_K = A_Q
A_V = A_Q
B_Q = WB_HEADS * WB_DIM
B_K = WB_KV * WB_DIM
B_V = B_K
C_Q = MC_HEADS * MC_QK
C_K = C_Q
C_V = MC_HEADS * MC_V
C_O = C_V
C_G = 4 * MC_HEADS
GATE_W = N_BRANCH * D_MODEL
SPLITS = (A_Q, A_K, A_V, B_Q, B_K, B_V, C_Q, C_K, C_V, C_O, C_G, GATE_W)
OFFS = tuple(int(v) for v in np.cumsum((0,) + SPLITS))

LANES = 128
MXU_TILE = 256
EXPERT_ROWS = 256
VMEM_LIMIT = 56 * 1024 * 1024

F32 = jnp.float32
MXU_DTYPE = jnp.bfloat16


def _dot(a, b):
    return jnp.dot(a, b, preferred_element_type=F32)


def _dot_nt(a, b):
    return lax.dot_general(a, b, (((1,), (1,)), ((), ())), preferred_element_type=F32)


def _dot_tn(a, b):
    return lax.dot_general(a, b, (((0,), (0,)), ((), ())), preferred_element_type=F32)


def _params(n_axes):
    return pltpu.CompilerParams(dimension_semantics=("arbitrary",) * n_axes,
                                vmem_limit_bytes=VMEM_LIMIT)


def _row_tile(n_rows, target):
    best = BLOCK
    for t in range(BLOCK, target + 1, BLOCK):
        if n_rows % t == 0:
            best = t
    return best


def _token_tile(n_rows, target):
    best = 8
    for t in range(8, target + 1, 8):
        if n_rows % t == 0:
            best = t
    return best


def _layer_norm(x, g, b):
    mu = jnp.mean(x, axis=-1, keepdims=True)
    xc = x - mu
    var = jnp.mean(xc * xc, axis=-1, keepdims=True)
    return xc * lax.rsqrt(var + LN_EPS) * g + b


def _ln_kernel(x_ref, g_ref, b_ref, o_ref):
    o_ref[...] = _layer_norm(x_ref[...], g_ref[...], b_ref[...])


def _ln_call(x, g, b, tm):
    m, d = x.shape
    return pl.pallas_call(
        _ln_kernel,
        grid=(m // tm,),
        in_specs=[pl.BlockSpec((tm, d), lambda i: (i, 0)),
                  pl.BlockSpec((1, d), lambda i: (0, 0)),
                  pl.BlockSpec((1, d), lambda i: (0, 0))],
        out_specs=pl.BlockSpec((tm, d), lambda i: (i, 0)),
        out_shape=jax.ShapeDtypeStruct((m, d), F32),
        compiler_params=_params(1),
        name="ln_in",
    )(x, g.reshape(1, d), b.reshape(1, d))


def _proj_rope_kernel(x_ref, w_ref, cos_ref, sin_ref, scale_ref, o_ref):
    z = _dot(x_ref[...].astype(MXU_DTYPE), w_ref[...])
    cos = cos_ref[...]
    sin = sin_ref[...]
    lane = lax.broadcasted_iota(jnp.int32, cos.shape, 1)
    first_half = (lane % DA_DIM) < (DA_DIM // 2)
    for c in range(z.shape[1] // LANES):
        sl = slice(c * LANES, (c + 1) * LANES)
        zc = z[:, sl]
        partner = jnp.where(first_half, pltpu.roll(zc, LANES - DA_DIM // 2, 1),
                            pltpu.roll(zc, DA_DIM // 2, 1))
        o_ref[:, sl] = ((zc * cos + partner * sin) * scale_ref[:, sl]).astype(o_ref.dtype)


def _proj_rope_call(h, w, cos, sin, scale, tm, lp):
    m, d = h.shape
    n = w.shape[1]
    per_batch = lp // tm
    return pl.pallas_call(
        _proj_rope_kernel,
        grid=(m // tm,),
        in_specs=[pl.BlockSpec((tm, d), lambda i: (i, 0)),
                  pl.BlockSpec((d, n), lambda i: (0, 0)),
                  pl.BlockSpec((tm, LANES), lambda i: (i % per_batch, 0)),
                  pl.BlockSpec((tm, LANES), lambda i: (i % per_batch, 0)),
                  pl.BlockSpec((1, n), lambda i: (0, 0))],
        out_specs=pl.BlockSpec((tm, n), lambda i: (i, 0)),
        out_shape=jax.ShapeDtypeStruct((m, n), MXU_DTYPE),
        compiler_params=_params(1),
        name="proj_rope",
    )(h, w, cos, sin, scale)


def _proj_kernel(x_ref, w_ref, o_ref, *, act):
    z = _dot(x_ref[...].astype(MXU_DTYPE), w_ref[...])
    if act == "sigmoid":
        z = jax.nn.sigmoid(z)
    o_ref[...] = z.astype(o_ref.dtype)


def _proj_call(h, w, tm, out_dtype, act, name):
    m, d = h.shape
    n = w.shape[1]
    return pl.pallas_call(
        functools.partial(_proj_kernel, act=act),
        grid=(m // tm,),
        in_specs=[pl.BlockSpec((tm, d), lambda i: (i, 0)),
                  pl.BlockSpec((d, n), lambda i: (0, 0))],
        out_specs=pl.BlockSpec((tm, n), lambda i: (i, 0)),
        out_shape=jax.ShapeDtypeStruct((m, n), out_dtype),
        compiler_params=_params(1),
        name=name,
    )(h, w)


def _gates_kernel(x_ref, w_ref, b_ref, gc_ref, gr_ref, *, tm, lp, l_end):
    z = _dot(x_ref[...].astype(MXU_DTYPE), w_ref[...]) + b_ref[...]
    lane = lax.broadcasted_iota(jnp.int32, (tm, LANES), 1)
    kind = lane // MC_HEADS
    row = lax.broadcasted_iota(jnp.int32, (tm, LANES), 0) + pl.program_id(0) * tm
    pos = row % lp
    unused = (pos < ROW_PAD) | (pos >= l_end)
    log_f = jnp.minimum(z, 0.0) - jnp.log1p(jnp.exp(-jnp.abs(z)))
    is_forget = (kind % 2) == 1
    base = jnp.where(is_forget, jnp.where(unused, 0.0, log_f), jnp.where(unused, NEG, z))
    r128 = lax.broadcasted_iota(jnp.int32, (BLOCK, LANES), 0)
    fwd_lane = lax.broadcasted_iota(jnp.int32, (BLOCK, LANES), 1) // MC_HEADS == 1
    forget128 = (lax.broadcasted_iota(jnp.int32, (BLOCK, LANES), 1) // MC_HEADS) % 2 == 1
    for c in range(tm // BLOCK):
        x = base[c * BLOCK:(c + 1) * BLOCK]
        pre = x
        suf = x
        s = 1
        while s < BLOCK:
            pre = pre + jnp.where(r128 >= s, pltpu.roll(pre, s, 0), 0.0)
            suf = suf + jnp.where(r128 < BLOCK - s, pltpu.roll(suf, BLOCK - s, 0), 0.0)
            s *= 2
        out = jnp.where(forget128, jnp.where(fwd_lane, pre, suf), x)
        gc_ref[c * BLOCK:(c + 1) * BLOCK, :] = out
        gr_ref[:, c * BLOCK:(c + 1) * BLOCK] = out.T[0:C_G, :]


def _gates_call(h, w, b, tm, lp, l_end):
    m, d = h.shape
    return pl.pallas_call(
        functools.partial(_gates_kernel, tm=tm, lp=lp, l_end=l_end),
        grid=(m // tm,),
        in_specs=[pl.BlockSpec((tm, d), lambda i: (i, 0)),
                  pl.BlockSpec((d, LANES), lambda i: (0, 0)),
                  pl.BlockSpec((1, LANES), lambda i: (0, 0))],
        out_specs=[pl.BlockSpec((tm, LANES), lambda i: (i, 0)),
                   pl.BlockSpec((C_G, tm), lambda i: (0, i))],
        out_shape=[jax.ShapeDtypeStruct((m, LANES), F32),
                   jax.ShapeDtypeStruct((C_G, m), F32)],
        compiler_params=_params(1),
        name="mlstm_gates",
    )(h, w, b)


def _conv_kernel(z_ref, w_ref, b_ref, scale_ref, o_ref, *, lp, l_end, tr):
    w0 = w_ref[0:1, :]
    w1 = w_ref[1:2, :]
    w2 = w_ref[2:3, :]
    row = lax.broadcasted_iota(jnp.int32, (tr, LANES), 0)
    for c in range(lp // tr):
        r0 = c * tr
        zc = z_ref[0, r0:r0 + tr, :]
        before = jnp.zeros((1, LANES), F32) if r0 == 0 else z_ref[0, r0 - 1:r0, :]
        after = jnp.zeros((1, LANES), F32) if r0 + tr == lp else z_ref[0, r0 + tr:r0 + tr + 1, :]
        prev = jnp.where(row == 0, before, pltpu.roll(zc, 1, 0))
        nxt = jnp.where(row == tr - 1, after, pltpu.roll(zc, tr - 1, 0))
        if r0 <= ROW_PAD < r0 + tr:
            prev = jnp.where(row == ROW_PAD - r0, 0.0, prev)
        if r0 <= l_end - 1 < r0 + tr:
            nxt = jnp.where(row == l_end - 1 - r0, 0.0, nxt)
        y = prev * w0 + zc * w1 + nxt * w2 + b_ref[...]
        o_ref[0, r0:r0 + tr, :] = (jax.nn.silu(y) * scale_ref[...]).astype(o_ref.dtype)


def _conv_call(z, w, b, scale, l_end, tr):
    bsz, lp, n = z.shape
    return pl.pallas_call(
        functools.partial(_conv_kernel, lp=lp, l_end=l_end, tr=tr),
        grid=(bsz, n // LANES),
        in_specs=[pl.BlockSpec((1, lp, LANES), lambda b_, j: (b_, 0, j)),
                  pl.BlockSpec((3, LANES), lambda b_, j: (0, j)),
                  pl.BlockSpec((1, LANES), lambda b_, j: (0, j)),
                  pl.BlockSpec((1, LANES), lambda b_, j: (0, j))],
        out_specs=pl.BlockSpec((1, lp, LANES), lambda b_, j: (b_, 0, j)),
        out_shape=jax.ShapeDtypeStruct((bsz, lp, n), MXU_DTYPE),
        compiler_params=_params(2),
        name="mlstm_conv",
    )(z, w, b, scale)


def _attn_a_kernel(lamv_ref, g_ref, q_ref, k_ref, v_ref, o_ref, s0_scr, s1_scr, acc_scr, *, tk, n_chunks, l_end,
                   lam_init):
    q = q_ref[0]
    tq = q.shape[0]
    lane = lax.broadcasted_iota(jnp.int32, q.shape, 1)
    key_row = lax.broadcasted_iota(jnp.int32, (tk, tq), 0)
    zero = jnp.zeros_like(q)
    qz = (jnp.where(lane < DA_DIM, q, zero), jnp.where(lane >= DA_DIM, q, zero))

    s_bufs = (s0_scr, s1_scr)

    def scores(j, slot):
        start = pl.multiple_of(j * tk, tk)
        kj = k_ref[0, pl.ds(start, tk), :]
        for c in range(2):
            s_bufs[slot][c] = _dot_nt(kj, qz[c])

    def softmax_pv(j, slot, carry, lo, hi):
        start = pl.multiple_of(j * tk, tk)
        vj = v_ref[0, pl.ds(start, tk), :]
        out = []
        for c in range(2):
            m, l = carry[c]
            s = s_bufs[slot][c]
            if lo is not None:
                s = jnp.where(key_row >= lo, s, NEG)
            if hi is not None:
                s = jnp.where(key_row < hi, s, NEG)
            m_new = jnp.maximum(m, jnp.max(s, axis=0, keepdims=True))
            alpha = jnp.exp(m - m_new)
            p = jnp.exp(s - m_new)
            l = alpha * l + jnp.sum(p, axis=0, keepdims=True)
            acc_scr[c] = alpha * acc_scr[c] + _dot_tn(vj, p.astype(MXU_DTYPE))
            out.append((m_new, l))
        return tuple(out)

    one = (jnp.full((1, tq), NEG, F32), jnp.zeros((1, tq), F32))
    acc_scr[...] = jnp.zeros_like(acc_scr)
    last_hi = l_end - (n_chunks - 1) * tk
    last_hi = None if last_hi == tk else last_hi
    scores(0, 0)
    if n_chunks == 1:
        carry = softmax_pv(0, 0, (one, one), ROW_PAD, last_hi)
    else:
        def step(j, parity, carry, lo):
            scores(j + 1, 1 - parity)
            return softmax_pv(j, parity, carry, lo, None)

        carry = step(0, 0, (one, one), ROW_PAD)
        n_pairs = (n_chunks - 2) // 2

        def pair(i, cr):
            j = 1 + 2 * i
            return step(j + 1, 0, step(j, 1, cr, None), None)

        carry = lax.fori_loop(0, n_pairs, pair, carry)
        if (n_chunks - 2) % 2:
            carry = step(n_chunks - 2, (n_chunks - 2) % 2, carry, None)
        carry = softmax_pv(n_chunks - 1, (n_chunks - 1) % 2, carry, None, last_hi)
    o0 = acc_scr[0] / carry[0][1]
    o1 = acc_scr[1] / carry[1][1]
    lv = lamv_ref[...]
    lam = (jnp.exp(jnp.sum(lv[0:1] * lv[1:2], axis=-1, keepdims=True))
           - jnp.exp(jnp.sum(lv[2:3] * lv[3:4], axis=-1, keepdims=True)) + lam_init)
    o = o0 - lam * o1
    ms = jnp.mean(o * o, axis=0, keepdims=True)
    o = o * lax.rsqrt(ms + LN_EPS) * g_ref[...] * (1.0 - lam_init)
    o_ref[0] = o.T.astype(o_ref.dtype)


def _attn_a_call(rq, vv, lamv, g_col, lam_init, l_end, tq, tk):
    bsz, lp, _ = rq.shape
    k_blk = A_Q // LANES
    return pl.pallas_call(
        functools.partial(_attn_a_kernel, tk=tk, n_chunks=lp // tk, l_end=l_end, lam_init=lam_init),
        grid=(bsz, DA_HEADS, lp // tq),
        in_specs=[pl.BlockSpec((4, DA_DIM), lambda b, h, i: (0, 0)),
                  pl.BlockSpec((2 * DA_DIM, 1), lambda b, h, i: (0, 0)),
                  pl.BlockSpec((1, tq, LANES), lambda b, h, i: (b, i, h)),
                  pl.BlockSpec((1, lp, LANES), lambda b, h, i: (b, 0, k_blk + h)),
                  pl.BlockSpec((1, lp, LANES), lambda b, h, i: (b, 0, h))],
        out_specs=pl.BlockSpec((1, tq, LANES), lambda b, h, i: (b, i, h)),
        out_shape=jax.ShapeDtypeStruct((bsz, lp, A_V), MXU_DTYPE),
        scratch_shapes=[pltpu.VMEM((2, tk, tq), F32),
                        pltpu.VMEM((2, tk, tq), F32),
                        pltpu.VMEM((2, 2 * DA_DIM, tq), F32)],
        compiler_params=_params(3),
        name="diff_attn",
    )(lamv, g_col, rq, rq, vv)


def _attn_b_kernel(sink_ref, q_ref, k0_ref, k1_ref, v0_ref, v1_ref, o_ref, *, lp, l_end):
    n = pl.program_id(1)
    nb = lp // BLOCK
    grp = WB_HEADS // WB_KV

    def blocks(ref):
        parts = [ref[0, 0:BLOCK, :]]
        for d in (-1, 0, 1):
            idx = jnp.clip(n + d, 0, nb - 1)
            parts.append(ref[0, pl.ds(pl.multiple_of(idx * BLOCK, BLOCK), BLOCK), :])
        return jnp.concatenate(parts, axis=0)

    keys = (blocks(k0_ref), blocks(k1_ref))
    vals = (blocks(v0_ref), blocks(v1_ref))
    shape = (4 * BLOCK, grp * BLOCK)
    krow = lax.broadcasted_iota(jnp.int32, shape, 0)
    qcol = lax.broadcasted_iota(jnp.int32, shape, 1)
    qpos = qcol % BLOCK + n * BLOCK
    kpos = (n - 2) * BLOCK + krow
    band_ok = (kpos >= BLOCK) & (kpos < l_end) & (jnp.abs(qpos - kpos) <= WINDOW)
    valid = ((krow >= ROW_PAD) & (krow < BLOCK)) | ((krow >= BLOCK) & band_ok)
    head_of_col = lax.broadcasted_iota(jnp.int32, (1, grp * BLOCK), 1) // BLOCK
    lane = lax.broadcasted_iota(jnp.int32, (BLOCK, LANES), 1)
    for g in range(WB_KV):
        qs = []
        sink = jnp.zeros((1, grp * BLOCK), F32)
        for j in range(grp):
            h = g * grp + j
            qt = q_ref[0, :, (h // 2) * LANES:(h // 2 + 1) * LANES]
            keep = (lane >= WB_DIM) if h % 2 else (lane < WB_DIM)
            qs.append(jnp.where(keep, qt, jnp.zeros_like(qt)))
            sink = jnp.where(head_of_col == j, sink_ref[:, h:h + 1], sink)
        s = jnp.where(valid, _dot_nt(keys[g], jnp.concatenate(qs, axis=0)), NEG)
        m = jnp.maximum(jnp.max(s, axis=0, keepdims=True), sink)
        p = jnp.exp(s - m)
        den = jnp.sum(p, axis=0, keepdims=True) + jnp.exp(sink - m)
        o = (_dot_tn(vals[g], p.astype(MXU_DTYPE)) / den).T
        for jj in range(grp // 2):
            lo = o[(2 * jj) * BLOCK:(2 * jj + 1) * BLOCK]
            hi = o[(2 * jj + 1) * BLOCK:(2 * jj + 2) * BLOCK]
            t = (g * grp) // 2 + jj
            o_ref[0, :, t * LANES:(t + 1) * LANES] = jnp.where(lane < WB_DIM, lo, hi).astype(o_ref.dtype)


def _attn_b_call(rq, vv, sink, l_end):
    bsz, lp, _ = rq.shape
    q_blk = (A_Q + A_K) // B_Q
    k_blk = (A_Q + A_K + B_Q) // LANES
    v_blk = (A_V + C_V) // LANES
    seq = lambda c: pl.BlockSpec((1, lp, LANES), lambda b, n: (b, 0, c))
    return pl.pallas_call(
        functools.partial(_attn_b_kernel, lp=lp, l_end=l_end),
        grid=(bsz, lp // BLOCK),
        in_specs=[pl.BlockSpec((1, WB_HEADS), lambda b, n: (0, 0)),
                  pl.BlockSpec((1, BLOCK, B_Q), lambda b, n: (b, n, q_blk)),
                  seq(k_blk), seq(k_blk + 1), seq(v_blk), seq(v_blk + 1)],
        out_specs=pl.BlockSpec((1, BLOCK, B_Q), lambda b, n: (b, n, 0)),
        out_shape=jax.ShapeDtypeStruct((bsz, lp, B_Q), MXU_DTYPE),
        compiler_params=_params(2),
        name="window_attn",
    )(sink, rq, rq, rq, vv, vv)


def _mlstm_kernel(qkf_ref, vf_ref, gcf_ref, grf_ref, qkb_ref, vb_ref, gcb_ref, grb_ref,
                  hf_ref, hb_ref, c_scr, m_scr):
    t = pl.program_id(1)

    @pl.when(t == 0)
    def _():
        c_scr[...] = jnp.zeros_like(c_scr)
        m_scr[...] = jnp.zeros_like(m_scr)

    row = lax.broadcasted_iota(jnp.int32, (BLOCK, BLOCK), 0)
    col = lax.broadcasted_iota(jnp.int32, (BLOCK, BLOCK), 1)
    ones_col = jnp.where(col == 0, 1.0, 0.0).astype(MXU_DTYPE)
    dirs = ((qkf_ref, vf_ref, gcf_ref, grf_ref, hf_ref, row >= col, BLOCK - 1),
            (qkb_ref, vb_ref, gcb_ref, grb_ref, hb_ref, row <= col, 0))
    for d, (qk_ref, v_ref, gc_ref, gr_ref, h_ref, tri, last) in enumerate(dirs):
        for hd in range(MC_HEADS):
            ci = d * MC_HEADS + hd
            j_li = (2 * d) * MC_HEADS + hd
            j_b = (2 * d + 1) * MC_HEADS + hd
            q = qk_ref[0, :, hd * MC_QK:(hd + 1) * MC_QK]
            k = qk_ref[0, :, C_Q + hd * MC_QK:C_Q + (hd + 1) * MC_QK]
            v = v_ref[0, :, hd * MC_V:(hd + 1) * MC_V]
            vext = jnp.concatenate([v, ones_col], axis=1)
            li_col = gc_ref[:, j_li:j_li + 1]
            b_col = gc_ref[:, j_b:j_b + 1]
            li_row = gr_ref[j_li:j_li + 1, :]
            b_row = gr_ref[j_b:j_b + 1, :]
            g = b_row[:, last:last + 1]
            m_prev = m_scr[ci, 0:1, 0:1]
            c_prev = c_scr[ci]

            dmat = jnp.where(tri, b_col - b_row + li_row, NEG)
            m_t = jnp.maximum(b_col + m_prev, jnp.max(dmat, axis=-1, keepdims=True))
            inter = jnp.exp(b_col + m_prev - m_t)
            s = (_dot_nt(q, k) * jnp.exp(dmat - m_t)).astype(MXU_DTYPE)
            nd = inter * _dot(q, c_prev.astype(MXU_DTYPE)) + _dot(s, vext)
            den = nd[:, MC_V:MC_V + 1]
            h_ref[0, :, hd * MC_V:(hd + 1) * MC_V] = nd[:, 0:MC_V] / jnp.maximum(jnp.abs(den), jnp.exp(-m_t))

            a_col = g - b_col + li_col
            m_new = jnp.maximum(g + m_prev, jnp.max(a_col, axis=0, keepdims=True))
            decay = jnp.exp(g + m_prev - m_new)
            kw = (jnp.exp(a_col - m_new) * k.astype(F32)).astype(MXU_DTYPE)
            c_scr[ci] = decay * c_prev + _dot_tn(kw, vext)
            m_scr[ci] = jnp.broadcast_to(m_new, m_scr.shape[1:])


def _mlstm_call(qk, vv, gc, gr):
    bsz, lp, _ = qk.shape
    nch = lp // BLOCK
    v_blk = A_V // C_V
    fwd = lambda b, t: (b, t, 0)
    bwd = lambda b, t: (b, nch - 1 - t, 0)
    return pl.pallas_call(
        _mlstm_kernel,
        grid=(bsz, nch),
        in_specs=[pl.BlockSpec((1, BLOCK, C_Q + C_K), fwd),
                  pl.BlockSpec((1, BLOCK, C_V), lambda b, t: (b, t, v_blk)),
                  pl.BlockSpec((BLOCK, LANES), lambda b, t: (b * nch + t, 0)),
                  pl.BlockSpec((C_G, BLOCK), lambda b, t: (0, b * nch + t)),
                  pl.BlockSpec((1, BLOCK, C_Q + C_K), bwd),
                  pl.BlockSpec((1, BLOCK, C_V), lambda b, t: (b, nch - 1 - t, v_blk)),
                  pl.BlockSpec((BLOCK, LANES), lambda b, t: (b * nch + nch - 1 - t, 0)),
                  pl.BlockSpec((C_G, BLOCK), lambda b, t: (0, b * nch + nch - 1 - t))],
        out_specs=[pl.BlockSpec((1, BLOCK, C_V), fwd),
                   pl.BlockSpec((1, BLOCK, C_V), bwd)],
        out_shape=[jax.ShapeDtypeStruct((bsz, lp, C_V), F32),
                   jax.ShapeDtypeStruct((bsz, lp, C_V), F32)],
        scratch_shapes=[pltpu.VMEM((2 * MC_HEADS, MC_QK, 2 * MC_V), F32),
                        pltpu.VMEM((2 * MC_HEADS, 8, LANES), F32)],
        compiler_params=_params(2),
        name="mlstm_scan",
    )(qk, vv, gc, gr, qk, vv, gc, gr)


def _merge_kernel(h_ref, oa_ref, ob_ref, hf_ref, hb_ref, co_ref, mg_ref, wg_ref, wb_ref, wo_ref,
                  lg_ref, lb_ref, o_ref, *, alpha):
    h = h_ref[...]
    hx = h.astype(MXU_DTYPE)
    hc = hf_ref[...] + hb_ref[...]
    parts = []
    for hd in range(MC_HEADS):
        sl = slice(hd * MC_V, (hd + 1) * MC_V)
        x = hc[:, sl]
        mu = jnp.mean(x, axis=-1, keepdims=True)
        xc = x - mu
        var = jnp.mean(xc * xc, axis=-1, keepdims=True)
        parts.append(xc * lax.rsqrt(var + LN_EPS) * mg_ref[:, sl] * co_ref[:, sl])
    oc = jnp.concatenate(parts, axis=1).astype(MXU_DTYPE)
    branches = (oa_ref[...], ob_ref[...], oc)
    merged = None
    for br in range(N_BRANCH):
        gate = jax.nn.sigmoid(_dot(hx, wg_ref[:, br * D_MODEL:(br + 1) * D_MODEL]))
        term = gate * _dot(branches[br], wb_ref[br])
        merged = term if merged is None else merged + term
    y = _dot(merged.astype(MXU_DTYPE), wo_ref[...])
    o_ref[...] = _layer_norm(alpha * h + y, lg_ref[...], lb_ref[...])


def _merge_call(h, oa, ob, hf, hb, co, mg, wg, wb, wo, lg, lb, alpha, tm):
    m, d = h.shape
    rows = lambda n: pl.BlockSpec((tm, n), lambda i: (i, 0))
    full2 = lambda a: pl.BlockSpec(a.shape, lambda i: (0, 0))
    return pl.pallas_call(
        functools.partial(_merge_kernel, alpha=alpha),
        grid=(m // tm,),
        in_specs=[rows(d), rows(A_V), rows(B_Q), rows(C_V), rows(C_V), rows(C_O),
                  full2(mg), full2(wg), pl.BlockSpec(wb.shape, lambda i: (0, 0, 0)), full2(wo),
                  full2(lg), full2(lb)],
        out_specs=rows(d),
        out_shape=jax.ShapeDtypeStruct((m, d), F32),
        compiler_params=_params(1),
        name="merge_ln1",
    )(h, oa, ob, hf, hb, co, mg, wg, wb, wo, lg, lb)


def _split3(x):
    hi = x.astype(MXU_DTYPE)
    lo = (x - hi.astype(F32)).astype(MXU_DTYPE)
    return hi, lo


def _router_kernel(h_ref, w_ref, b_ref, rt_ref, rc_ref, *, tm, lp, l_end):
    x_hi, x_lo = _split3(h_ref[...])
    w_hi, w_lo = _split3(w_ref[...])
    logits = (_dot_nt(w_hi, x_hi) + _dot_nt(w_hi, x_lo) + _dot_nt(w_lo, x_hi)) + b_ref[...]
    none = float(N_EXPERTS)
    gl = logits[N_EXPERTS:N_EXPERTS + 8]
    grow = lax.broadcasted_iota(jnp.int32, gl.shape, 0).astype(F32)
    gmax = jnp.max(gl, axis=0, keepdims=True)
    g_sel = jnp.min(jnp.where(gl == gmax, grow, none), axis=0, keepdims=True)
    p_grp = 1.0 / jnp.sum(jnp.exp(gl - gmax), axis=0, keepdims=True)
    el = logits[0:N_EXPERTS]
    erow_i = lax.broadcasted_iota(jnp.int32, el.shape, 0)
    erow = erow_i.astype(F32)
    cand = jnp.where((erow_i // EXP_PER_GROUP).astype(F32) == g_sel, el, -jnp.inf)
    top1 = jnp.max(cand, axis=0, keepdims=True)
    i1 = jnp.min(jnp.where(cand == top1, erow, none), axis=0, keepdims=True)
    cand2 = jnp.where(erow == i1, -jnp.inf, cand)
    top2 = jnp.max(cand2, axis=0, keepdims=True)
    i2 = jnp.min(jnp.where(cand2 == top2, erow, none), axis=0, keepdims=True)
    e = jnp.exp(top2 - top1)
    w1 = (1.0 / (1.0 + e)) * p_grp
    w2 = (e / (1.0 + e)) * p_grp
    pos = lax.broadcasted_iota(jnp.int32, (1, tm), 1) + pl.program_id(0) * tm
    real = ((pos % lp) >= ROW_PAD) & ((pos % lp) < l_end)
    e1 = jnp.where(real, i1, none)
    e2 = jnp.where(real, i2, none)
    r = lax.broadcasted_iota(jnp.int32, (LANES, tm), 0)
    table = jnp.where(r == 0, e1, jnp.where(r == 1, e2, jnp.where(r == 2, w1, jnp.where(r == 3, w2, 0.0))))
    rt_ref[...] = table[0:8]
    rc_ref[...] = table.T


def _router_call(h, w, b, tm, lp, l_end):
    m, d = h.shape
    return pl.pallas_call(
        functools.partial(_router_kernel, tm=tm, lp=lp, l_end=l_end),
        grid=(m // tm,),
        in_specs=[pl.BlockSpec((tm, d), lambda i: (i, 0)),
                  pl.BlockSpec((LANES, d), lambda i: (0, 0)),
                  pl.BlockSpec((LANES, 1), lambda i: (0, 0))],
        out_specs=[pl.BlockSpec((8, tm), lambda i: (0, i)),
                   pl.BlockSpec((tm, LANES), lambda i: (i, 0))],
        out_shape=[jax.ShapeDtypeStruct((8, m), F32),
                   jax.ShapeDtypeStruct((m, LANES), F32)],
        compiler_params=_params(1),
        name="moe_router",
    )(h, w, b)


def _rank_kernel(rt_ref, rk_ref, cnt_ref, carry, *, tm):
    @pl.when(pl.program_id(0) == 0)
    def _():
        carry[...] = jnp.zeros_like(carry)

    erow = lax.broadcasted_iota(jnp.int32, (N_EXPERTS, tm), 0).astype(F32)
    oh1 = jnp.where(erow == rt_ref[0:1, :], 1.0, 0.0)
    oh2 = jnp.where(erow == rt_ref[1:2, :], 1.0, 0.0)
    oh = oh1 + oh2
    earlier = (lax.broadcasted_iota(jnp.int32, (tm, tm), 0)
               < lax.broadcasted_iota(jnp.int32, (tm, tm), 1))
    before = _dot(oh.astype(MXU_DTYPE), jnp.where(earlier, 1.0, 0.0).astype(MXU_DTYPE)) + carry[:, 0:1]
    r1 = jnp.sum(oh1 * before, axis=0, keepdims=True)
    r2 = jnp.sum(oh2 * before, axis=0, keepdims=True)
    r = lax.broadcasted_iota(jnp.int32, (8, tm), 0)
    rk_ref[...] = jnp.where(r == 0, r1, jnp.where(r == 1, r2, 0.0))
    total = carry[...] + jnp.sum(oh, axis=1, keepdims=True)
    carry[...] = total
    cnt_ref[...] = total


def _rank_call(rt, tm):
    m = rt.shape[1]
    return pl.pallas_call(
        functools.partial(_rank_kernel, tm=tm),
        grid=(m // tm,),
        in_specs=[pl.BlockSpec((8, tm), lambda i: (0, i))],
        out_specs=[pl.BlockSpec((8, tm), lambda i: (0, i)),
                   pl.BlockSpec((N_EXPERTS, LANES), lambda i: (0, 0))],
        out_shape=[jax.ShapeDtypeStruct((8, m), F32),
                   jax.ShapeDtypeStruct((N_EXPERTS, LANES), F32)],
        scratch_shapes=[pltpu.VMEM((N_EXPERTS, LANES), F32)],
        compiler_params=_params(1),
        name="moe_rank",
    )(rt)


def _dispatch_kernel(dest_ref, h_ref, xs_in_ref, xs_ref, sem, *, tm):
    del xs_in_ref
    base = pl.program_id(0) * tm

    def body(r, c):
        for k in range(2):
            d = dest_ref[2 * (base + r) + k]
            pltpu.make_async_copy(h_ref.at[pl.ds(r, 1)], xs_ref.at[pl.ds(d, 1)], sem).start()
        return c

    lax.fori_loop(0, tm, body, 0)
    for _ in range(2):
        pltpu.make_async_copy(h_ref, xs_ref.at[pl.ds(0, tm)], sem).wait()


def _dispatch_call(dest, h, xs0, tm):
    m, d = h.shape
    return pl.pallas_call(
        functools.partial(_dispatch_kernel, tm=tm),
        grid_spec=pltpu.PrefetchScalarGridSpec(
            num_scalar_prefetch=1,
            grid=(m // tm,),
            in_specs=[pl.BlockSpec((tm, d), lambda i, dest_: (i, 0)),
                      pl.BlockSpec(memory_space=pl.ANY)],
            out_specs=pl.BlockSpec(memory_space=pl.ANY),
            scratch_shapes=[pltpu.SemaphoreType.DMA(())]),
        out_shape=jax.ShapeDtypeStruct(xs0.shape, xs0.dtype),
        input_output_aliases={2: 0},
        compiler_params=_params(1),
        name="moe_dispatch",
    )(dest, h, xs0)


def _ffn_kernel(be_ref, nu_ref, xs_ref, wg_ref, wu_ref, wd_ref, ys_ref, wg_s, wu_s, wd_s):
    i = pl.program_id(0)
    new_expert = jnp.logical_or(i == 0, be_ref[i] != be_ref[jnp.maximum(i - 1, 0)])

    @pl.when(new_expert)
    def _():
        wg_s[...] = wg_ref[0].astype(MXU_DTYPE)
        wu_s[...] = wu_ref[0].astype(MXU_DTYPE)
        wd_s[...] = wd_ref[0].astype(MXU_DTYPE)

    @pl.when(i < nu_ref[0])
    def _():
        xb = xs_ref[...].astype(MXU_DTYPE)
        act = jax.nn.silu(_dot(xb, wg_s[...])) * _dot(xb, wu_s[...])
        ys_ref[...] = _dot(act.astype(MXU_DTYPE), wd_s[...])

    @pl.when(i >= nu_ref[0])
    def _():
        ys_ref[...] = jnp.zeros_like(ys_ref)


def _ffn_call(block_e, n_used, xs, wg, wu, wd, n_blocks):
    d = xs.shape[1]
    br = EXPERT_ROWS
    return pl.pallas_call(
        _ffn_kernel,
        grid_spec=pltpu.PrefetchScalarGridSpec(
            num_scalar_prefetch=2,
            grid=(n_blocks,),
            in_specs=[pl.BlockSpec((br, d), lambda i, be, nu: (i, 0)),
                      pl.BlockSpec((1, d, D_EXPERT), lambda i, be, nu: (be[i], 0, 0)),
                      pl.BlockSpec((1, d, D_EXPERT), lambda i, be, nu: (be[i], 0, 0)),
                      pl.BlockSpec((1, D_EXPERT, d), lambda i, be, nu: (be[i], 0, 0))],
            out_specs=pl.BlockSpec((br, d), lambda i, be, nu: (i, 0)),
            scratch_shapes=[pltpu.VMEM((d, D_EXPERT), MXU_DTYPE),
                            pltpu.VMEM((d, D_EXPERT), MXU_DTYPE),
                            pltpu.VMEM((D_EXPERT, d), MXU_DTYPE)]),
        out_shape=jax.ShapeDtypeStruct((n_blocks * br, d), F32),
        compiler_params=_params(1),
        name="moe_experts",
    )(block_e, n_used, xs, wg, wu, wd)


def _combine_kernel(src_ref, h_ref, rc_ref, lg_ref, lb_ref, ys_ref, o_ref, buf, sem, *, tm, n_tiles, alpha):
    i = pl.program_id(0)

    def issue(tile, slot):
        base = tile * tm

        def body(r, c):
            for k in range(2):
                s = src_ref[2 * (base + r) + k]
                pltpu.make_async_copy(ys_ref.at[pl.ds(s, 1)], buf.at[slot, k, pl.ds(r, 1)], sem.at[slot]).start()
            return c

        lax.fori_loop(0, tm, body, 0)

    @pl.when(i == 0)
    def _():
        issue(0, 0)

    @pl.when(i + 1 < n_tiles)
    def _():
        issue(i + 1, (i + 1) % 2)

    slot = i % 2
    for k in range(2):
        pltpu.make_async_copy(ys_ref.at[pl.ds(0, tm)], buf.at[slot, k], sem.at[slot]).wait()
    rc = rc_ref[...]
    real = rc[:, 0:1] < float(N_EXPERTS)
    y = jnp.where(real, rc[:, 2:3] * buf[slot, 0] + rc[:, 3:4] * buf[slot, 1], 0.0)
    o_ref[...] = _layer_norm(alpha * h_ref[...] + y, lg_ref[...], lb_ref[...])


def _combine_call(src, h, rc, lg, lb, ys, alpha, tm):
    m, d = h.shape
    n_tiles = m // tm
    return pl.pallas_call(
        functools.partial(_combine_kernel, tm=tm, n_tiles=n_tiles, alpha=alpha),
        grid_spec=pltpu.PrefetchScalarGridSpec(
            num_scalar_prefetch=1,
            grid=(n_tiles,),
            in_specs=[pl.BlockSpec((tm, d), lambda i, s: (i, 0)),
                      pl.BlockSpec((tm, LANES), lambda i, s: (i, 0)),
                      pl.BlockSpec((1, d), lambda i, s: (0, 0)),
                      pl.BlockSpec((1, d), lambda i, s: (0, 0)),
                      pl.BlockSpec(memory_space=pl.ANY)],
            out_specs=pl.BlockSpec((tm, d), lambda i, s: (i, 0)),
            scratch_shapes=[pltpu.VMEM((2, 2, tm, d), F32),
                            pltpu.SemaphoreType.DMA((2,))]),
        out_shape=jax.ShapeDtypeStruct((m, d), F32),
        compiler_params=_params(1),
        name="moe_combine_ln2",
    )(src, h, rc, lg, lb, ys)


def _moe(h1, w_rg, b_rg, w_re, b_re, w_gate, w_up, w_down, lg, lb, alpha, bsz, lp, l_end, tm):
    m, d = h1.shape
    wr = jnp.zeros((LANES, d), F32).at[0:N_EXPERTS].set(w_re.T).at[N_EXPERTS:N_EXPERTS + N_GROUPS].set(w_rg.T)
    br_ = jnp.zeros((LANES,), F32).at[0:N_EXPERTS].set(b_re).at[N_EXPERTS:N_EXPERTS + N_GROUPS].set(b_rg)
    br_ = br_.at[N_EXPERTS + N_GROUPS:N_EXPERTS + 8].set(NEG).reshape(LANES, 1)
    rt, rc = _router_call(h1, wr, br_, tm, lp, l_end)
    rk, cnt = _rank_call(rt, tm)

    rows = EXPERT_ROWS
    n_assign = 2 * bsz * (l_end - ROW_PAD)
    n_unused = lp - (l_end - ROW_PAD)
    n_blocks = -(-(n_assign + N_EXPERTS * (rows - 1)) // rows)
    n_slots = n_blocks * rows
    counts = cnt[:, 0].astype(jnp.int32)
    pcounts = (counts + rows - 1) // rows * rows
    pend = jnp.cumsum(pcounts)
    pstart = pend - pcounts
    e = rt[0:2].astype(jnp.int32).T
    rank = rk[0:2].astype(jnp.int32).T
    real = e < N_EXPERTS
    slot = pstart[jnp.minimum(e, N_EXPERTS - 1)] + rank
    tok = jnp.arange(m, dtype=jnp.int32)
    pos = tok % lp
    unused_idx = (tok // lp) * n_unused + jnp.where(pos < ROW_PAD, pos, pos - l_end + ROW_PAD)
    spare = n_slots + 2 * unused_idx[:, None] + jnp.arange(2, dtype=jnp.int32)[None, :]
    dest = jnp.where(real, slot, spare).reshape(-1)
    src = jnp.where(real, slot, 0).reshape(-1)
    block_e = jnp.minimum(jnp.searchsorted(pend, jnp.arange(n_blocks, dtype=jnp.int32) * rows, side="right"),
                          N_EXPERTS - 1).astype(jnp.int32)
    n_used = (pend[-1:] // rows).astype(jnp.int32)

    n_spare = -(-(2 * bsz * n_unused) // rows) * rows
    xs = _dispatch_call(dest, h1, jnp.zeros((n_slots + n_spare, d), F32), tm)
    ys = _ffn_call(block_e, n_used, xs, w_gate, w_up, w_down, n_blocks)
    return _combine_call(src, h1, rc, lg, lb, ys, alpha, _token_tile(m, 384))


def _rope_tables(lp):
    pos = jnp.arange(lp, dtype=F32) - float(ROW_PAD)
    inv = 1.0 / (ROPE_THETA ** (jnp.arange(0, DA_DIM, 2, dtype=F32) / DA_DIM))
    ang = pos[:, None] * inv[None, :]
    reps = LANES // (DA_DIM // 2)
    sign = jnp.tile(jnp.concatenate([-jnp.ones((DA_DIM // 2,), F32), jnp.ones((DA_DIM // 2,), F32)]), LANES // DA_DIM)
    return jnp.tile(jnp.cos(ang), (1, reps)), jnp.tile(jnp.sin(ang), (1, reps)) * sign[None, :]


def _dup_heads(w, n_heads, dim):
    d = w.shape[0]
    return jnp.broadcast_to(w.reshape(d, n_heads, 1, dim), (d, n_heads, 2, dim)).reshape(d, n_heads * 2 * dim)


def kernel(x, meta, ln_in_g, ln_in_b, w_in, conv_w, conv_b, gate_b, lam_q1, lam_k1, lam_q2, lam_k2, diff_g, sink, mlstm_g, w_branch, w_out, ln1_g, ln1_b, ln2_g, ln2_b, w_rg, b_rg, w_re, b_re, w_gate, w_up, w_down):
    bsz, seq, d = x.shape
    depth = w_in.shape[0]
    l_end = seq + BLOCK
    lp = -(-l_end // MXU_TILE) * MXU_TILE
    m = bsz * lp
    alpha = (2.0 * depth) ** 0.25
    tm = _row_tile(lp, 768)
    tk_attn = _row_tile(lp, 768)

    hp = jnp.concatenate([jnp.zeros((bsz, ROW_PAD, d), x.dtype),
                          jnp.broadcast_to(meta.astype(x.dtype)[None], (bsz, N_META_TOK, d)), x,
                          jnp.zeros((bsz, lp - l_end, d), x.dtype)], axis=1)
    h = _ln_call(hp.reshape(m, d), ln_in_g, ln_in_b, tm)
    cos, sin = _rope_tables(lp)
    q_scale = DA_DIM ** -0.5
    rope_scale = jnp.concatenate([jnp.full((A_Q,), q_scale, F32), jnp.ones((A_K,), F32),
                                  jnp.full((B_Q,), q_scale, F32), jnp.ones((2 * B_K,), F32)]).reshape(1, -1)
    conv_scale = jnp.concatenate([jnp.ones((C_Q,), F32), jnp.full((C_K,), MC_QK ** -0.5, F32)]).reshape(1, -1)

    for l in range(depth):
        lam_init = 0.8 - 0.6 * math.exp(-0.3 * l)
        wl = w_in[l]
        col = lambda i: wl[:, OFFS[i]:OFFS[i + 1]]
        w_rope = jnp.concatenate([col(0), col(1), col(3), _dup_heads(col(4), WB_KV, WB_DIM)], axis=1).astype(MXU_DTYPE)
        w_val = jnp.concatenate([col(2), col(8), _dup_heads(col(5), WB_KV, WB_DIM)], axis=1).astype(MXU_DTYPE)
        w_conv = jnp.concatenate([col(6), col(7)], axis=1).astype(MXU_DTYPE)
        w_o = col(9).astype(MXU_DTYPE)
        w_g = jnp.pad(col(10), ((0, 0), (0, LANES - C_G))).astype(MXU_DTYPE)
        b_g = jnp.pad(gate_b[l], (0, LANES - C_G)).reshape(1, LANES)
        w_mg = col(11).astype(MXU_DTYPE)

        rq = _proj_rope_call(h, w_rope, cos, sin, rope_scale, tm, lp).reshape(bsz, lp, -1)
        vv = _proj_call(h, w_val, tm, MXU_DTYPE, None, "proj_val").reshape(bsz, lp, -1)
        zc = _proj_call(h, w_conv, tm, F32, None, "proj_conv").reshape(bsz, lp, -1)
        co = _proj_call(h, w_o, tm, F32, "sigmoid", "proj_ogate")
        gc, gr = _gates_call(h, w_g, b_g, tm, lp, l_end)

        lamv = jnp.stack([lam_q1[l], lam_k1[l], lam_q2[l], lam_k2[l]])
        out_a = _attn_a_call(rq, vv, lamv, diff_g[l].reshape(-1, 1), lam_init, l_end, MXU_TILE, tk_attn)
        out_b = _attn_b_call(rq, vv, sink[l].reshape(1, -1), l_end)
        qk = _conv_call(zc, conv_w[l], conv_b[l].reshape(1, -1), conv_scale, l_end, _row_tile(lp, 768))
        h_f, h_b = _mlstm_call(qk, vv, gc, gr)

        h = _merge_call(h, out_a.reshape(m, -1), out_b.reshape(m, -1), h_f.reshape(m, -1), h_b.reshape(m, -1),
                        co, mlstm_g[l].reshape(1, -1), w_mg, w_branch[l].astype(MXU_DTYPE),
                        w_out[l].astype(MXU_DTYPE), ln1_g[l].reshape(1, -1), ln1_b[l].reshape(1, -1),
                        alpha, _token_tile(m, 384))
        h = _moe(h, w_rg[l], b_rg[l], w_re[l], b_re[l], w_gate[l], w_up[l], w_down[l],
                 ln2_g[l].reshape(1, -1), ln2_b[l].reshape(1, -1), alpha, bsz, lp, l_end, tm)
    return h.reshape(bsz, lp, d)[:, BLOCK:l_end]
```

```python
import functools
import math

import numpy as np
import jax
import jax.numpy as jnp
from jax import lax
from jax.experimental import pallas as pl
from jax.experimental.pallas import tpu as pltpu

D_MODEL = 1024
N_META_TOK = 16
BLOCK = 128
ROW_PAD = BLOCK - N_META_TOK
ROPE_THETA = 10000.0
LN_EPS = 1e-5
NEG = -1e30

DA_HEADS = 4
DA_DIM = 64
WB_HEADS = 8
WB_KV = 2
WB_DIM = 64
WINDOW = 128
MC_HEADS = 4
MC_QK = 128
MC_V = 128
N_BRANCH = 3
BRANCH_W = 512
N_GROUPS = 4
EXP_PER_GROUP = 8
N_EXPERTS = N_GROUPS * EXP_PER_GROUP
D_EXPERT = 512

A_Q = DA_HEADS * 2 * DA_DIM
A_K = A_Q
A_V = A_Q
B_Q = WB_HEADS * WB_DIM
B_K = WB_KV * WB_DIM
B_V = B_K
C_Q = MC_HEADS * MC_QK
C_K = C_Q
C_V = MC_HEADS * MC_V
C_O = C_V
C_G = 4 * MC_HEADS
GATE_W = N_BRANCH * D_MODEL
SPLITS = (A_Q, A_K, A_V, B_Q, B_K, B_V, C_Q, C_K, C_V, C_O, C_G, GATE_W)
OFFS = tuple(int(v) for v in np.cumsum((0,) + SPLITS))

LANES = 128
MXU_TILE = 256
EXPERT_ROWS = 256
MLSTM_EXT = 16
ATTN_EXT = 16
VMEM_LIMIT = 56 * 1024 * 1024

F32 = jnp.float32
MXU_DTYPE = jnp.bfloat16


def _dot(a, b):
    return jnp.dot(a, b, preferred_element_type=F32)


def _dot_nt(a, b):
    return lax.dot_general(a, b, (((1,), (1,)), ((), ())), preferred_element_type=F32)


def _dot_tn(a, b):
    return lax.dot_general(a, b, (((0,), (0,)), ((), ())), preferred_element_type=F32)


def _params(n_axes, flags=None):
    return pltpu.CompilerParams(dimension_semantics=("arbitrary",) * n_axes,
                                vmem_limit_bytes=VMEM_LIMIT, flags=flags)


def _row_tile(n_rows, target):
    best = BLOCK
    for t in range(BLOCK, target + 1, BLOCK):
        if n_rows % t == 0:
            best = t
    return best


def _token_tile(n_rows, target):
    best = 8
    for t in range(8, target + 1, 8):
        if n_rows % t == 0:
            best = t
    return best


def _layer_norm(x, g, b):
    mu = jnp.mean(x, axis=-1, keepdims=True)
    xc = x - mu
    var = jnp.mean(xc * xc, axis=-1, keepdims=True)
    return xc * lax.rsqrt(var + LN_EPS) * g + b


def _ln_kernel(x_ref, g_ref, b_ref, o_ref):
    o_ref[...] = _layer_norm(x_ref[...], g_ref[...], b_ref[...])


def _ln_call(x, g, b, tm):
    m, d = x.shape
    return pl.pallas_call(
        _ln_kernel,
        grid=(m // tm,),
        in_specs=[pl.BlockSpec((tm, d), lambda i: (i, 0)),
                  pl.BlockSpec((1, d), lambda i: (0, 0)),
                  pl.BlockSpec((1, d), lambda i: (0, 0))],
        out_specs=pl.BlockSpec((tm, d), lambda i: (i, 0)),
        out_shape=jax.ShapeDtypeStruct((m, d), F32),
        compiler_params=_params(1),
        name="ln_in",
    )(x, g.reshape(1, d), b.reshape(1, d))


def _proj_rope_kernel(x_ref, w_ref, cos_ref, sin_ref, scale_ref, o_ref):
    z = _dot(x_ref[...].astype(MXU_DTYPE), w_ref[...])
    cos = cos_ref[...]
    sin = sin_ref[...]
    lane = lax.broadcasted_iota(jnp.int32, cos.shape, 1)
    first_half = (lane % DA_DIM) < (DA_DIM // 2)
    for c in range(z.shape[1] // LANES):
        sl = slice(c * LANES, (c + 1) * LANES)
        zc = z[:, sl]
        partner = jnp.where(first_half, pltpu.roll(zc, LANES - DA_DIM // 2, 1),
                            pltpu.roll(zc, DA_DIM // 2, 1))
        o_ref[:, sl] = ((zc * cos + partner * sin) * scale_ref[:, sl]).astype(o_ref.dtype)


def _proj_rope_call(h, w, cos, sin, scale, tm, lp):
    m, d = h.shape
    n = w.shape[1]
    per_batch = lp // tm
    return pl.pallas_call(
        _proj_rope_kernel,
        grid=(m // tm,),
        in_specs=[pl.BlockSpec((tm, d), lambda i: (i, 0)),
                  pl.BlockSpec((d, n), lambda i: (0, 0)),
                  pl.BlockSpec((tm, LANES), lambda i: (i % per_batch, 0)),
                  pl.BlockSpec((tm, LANES), lambda i: (i % per_batch, 0)),
                  pl.BlockSpec((1, n), lambda i: (0, 0))],
        out_specs=pl.BlockSpec((tm, n), lambda i: (i, 0)),
        out_shape=jax.ShapeDtypeStruct((m, n), MXU_DTYPE),
        compiler_params=_params(1),
        name="proj_rope",
    )(h, w, cos, sin, scale)


def _proj_kernel(x_ref, w_ref, o_ref, *, act):
    z = _dot(x_ref[...].astype(MXU_DTYPE), w_ref[...])
    if act == "sigmoid":
        z = jax.nn.sigmoid(z)
    o_ref[...] = z.astype(o_ref.dtype)


def _proj_call(h, w, tm, out_dtype, act, name):
    m, d = h.shape
    n = w.shape[1]
    return pl.pallas_call(
        functools.partial(_proj_kernel, act=act),
        grid=(m // tm,),
        in_specs=[pl.BlockSpec((tm, d), lambda i: (i, 0)),
                  pl.BlockSpec((d, n), lambda i: (0, 0))],
        out_specs=pl.BlockSpec((tm, n), lambda i: (i, 0)),
        out_shape=jax.ShapeDtypeStruct((m, n), out_dtype),
        compiler_params=_params(1),
        name=name,
    )(h, w)


def _proj_t_kernel(x_ref, wt_ref, o_ref):
    o_ref[...] = _dot_nt(wt_ref[...], x_ref[...].astype(MXU_DTYPE)).astype(o_ref.dtype)


def _proj_t_call(h, wt, tm, name):
    m, d = h.shape
    n = wt.shape[0]
    return pl.pallas_call(
        _proj_t_kernel,
        grid=(m // tm,),
        in_specs=[pl.BlockSpec((tm, d), lambda i: (i, 0)),
                  pl.BlockSpec((n, d), lambda i: (0, 0))],
        out_specs=pl.BlockSpec((n, tm), lambda i: (0, i)),
        out_shape=jax.ShapeDtypeStruct((n, m), MXU_DTYPE),
        compiler_params=_params(1),
        name=name,
    )(h, wt)


def _gates_kernel(x_ref, w_ref, b_ref, gc_ref, gr_ref, *, tm, lp, l_end):
    z = _dot(x_ref[...].astype(MXU_DTYPE), w_ref[...]) + b_ref[...]
    lane = lax.broadcasted_iota(jnp.int32, (tm, LANES), 1)
    kind = lane // MC_HEADS
    row = lax.broadcasted_iota(jnp.int32, (tm, LANES), 0) + pl.program_id(0) * tm
    pos = row % lp
    unused = (pos < ROW_PAD) | (pos >= l_end)
    log_f = jnp.minimum(z, 0.0) - jnp.log1p(jnp.exp(-jnp.abs(z)))
    is_forget = (kind % 2) == 1
    base = jnp.where(is_forget, jnp.where(unused, 0.0, log_f), jnp.where(unused, NEG, z))
    r128 = lax.broadcasted_iota(jnp.int32, (BLOCK, LANES), 0)
    fwd_lane = lax.broadcasted_iota(jnp.int32, (BLOCK, LANES), 1) // MC_HEADS == 1
    forget128 = (lax.broadcasted_iota(jnp.int32, (BLOCK, LANES), 1) // MC_HEADS) % 2 == 1
    for c in range(tm // BLOCK):
        x = base[c * BLOCK:(c + 1) * BLOCK]
        pre = x
        suf = x
        s = 1
        while s < BLOCK:
            pre = pre + jnp.where(r128 >= s, pltpu.roll(pre, s, 0), 0.0)
            suf = suf + jnp.where(r128 < BLOCK - s, pltpu.roll(suf, BLOCK - s, 0), 0.0)
            s *= 2
        out = jnp.where(forget128, jnp.where(fwd_lane, pre, suf), x)
        gc_ref[c * BLOCK:(c + 1) * BLOCK, :] = out
        gr_ref[:, c * BLOCK:(c + 1) * BLOCK] = out.T[0:C_G, :]


def _gates_call(h, w, b, tm, lp, l_end):
    m, d = h.shape
    return pl.pallas_call(
        functools.partial(_gates_kernel, tm=tm, lp=lp, l_end=l_end),
        grid=(m // tm,),
        in_specs=[pl.BlockSpec((tm, d), lambda i: (i, 0)),
                  pl.BlockSpec((d, LANES), lambda i: (0, 0)),
                  pl.BlockSpec((1, LANES), lambda i: (0, 0))],
        out_specs=[pl.BlockSpec((tm, LANES), lambda i: (i, 0)),
                   pl.BlockSpec((C_G, tm), lambda i: (0, i))],
        out_shape=[jax.ShapeDtypeStruct((m, LANES), F32),
                   jax.ShapeDtypeStruct((C_G, m), F32)],
        compiler_params=_params(1),
        name="mlstm_gates",
    )(h, w, b)


def _conv_kernel(z_ref, w_ref, b_ref, scale_ref, o_ref, *, lp, l_end, tr):
    w0 = w_ref[0:1, :]
    w1 = w_ref[1:2, :]
    w2 = w_ref[2:3, :]
    row = lax.broadcasted_iota(jnp.int32, (tr, LANES), 0)
    for c in range(lp // tr):
        r0 = c * tr
        zc = z_ref[0, r0:r0 + tr, :]
        before = jnp.zeros((1, LANES), F32) if r0 == 0 else z_ref[0, r0 - 1:r0, :]
        after = jnp.zeros((1, LANES), F32) if r0 + tr == lp else z_ref[0, r0 + tr:r0 + tr + 1, :]
        prev = jnp.where(row == 0, before, pltpu.roll(zc, 1, 0))
        nxt = jnp.where(row == tr - 1, after, pltpu.roll(zc, tr - 1, 0))
        if r0 <= ROW_PAD < r0 + tr:
            prev = jnp.where(row == ROW_PAD - r0, 0.0, prev)
        if r0 <= l_end - 1 < r0 + tr:
            nxt = jnp.where(row == l_end - 1 - r0, 0.0, nxt)
        y = prev * w0 + zc * w1 + nxt * w2 + b_ref[...]
        o_ref[0, r0:r0 + tr, :] = (jax.nn.silu(y) * scale_ref[...]).astype(o_ref.dtype)


def _conv_call(z, w, b, scale, l_end, tr):
    bsz, lp, n = z.shape
    return pl.pallas_call(
        functools.partial(_conv_kernel, lp=lp, l_end=l_end, tr=tr),
        grid=(bsz, n // LANES),
        in_specs=[pl.BlockSpec((1, lp, LANES), lambda b_, j: (b_, 0, j)),
                  pl.BlockSpec((3, LANES), lambda b_, j: (0, j)),
                  pl.BlockSpec((1, LANES), lambda b_, j: (0, j)),
                  pl.BlockSpec((1, LANES), lambda b_, j: (0, j))],
        out_specs=pl.BlockSpec((1, lp, LANES), lambda b_, j: (b_, 0, j)),
        out_shape=jax.ShapeDtypeStruct((bsz, lp, n), MXU_DTYPE),
        compiler_params=_params(2),
        name="mlstm_conv",
    )(z, w, b, scale)


def _attn_a_kernel(lamv_ref, g_ref, q_ref, k_ref, vt_ref, o_ref, s0_scr, s1_scr, acc_scr, *,
                   tk, n_chunks, l_end, lam_init):
    q = q_ref[0]
    tq = q.shape[0]
    qt = q.astype(F32).T.astype(MXU_DTYPE)
    feat = lax.broadcasted_iota(jnp.int32, qt.shape, 0)
    key_row = lax.broadcasted_iota(jnp.int32, (tk, tq), 0)
    zero = jnp.zeros_like(qt)
    qz = (jnp.where(feat < DA_DIM, qt, zero), jnp.where(feat >= DA_DIM, qt, zero))
    s_bufs = (s0_scr, s1_scr)

    def scores(j, slot):
        start = pl.multiple_of(j * tk, tk)
        kj = k_ref[0, pl.ds(start, tk), :]
        for c in range(2):
            s_bufs[slot][c] = _dot(kj, qz[c])

    ones_rows = jnp.ones((ATTN_EXT, tk), MXU_DTYPE)

    def softmax_values(j, slot, stats, lo, hi):
        start = pl.multiple_of(j * tk, tk)
        vt = jnp.concatenate([vt_ref[:, pl.ds(start, tk)], ones_rows], axis=0)
        new_stats = []
        for c in range(2):
            m = stats[c]
            s = s_bufs[slot][c]
            if lo is not None:
                s = jnp.where(key_row >= lo, s, NEG)
            if hi is not None:
                s = jnp.where(key_row < hi, s, NEG)
            m_new = jnp.maximum(m, jnp.max(s, axis=0, keepdims=True))
            alpha = jnp.exp2(m - m_new)
            p = jnp.exp2((s - m_new).astype(MXU_DTYPE))
            acc_scr[c] = alpha * acc_scr[c] + _dot(vt, p)
            new_stats.append(m_new)
        return tuple(new_stats)

    one = jnp.full((1, tq), NEG, F32)
    acc_scr[...] = jnp.zeros_like(acc_scr)
    last = n_chunks - 1
    last_hi = l_end - last * tk
    last_hi = None if last_hi == tk else last_hi
    scores(0, 0)
    if n_chunks == 1:
        softmax_values(0, 0, (one, one), ROW_PAD, last_hi)
    else:
        def step(j, parity, stats, lo):
            scores(j + 1, 1 - parity)
            return softmax_values(j, parity, stats, lo, None)

        def pair(i, st):
            j = 1 + 2 * i
            return step(j + 1, 0, step(j, 1, st, None), None)

        stats = step(0, 0, (one, one), ROW_PAD)
        stats = lax.fori_loop(0, (last - 1) // 2, pair, stats)
        if (last - 1) % 2:
            stats = step(last - 1, (last - 1) % 2, stats, None)
        softmax_values(last, last % 2, stats, None, last_hi)
    dv = 2 * DA_DIM
    o0 = acc_scr[0, 0:dv, :] / acc_scr[0, dv:dv + 1, :]
    o1 = acc_scr[1, 0:dv, :] / acc_scr[1, dv:dv + 1, :]
    lv = lamv_ref[...]
    lam = (jnp.exp(jnp.sum(lv[0:1] * lv[1:2], axis=-1, keepdims=True))
           - jnp.exp(jnp.sum(lv[2:3] * lv[3:4], axis=-1, keepdims=True)) + lam_init)
    o = o0 - lam * o1
    ms = jnp.mean(o * o, axis=0, keepdims=True)
    o = o * lax.rsqrt(ms + LN_EPS) * g_ref[...] * (1.0 - lam_init)
    o_ref[0] = o.T.astype(o_ref.dtype)


def _attn_a_call(rq, vt, lamv, g_col, lam_init, l_end, tq, tk):
    bsz, lp, _ = rq.shape
    k_blk = A_Q // LANES
    vt_blk = C_V // LANES
    return pl.pallas_call(
        functools.partial(_attn_a_kernel, tk=tk, n_chunks=lp // tk, l_end=l_end, lam_init=lam_init),
        grid=(bsz, DA_HEADS, lp // tq),
        in_specs=[pl.BlockSpec((4, DA_DIM), lambda b, h, i: (0, 0)),
                  pl.BlockSpec((2 * DA_DIM, 1), lambda b, h, i: (0, 0)),
                  pl.BlockSpec((1, tq, LANES), lambda b, h, i: (b, i, h)),
                  pl.BlockSpec((1, lp, LANES), lambda b, h, i: (b, 0, k_blk + h)),
                  pl.BlockSpec((2 * DA_DIM, lp), lambda b, h, i: (vt_blk + h, b))],
        out_specs=pl.BlockSpec((1, tq, LANES), lambda b, h, i: (b, i, h)),
        out_shape=jax.ShapeDtypeStruct((bsz, lp, A_V), MXU_DTYPE),
        scratch_shapes=[pltpu.VMEM((2, tk, tq), F32),
                        pltpu.VMEM((2, tk, tq), F32),
                        pltpu.VMEM((2, 2 * DA_DIM + ATTN_EXT, tq), F32)],
        compiler_params=_params(3),
        name="diff_attn",
    )(lamv, g_col, rq, rq, vt)


def _attn_b_kernel(sink_ref, q_ref, k0_ref, k1_ref, v0_ref, v1_ref, o_ref, *, lp, l_end):
    n = pl.program_id(1)
    nb = lp // BLOCK
    grp = WB_HEADS // WB_KV

    def blocks(ref):
        parts = [ref[0, 0:BLOCK, :]]
        for d in (-1, 0, 1):
            idx = jnp.clip(n + d, 0, nb - 1)
            parts.append(ref[0, pl.ds(pl.multiple_of(idx * BLOCK, BLOCK), BLOCK), :])
        return jnp.concatenate(parts, axis=0)

    keys = (blocks(k0_ref), blocks(k1_ref))
    vals = (blocks(v0_ref), blocks(v1_ref))
    shape = (4 * BLOCK, grp * BLOCK)
    krow = lax.broadcasted_iota(jnp.int32, shape, 0)
    qcol = lax.broadcasted_iota(jnp.int32, shape, 1)
    qpos = qcol % BLOCK + n * BLOCK
    kpos = (n - 2) * BLOCK + krow
    band_ok = (kpos >= BLOCK) & (kpos < l_end) & (jnp.abs(qpos - kpos) <= WINDOW)
    valid = ((krow >= ROW_PAD) & (krow < BLOCK)) | ((krow >= BLOCK) & band_ok)
    head_of_col = lax.broadcasted_iota(jnp.int32, (1, grp * BLOCK), 1) // BLOCK
    lane = lax.broadcasted_iota(jnp.int32, (BLOCK, LANES), 1)
    for g in range(WB_KV):
        qs = []
        sink = jnp.zeros((1, grp * BLOCK), F32)
        for j in range(grp):
            h = g * grp + j
            qt = q_ref[0, :, (h // 2) * LANES:(h // 2 + 1) * LANES]
            keep = (lane >= WB_DIM) if h % 2 else (lane < WB_DIM)
            qs.append(jnp.where(keep, qt, jnp.zeros_like(qt)))
            sink = jnp.where(head_of_col == j, sink_ref[:, h:h + 1], sink)
        s = jnp.where(valid, _dot_nt(keys[g], jnp.concatenate(qs, axis=0)), NEG)
        m = jnp.maximum(jnp.max(s, axis=0, keepdims=True), sink)
        p = jnp.exp(s - m)
        den = jnp.sum(p, axis=0, keepdims=True) + jnp.exp(sink - m)
        o = (_dot_tn(vals[g], p.astype(MXU_DTYPE)) / den).T
        for jj in range(grp // 2):
            lo = o[(2 * jj) * BLOCK:(2 * jj + 1) * BLOCK]
            hi = o[(2 * jj + 1) * BLOCK:(2 * jj + 2) * BLOCK]
            t = (g * grp) // 2 + jj
            o_ref[0, :, t * LANES:(t + 1) * LANES] = jnp.where(lane < WB_DIM, lo, hi).astype(o_ref.dtype)


def _attn_b_call(rq, vv, sink, l_end):
    bsz, lp, _ = rq.shape
    q_blk = (A_Q + A_K) // B_Q
    k_blk = (A_Q + A_K + B_Q) // LANES
    v_blk = 0
    seq = lambda c: pl.BlockSpec((1, lp, LANES), lambda b, n: (b, 0, c))
    return pl.pallas_call(
        functools.partial(_attn_b_kernel, lp=lp, l_end=l_end),
        grid=(bsz, lp // BLOCK),
        in_specs=[pl.BlockSpec((1, WB_HEADS), lambda b, n: (0, 0)),
                  pl.BlockSpec((1, BLOCK, B_Q), lambda b, n: (b, n, q_blk)),
                  seq(k_blk), seq(k_blk + 1), seq(v_blk), seq(v_blk + 1)],
        out_specs=pl.BlockSpec((1, BLOCK, B_Q), lambda b, n: (b, n, 0)),
        out_shape=jax.ShapeDtypeStruct((bsz, lp, B_Q), MXU_DTYPE),
        compiler_params=_params(2),
        name="window_attn",
    )(sink, rq, rq, rq, vv, vv)


def _mlstm_kernel(qkf_ref, vf_ref, gcf_ref, grf_ref, qkb_ref, vb_ref, gcb_ref, grb_ref,
                  hf_ref, hb_ref, c_scr, m_scr):
    t = pl.program_id(1)

    @pl.when(t == 0)
    def _():
        c_scr[...] = jnp.zeros_like(c_scr)
        m_scr[...] = jnp.zeros_like(m_scr)

    srow = lax.broadcasted_iota(jnp.int32, (BLOCK, BLOCK), 0)
    ccol = lax.broadcasted_iota(jnp.int32, (BLOCK, BLOCK), 1)
    ext_row = lax.broadcasted_iota(jnp.int32, (MLSTM_EXT, BLOCK), 0)
    ones_rows = jnp.where(ext_row == 0, 1.0, 0.0).astype(MXU_DTYPE)
    dirs = ((qkf_ref, vf_ref, gcf_ref, grf_ref, hf_ref, srow <= ccol, BLOCK - 1),
            (qkb_ref, vb_ref, gcb_ref, grb_ref, hb_ref, srow >= ccol, 0))
    for d, (qk_ref, vt_ref, gc_ref, gr_ref, h_ref, tri, last) in enumerate(dirs):
        for hd in range(MC_HEADS):
            ci = d * MC_HEADS + hd
            j_li = (2 * d) * MC_HEADS + hd
            j_b = (2 * d + 1) * MC_HEADS + hd
            q = qk_ref[0, :, hd * MC_QK:(hd + 1) * MC_QK]
            k = qk_ref[0, :, C_Q + hd * MC_QK:C_Q + (hd + 1) * MC_QK]
            vt = vt_ref[hd * MC_V:(hd + 1) * MC_V, :]
            vext = jnp.concatenate([vt, ones_rows], axis=0)
            key_col = gc_ref[:, j_li:j_li + 1] - gc_ref[:, j_b:j_b + 1]
            li_row = gr_ref[j_li:j_li + 1, :]
            b_row = gr_ref[j_b:j_b + 1, :]
            g = b_row[:, last:last + 1]
            m_prev = m_scr[ci, 0:1, 0:1]
            c_prev = c_scr[ci]

            dmat = jnp.where(tri, b_row + key_col, NEG)
            m_t = jnp.maximum(b_row + m_prev, jnp.max(dmat, axis=0, keepdims=True))
            inter = jnp.exp(b_row + m_prev - m_t)
            s = (_dot_nt(k, q) * jnp.exp(dmat - m_t)).astype(MXU_DTYPE)
            nd = inter * _dot_nt(c_prev.astype(MXU_DTYPE), q) + _dot(vext, s)
            den = nd[MC_V:MC_V + 1, :]
            h_t = nd[0:MC_V, :] / jnp.maximum(jnp.abs(den), jnp.exp(-m_t))
            h_ref[0, :, hd * MC_V:(hd + 1) * MC_V] = h_t.T

            a_row = g - b_row + li_row
            m_new = jnp.maximum(g + m_prev, jnp.max(a_row, axis=1, keepdims=True))
            decay = jnp.exp(g + m_prev - m_new)
            vw = (vext.astype(F32) * jnp.exp(a_row - m_new)).astype(MXU_DTYPE)
            c_scr[ci] = decay * c_prev + _dot(vw, k)
            m_scr[ci] = jnp.broadcast_to(m_new, m_scr.shape[1:])


def _mlstm_call(qk, vt, gc, gr):
    bsz, lp, _ = qk.shape
    nch = lp // BLOCK
    fwd = lambda b, t: (b, t, 0)
    bwd = lambda b, t: (b, nch - 1 - t, 0)
    return pl.pallas_call(
        _mlstm_kernel,
        grid=(bsz, nch),
        in_specs=[pl.BlockSpec((1, BLOCK, C_Q + C_K), fwd),
                  pl.BlockSpec((C_V, BLOCK), lambda b, t: (0, b * nch + t)),
                  pl.BlockSpec((BLOCK, LANES), lambda b, t: (b * nch + t, 0)),
                  pl.BlockSpec((C_G, BLOCK), lambda b, t: (0, b * nch + t)),
                  pl.BlockSpec((1, BLOCK, C_Q + C_K), bwd),
                  pl.BlockSpec((C_V, BLOCK), lambda b, t: (0, b * nch + nch - 1 - t)),
                  pl.BlockSpec((BLOCK, LANES), lambda b, t: (b * nch + nch - 1 - t, 0)),
                  pl.BlockSpec((C_G, BLOCK), lambda b, t: (0, b * nch + nch - 1 - t))],
        out_specs=[pl.BlockSpec((1, BLOCK, C_V), fwd),
                   pl.BlockSpec((1, BLOCK, C_V), bwd)],
        out_shape=[jax.ShapeDtypeStruct((bsz, lp, C_V), F32),
                   jax.ShapeDtypeStruct((bsz, lp, C_V), F32)],
        scratch_shapes=[pltpu.VMEM((2 * MC_HEADS, MC_V + MLSTM_EXT, MC_QK), F32),
                        pltpu.VMEM((2 * MC_HEADS, 8, LANES), F32)],
        compiler_params=_params(2),
        name="mlstm_scan",
    )(qk, vt, gc, gr, qk, vt, gc, gr)


def _merge_kernel(h_ref, oa_ref, ob_ref, hf_ref, hb_ref, co_ref, mg_ref, wg_ref, wb_ref, wo_ref,
                  lg_ref, lb_ref, o_ref, *, alpha):
    h = h_ref[...]
    hx = h.astype(MXU_DTYPE)
    hc = hf_ref[...] + hb_ref[...]
    parts = []
    for hd in range(MC_HEADS):
        sl = slice(hd * MC_V, (hd + 1) * MC_V)
        x = hc[:, sl]
        mu = jnp.mean(x, axis=-1, keepdims=True)
        xc = x - mu
        var = jnp.mean(xc * xc, axis=-1, keepdims=True)
        parts.append(xc * lax.rsqrt(var + LN_EPS) * mg_ref[:, sl] * co_ref[:, sl])
    oc = jnp.concatenate(parts, axis=1).astype(MXU_DTYPE)
    branches = (oa_ref[...], ob_ref[...], oc)
    merged = None
    for br in range(N_BRANCH):
        gate = jax.nn.sigmoid(_dot(hx, wg_ref[:, br * D_MODEL:(br + 1) * D_MODEL]))
        term = gate * _dot(branches[br], wb_ref[br])
        merged = term if merged is None else merged + term
    y = _dot(merged.astype(MXU_DTYPE), wo_ref[...])
    o_ref[...] = _layer_norm(alpha * h + y, lg_ref[...], lb_ref[...])


def _merge_call(h, oa, ob, hf, hb, co, mg, wg, wb, wo, lg, lb, alpha, tm):
    m, d = h.shape
    rows = lambda n: pl.BlockSpec((tm, n), lambda i: (i, 0))
    full2 = lambda a: pl.BlockSpec(a.shape, lambda i: (0, 0))
    return pl.pallas_call(
        functools.partial(_merge_kernel, alpha=alpha),
        grid=(m // tm,),
        in_specs=[rows(d), rows(A_V), rows(B_Q), rows(C_V), rows(C_V), rows(C_O),
                  full2(mg), full2(wg), pl.BlockSpec(wb.shape, lambda i: (0, 0, 0)), full2(wo),
                  full2(lg), full2(lb)],
        out_specs=rows(d),
        out_shape=jax.ShapeDtypeStruct((m, d), F32),
        compiler_params=_params(1),
        name="merge_ln1",
    )(h, oa, ob, hf, hb, co, mg, wg, wb, wo, lg, lb)


def _split3(x):
    hi = x.astype(MXU_DTYPE)
    lo = (x - hi.astype(F32)).astype(MXU_DTYPE)
    return hi, lo


def _router_kernel(h_ref, w_ref, b_ref, rt_ref, rc_ref, *, tm, lp, l_end):
    x_hi, x_lo = _split3(h_ref[...])
    w_hi, w_lo = _split3(w_ref[...])
    logits = (_dot_nt(w_hi, x_hi) + _dot_nt(w_hi, x_lo) + _dot_nt(w_lo, x_hi)) + b_ref[...]
    none = float(N_EXPERTS)
    gl = logits[N_EXPERTS:N_EXPERTS + 8]
    grow = lax.broadcasted_iota(jnp.int32, gl.shape, 0).astype(F32)
    gmax = jnp.max(gl, axis=0, keepdims=True)
    g_sel = jnp.min(jnp.where(gl == gmax, grow, none), axis=0, keepdims=True)
    p_grp = 1.0 / jnp.sum(jnp.exp(gl - gmax), axis=0, keepdims=True)
    el = logits[0:N_EXPERTS]
    erow_i = lax.broadcasted_iota(jnp.int32, el.shape, 0)
    erow = erow_i.astype(F32)
    cand = jnp.where((erow_i // EXP_PER_GROUP).astype(F32) == g_sel, el, -jnp.inf)
    top1 = jnp.max(cand, axis=0, keepdims=True)
    i1 = jnp.min(jnp.where(cand == top1, erow, none), axis=0, keepdims=True)
    cand2 = jnp.where(erow == i1, -jnp.inf, cand)
    top2 = jnp.max(cand2, axis=0, keepdims=True)
    i2 = jnp.min(jnp.where(cand2 == top2, erow, none), axis=0, keepdims=True)
    e = jnp.exp(top2 - top1)
    w1 = (1.0 / (1.0 + e)) * p_grp
    w2 = (e / (1.0 + e)) * p_grp
    pos = lax.broadcasted_iota(jnp.int32, (1, tm), 1) + pl.program_id(0) * tm
    real = ((pos % lp) >= ROW_PAD) & ((pos % lp) < l_end)
    e1 = jnp.where(real, i1, none)
    e2 = jnp.where(real, i2, none)
    r = lax.broadcasted_iota(jnp.int32, (LANES, tm), 0)
    table = jnp.where(r == 0, e1, jnp.where(r == 1, e2, jnp.where(r == 2, w1, jnp.where(r == 3, w2, 0.0))))
    rt_ref[...] = table[0:8]
    rc_ref[...] = table.T


def _router_call(h, w, b, tm, lp, l_end):
    m, d = h.shape
    return pl.pallas_call(
        functools.partial(_router_kernel, tm=tm, lp=lp, l_end=l_end),
        grid=(m // tm,),
        in_specs=[pl.BlockSpec((tm, d), lambda i: (i, 0)),
                  pl.BlockSpec((LANES, d), lambda i: (0, 0)),
                  pl.BlockSpec((LANES, 1), lambda i: (0, 0))],
        out_specs=[pl.BlockSpec((8, tm), lambda i: (0, i)),
                   pl.BlockSpec((tm, LANES), lambda i: (i, 0))],
        out_shape=[jax.ShapeDtypeStruct((8, m), F32),
                   jax.ShapeDtypeStruct((m, LANES), F32)],
        compiler_params=_params(1),
        name="moe_router",
    )(h, w, b)


def _rank_kernel(rt_ref, rk_ref, cnt_ref, carry, *, tm):
    @pl.when(pl.program_id(0) == 0)
    def _():
        carry[...] = jnp.zeros_like(carry)

    erow = lax.broadcasted_iota(jnp.int32, (N_EXPERTS, tm), 0).astype(F32)
    oh1 = jnp.where(erow == rt_ref[0:1, :], 1.0, 0.0)
    oh2 = jnp.where(erow == rt_ref[1:2, :], 1.0, 0.0)
    oh = oh1 + oh2
    earlier = (lax.broadcasted_iota(jnp.int32, (tm, tm), 0)
               < lax.broadcasted_iota(jnp.int32, (tm, tm), 1))
    before = _dot(oh.astype(MXU_DTYPE), jnp.where(earlier, 1.0, 0.0).astype(MXU_DTYPE)) + carry[:, 0:1]
    r1 = jnp.sum(oh1 * before, axis=0, keepdims=True)
    r2 = jnp.sum(oh2 * before, axis=0, keepdims=True)
    r = lax.broadcasted_iota(jnp.int32, (8, tm), 0)
    rk_ref[...] = jnp.where(r == 0, r1, jnp.where(r == 1, r2, 0.0))
    total = carry[...] + jnp.sum(oh, axis=1, keepdims=True)
    carry[...] = total
    cnt_ref[...] = total


def _rank_call(rt, tm):
    m = rt.shape[1]
    return pl.pallas_call(
        functools.partial(_rank_kernel, tm=tm),
        grid=(m // tm,),
        in_specs=[pl.BlockSpec((8, tm), lambda i: (0, i))],
        out_specs=[pl.BlockSpec((8, tm), lambda i: (0, i)),
                   pl.BlockSpec((N_EXPERTS, LANES), lambda i: (0, 0))],
        out_shape=[jax.ShapeDtypeStruct((8, m), F32),
                   jax.ShapeDtypeStruct((N_EXPERTS, LANES), F32)],
        scratch_shapes=[pltpu.VMEM((N_EXPERTS, LANES), F32)],
        compiler_params=_params(1),
        name="moe_rank",
    )(rt)


def _dispatch_kernel(dest_ref, h_ref, xs_in_ref, xs_ref, sem, *, tm, m):
    del xs_in_ref
    base = pl.program_id(0) * tm

    def body(r, c):
        for k in range(2):
            d = dest_ref[k * m + base + r]
            pltpu.make_async_copy(h_ref.at[pl.ds(r, 1)], xs_ref.at[pl.ds(d, 1)], sem).start()
        return c

    lax.fori_loop(0, tm, body, 0)
    for _ in range(2):
        pltpu.make_async_copy(h_ref, xs_ref.at[pl.ds(0, tm)], sem).wait()


def _dispatch_call(dest, h, xs0, tm):
    m, d = h.shape
    return pl.pallas_call(
        functools.partial(_dispatch_kernel, tm=tm, m=m),
        grid_spec=pltpu.PrefetchScalarGridSpec(
            num_scalar_prefetch=1,
            grid=(m // tm,),
            in_specs=[pl.BlockSpec((tm, d), lambda i, dest_: (i, 0)),
                      pl.BlockSpec(memory_space=pl.ANY)],
            out_specs=pl.BlockSpec(memory_space=pl.ANY),
            scratch_shapes=[pltpu.SemaphoreType.DMA(())]),
        out_shape=jax.ShapeDtypeStruct(xs0.shape, xs0.dtype),
        input_output_aliases={2: 0},
        compiler_params=_params(1),
        name="moe_dispatch",
    )(dest, h, xs0)


def _ffn_kernel(be_ref, nu_ref, xs_ref, wg_ref, wu_ref, wd_ref, ys_ref, wg_s, wu_s, wd_s):
    i = pl.program_id(0)
    new_expert = jnp.logical_or(i == 0, be_ref[i] != be_ref[jnp.maximum(i - 1, 0)])

    @pl.when(new_expert)
    def _():
        wg_s[...] = wg_ref[0, 0].astype(MXU_DTYPE)
        wu_s[...] = wu_ref[0, 0].astype(MXU_DTYPE)
        wd_s[...] = wd_ref[0, 0].astype(MXU_DTYPE)

    @pl.when(i < nu_ref[0])
    def _():
        xb = xs_ref[...].astype(MXU_DTYPE)
        act = jax.nn.silu(_dot(xb, wg_s[...])) * _dot(xb, wu_s[...])
        ys_ref[...] = _dot(act.astype(MXU_DTYPE), wd_s[...])

    @pl.when(i >= nu_ref[0])
    def _():
        ys_ref[...] = jnp.zeros_like(ys_ref)


def _ffn_call(block_e, n_used, xs, wg, wu, wd, layer, n_blocks):
    d = xs.shape[1]
    br = EXPERT_ROWS
    return pl.pallas_call(
        _ffn_kernel,
        grid_spec=pltpu.PrefetchScalarGridSpec(
            num_scalar_prefetch=2,
            grid=(n_blocks,),
            in_specs=[pl.BlockSpec((br, d), lambda i, be, nu: (i, 0)),
                      pl.BlockSpec((1, 1, d, D_EXPERT), lambda i, be, nu: (layer, be[i], 0, 0)),
                      pl.BlockSpec((1, 1, d, D_EXPERT), lambda i, be, nu: (layer, be[i], 0, 0)),
                      pl.BlockSpec((1, 1, D_EXPERT, d), lambda i, be, nu: (layer, be[i], 0, 0))],
            out_specs=pl.BlockSpec((br, d), lambda i, be, nu: (i, 0)),
            scratch_shapes=[pltpu.VMEM((d, D_EXPERT), MXU_DTYPE),
                            pltpu.VMEM((d, D_EXPERT), MXU_DTYPE),
                            pltpu.VMEM((D_EXPERT, d), MXU_DTYPE)]),
        out_shape=jax.ShapeDtypeStruct((n_blocks * br, d), F32),
        compiler_params=_params(1),
        name="moe_experts",
    )(block_e, n_used, xs, wg, wu, wd)


def _combine_kernel(src_ref, h_ref, rc_ref, lg_ref, lb_ref, ys_ref, o_ref, buf, sem, *, tm, m, alpha):
    i = pl.program_id(0)

    def issue(tile, slot):
        base = tile * tm

        def body(r, c):
            for k in range(2):
                s = src_ref[k * m + base + r]
                pltpu.make_async_copy(ys_ref.at[pl.ds(s, 1)], buf.at[slot, k, pl.ds(r, 1)], sem.at[slot]).start()
            return c

        lax.fori_loop(0, tm, body, 0)

    @pl.when(i == 0)
    def _():
        issue(0, 0)

    @pl.when(i + 1 < m // tm)
    def _():
        issue(i + 1, (i + 1) % 2)

    slot = i % 2
    for k in range(2):
        pltpu.make_async_copy(ys_ref.at[pl.ds(0, tm)], buf.at[slot, k], sem.at[slot]).wait()
    rc = rc_ref[...]
    real = rc[:, 0:1] < float(N_EXPERTS)
    y = jnp.where(real, rc[:, 2:3] * buf[slot, 0] + rc[:, 3:4] * buf[slot, 1], 0.0)
    o_ref[...] = _layer_norm(alpha * h_ref[...] + y, lg_ref[...], lb_ref[...])


def _combine_call(src, h, rc, lg, lb, ys, alpha, tm):
    m, d = h.shape
    n_tiles = m // tm
    return pl.pallas_call(
        functools.partial(_combine_kernel, tm=tm, m=m, alpha=alpha),
        grid_spec=pltpu.PrefetchScalarGridSpec(
            num_scalar_prefetch=1,
            grid=(n_tiles,),
            in_specs=[pl.BlockSpec((tm, d), lambda i, s: (i, 0)),
                      pl.BlockSpec((tm, LANES), lambda i, s: (i, 0)),
                      pl.BlockSpec((1, d), lambda i, s: (0, 0)),
                      pl.BlockSpec((1, d), lambda i, s: (0, 0)),
                      pl.BlockSpec(memory_space=pl.ANY)],
            out_specs=pl.BlockSpec((tm, d), lambda i, s: (i, 0)),
            scratch_shapes=[pltpu.VMEM((2, 2, tm, d), F32),
                            pltpu.SemaphoreType.DMA((2,))]),
        out_shape=jax.ShapeDtypeStruct((m, d), F32),
        compiler_params=_params(1),
        name="moe_combine_ln2",
    )(src, h, rc, lg, lb, ys)


def _moe(h1, w_rg, b_rg, w_re, b_re, w_gate, w_up, w_down, layer, lg, lb, alpha, bsz, lp, l_end, tm):
    m, d = h1.shape
    wr = jnp.zeros((LANES, d), F32).at[0:N_EXPERTS].set(w_re.T).at[N_EXPERTS:N_EXPERTS + N_GROUPS].set(w_rg.T)
    br_ = jnp.zeros((LANES,), F32).at[0:N_EXPERTS].set(b_re).at[N_EXPERTS:N_EXPERTS + N_GROUPS].set(b_rg)
    br_ = br_.at[N_EXPERTS + N_GROUPS:N_EXPERTS + 8].set(NEG).reshape(LANES, 1)
    rt, rc = _router_call(h1, wr, br_, tm, lp, l_end)
    rk, cnt = _rank_call(rt, tm)

    rows = EXPERT_ROWS
    n_assign = 2 * bsz * (l_end - ROW_PAD)
    n_unused = lp - (l_end - ROW_PAD)
    n_blocks = -(-(n_assign + N_EXPERTS * (rows - 1)) // rows)
    n_slots = n_blocks * rows
    counts = cnt[:, 0].astype(jnp.int32)
    pcounts = (counts + rows - 1) // rows * rows
    pend = jnp.cumsum(pcounts)
    pstart = pend - pcounts
    e = rt[0:2].astype(jnp.int32)
    rank = rk[0:2].astype(jnp.int32)
    real = e < N_EXPERTS
    expert_ids = jnp.arange(N_EXPERTS, dtype=jnp.int32)[:, None, None]
    slot = jnp.sum(jnp.where(e[None] == expert_ids, pstart[:, None, None], 0), axis=0) + rank
    tok = jnp.arange(m, dtype=jnp.int32)
    pos = tok % lp
    unused_idx = (tok // lp) * n_unused + jnp.where(pos < ROW_PAD, pos, pos - l_end + ROW_PAD)
    spare = n_slots + 2 * unused_idx[None, :] + jnp.arange(2, dtype=jnp.int32)[:, None]
    dest = jnp.where(real, slot, spare).reshape(-1)
    src = jnp.where(real, slot, 0).reshape(-1)
    block_start = jnp.arange(n_blocks, dtype=jnp.int32) * rows
    block_e = jnp.minimum(jnp.sum((pend[None, :] <= block_start[:, None]).astype(jnp.int32), axis=1),
                          N_EXPERTS - 1)
    n_used = (pend[-1:] // rows).astype(jnp.int32)

    n_spare = -(-(2 * bsz * n_unused) // rows) * rows
    xs = _dispatch_call(dest, h1, jnp.zeros((n_slots + n_spare, d), F32), tm)
    ys = _ffn_call(block_e, n_used, xs, w_gate, w_up, w_down, layer, n_blocks)
    return _combine_call(src, h1, rc, lg, lb, ys, alpha, _token_tile(m, 384))


def _rope_tables(lp):
    pos = jnp.arange(lp, dtype=F32) - float(ROW_PAD)
    inv = 1.0 / (ROPE_THETA ** (jnp.arange(0, DA_DIM, 2, dtype=F32) / DA_DIM))
    ang = pos[:, None] * inv[None, :]
    reps = LANES // (DA_DIM // 2)
    sign = jnp.tile(jnp.concatenate([-jnp.ones((DA_DIM // 2,), F32), jnp.ones((DA_DIM // 2,), F32)]), LANES // DA_DIM)
    return jnp.tile(jnp.cos(ang), (1, reps)), jnp.tile(jnp.sin(ang), (1, reps)) * sign[None, :]


def _dup_heads(w, n_heads, dim):
    d = w.shape[0]
    return jnp.broadcast_to(w.reshape(d, n_heads, 1, dim), (d, n_heads, 2, dim)).reshape(d, n_heads * 2 * dim)


def kernel(x, meta, ln_in_g, ln_in_b, w_in, conv_w, conv_b, gate_b, lam_q1, lam_k1, lam_q2, lam_k2, diff_g, sink, mlstm_g, w_branch, w_out, ln1_g, ln1_b, ln2_g, ln2_b, w_rg, b_rg, w_re, b_re, w_gate, w_up, w_down):
    bsz, seq, d = x.shape
    depth = w_in.shape[0]
    l_end = seq + BLOCK
    lp = -(-l_end // MXU_TILE) * MXU_TILE
    m = bsz * lp
    alpha = (2.0 * depth) ** 0.25
    tm = _row_tile(lp, 768)
    tk_attn = _row_tile(lp, 768)

    hp = jnp.concatenate([jnp.zeros((bsz, ROW_PAD, d), x.dtype),
                          jnp.broadcast_to(meta.astype(x.dtype)[None], (bsz, N_META_TOK, d)), x,
                          jnp.zeros((bsz, lp - l_end, d), x.dtype)], axis=1)
    h = _ln_call(hp.reshape(m, d), ln_in_g, ln_in_b, tm)
    cos, sin = _rope_tables(lp)
    q_scale = DA_DIM ** -0.5
    rope_scale = jnp.concatenate([jnp.full((A_Q,), q_scale * math.log2(math.e), F32), jnp.ones((A_K,), F32),
                                  jnp.full((B_Q,), q_scale, F32), jnp.ones((2 * B_K,), F32)]).reshape(1, -1)
    conv_scale = jnp.concatenate([jnp.ones((C_Q,), F32), jnp.full((C_K,), MC_QK ** -0.5, F32)]).reshape(1, -1)

    for l in range(depth):
        lam_init = 0.8 - 0.6 * math.exp(-0.3 * l)
        wl = w_in[l]
        col = lambda i: wl[:, OFFS[i]:OFFS[i + 1]]
        w_rope = jnp.concatenate([col(0), col(1), col(3), _dup_heads(col(4), WB_KV, WB_DIM)], axis=1).astype(MXU_DTYPE)
        w_val = _dup_heads(col(5), WB_KV, WB_DIM).astype(MXU_DTYPE)
        w_vt = jnp.concatenate([col(8), col(2)], axis=1).T.astype(MXU_DTYPE)
        w_conv = jnp.concatenate([col(6), col(7)], axis=1).astype(MXU_DTYPE)
        w_o = col(9).astype(MXU_DTYPE)
        w_g = jnp.pad(col(10), ((0, 0), (0, LANES - C_G))).astype(MXU_DTYPE)
        b_g = jnp.pad(gate_b[l], (0, LANES - C_G)).reshape(1, LANES)
        w_mg = col(11).astype(MXU_DTYPE)

        rq = _proj_rope_call(h, w_rope, cos, sin, rope_scale, tm, lp).reshape(bsz, lp, -1)
        vv = _proj_call(h, w_val, tm, MXU_DTYPE, None, "proj_val").reshape(bsz, lp, -1)
        vt = _proj_t_call(h, w_vt, tm, "proj_val_t")
        zc = _proj_call(h, w_conv, tm, F32, None, "proj_conv").reshape(bsz, lp, -1)
        co = _proj_call(h, w_o, tm, F32, "sigmoid", "proj_ogate")
        gc, gr = _gates_call(h, w_g, b_g, tm, lp, l_end)

        lamv = jnp.stack([lam_q1[l], lam_k1[l], lam_q2[l], lam_k2[l]])
        out_a = _attn_a_call(rq, vt, lamv, diff_g[l].reshape(-1, 1), lam_init, l_end, MXU_TILE, tk_attn)
        out_b = _attn_b_call(rq, vv, sink[l].reshape(1, -1), l_end)
        qk = _conv_call(zc, conv_w[l], conv_b[l].reshape(1, -1), conv_scale, l_end, _row_tile(lp, 768))
        h_f, h_b = _mlstm_call(qk, vt, gc, gr)

        h = _merge_call(h, out_a.reshape(m, -1), out_b.reshape(m, -1), h_f.reshape(m, -1), h_b.reshape(m, -1),
                        co, mlstm_g[l].reshape(1, -1), w_mg, w_branch[l].astype(MXU_DTYPE),
                        w_out[l].astype(MXU_DTYPE), ln1_g[l].reshape(1, -1), ln1_b[l].reshape(1, -1),
                        alpha, _token_tile(m, 384))
        h = _moe(h, w_rg[l], b_rg[l], w_re[l], b_re[l], w_gate, w_up, w_down, l,
                 ln2_g[l].reshape(1, -1), ln2_b[l].reshape(1, -1), alpha, bsz, lp, l_end, tm)
    return h.reshape(bsz, lp, d)[:, BLOCK:l_end]
```

```python
import functools
import math

import numpy as np
import jax
import jax.numpy as jnp
from jax import lax
from jax.experimental import pallas as pl
from jax.experimental.pallas import tpu as pltpu

D_MODEL = 1024
N_META_TOK = 16
BLOCK = 128
ROW_PAD = BLOCK - N_META_TOK
ROPE_THETA = 10000.0
LN_EPS = 1e-5
NEG = -1e30

DA_HEADS = 4
DA_DIM = 64
WB_HEADS = 8
WB_KV = 2
WB_DIM = 64
WINDOW = 128
MC_HEADS = 4
MC_QK = 128
MC_V = 128
N_BRANCH = 3
BRANCH_W = 512
N_GROUPS = 4
EXP_PER_GROUP = 8
N_EXPERTS = N_GROUPS * EXP_PER_GROUP
D_EXPERT = 512

A_Q = DA_HEADS * 2 * DA_DIM
A_K = A_Q
A_V = A_Q
B_Q = WB_HEADS * WB_DIM
B_K = WB_KV * WB_DIM
B_V = B_K
C_Q = MC_HEADS * MC_QK
C_K = C_Q
C_V = MC_HEADS * MC_V
C_O = C_V
C_G = 4 * MC_HEADS
GATE_W = N_BRANCH * D_MODEL
SPLITS = (A_Q, A_K, A_V, B_Q, B_K, B_V, C_Q, C_K, C_V, C_O, C_G, GATE_W)
OFFS = tuple(int(v) for v in np.cumsum((0,) + SPLITS))

LANES = 128
MXU_TILE = 256
EXPERT_ROWS = 512
MLSTM_EXT = 16
DMA_ISSUE_UNROLL = 8
ATTN_UNROLL = 2
ATTN_EXT = 16
VMEM_LIMIT = 56 * 1024 * 1024

F32 = jnp.float32
MXU_DTYPE = jnp.bfloat16


def _dot(a, b):
    return jnp.dot(a, b, preferred_element_type=F32)


def _dot_nt(a, b):
    return lax.dot_general(a, b, (((1,), (1,)), ((), ())), preferred_element_type=F32)


def _dot_tn(a, b):
    return lax.dot_general(a, b, (((0,), (0,)), ((), ())), preferred_element_type=F32)


def _params(n_axes, flags=None):
    return pltpu.CompilerParams(dimension_semantics=("arbitrary",) * n_axes,
                                vmem_limit_bytes=VMEM_LIMIT, flags=flags)


def _row_tile(n_rows, target):
    best = BLOCK
    for t in range(BLOCK, target + 1, BLOCK):
        if n_rows % t == 0:
            best = t
    return best


def _token_tile(n_rows, target):
    best = 8
    for t in range(8, target + 1, 8):
        if n_rows % t == 0:
            best = t
    return best


def _layer_norm(x, g, b):
    mu = jnp.mean(x, axis=-1, keepdims=True)
    xc = x - mu
    var = jnp.mean(xc * xc, axis=-1, keepdims=True)
    return xc * lax.rsqrt(var + LN_EPS) * g + b


def _ln_kernel(x_ref, g_ref, b_ref, o_ref):
    o_ref[...] = _layer_norm(x_ref[...], g_ref[...], b_ref[...])


def _ln_call(x, g, b, tm):
    m, d = x.shape
    return pl.pallas_call(
        _ln_kernel,
        grid=(m // tm,),
        in_specs=[pl.BlockSpec((tm, d), lambda i: (i, 0)),
                  pl.BlockSpec((1, d), lambda i: (0, 0)),
                  pl.BlockSpec((1, d), lambda i: (0, 0))],
        out_specs=pl.BlockSpec((tm, d), lambda i: (i, 0)),
        out_shape=jax.ShapeDtypeStruct((m, d), F32),
        compiler_params=_params(1),
        name="ln_in",
    )(x, g.reshape(1, d), b.reshape(1, d))


def _proj_rope_kernel(x_ref, w_ref, cos_ref, sin_ref, scale_ref, o_ref, *, n_rope):
    z = _dot(x_ref[...].astype(MXU_DTYPE), w_ref[...])
    cos = cos_ref[...]
    sin = sin_ref[...]
    lane = lax.broadcasted_iota(jnp.int32, cos.shape, 1)
    first_half = (lane % DA_DIM) < (DA_DIM // 2)
    for c in range(n_rope // LANES):
        sl = slice(c * LANES, (c + 1) * LANES)
        zc = z[:, sl]
        partner = jnp.where(first_half, pltpu.roll(zc, LANES - DA_DIM // 2, 1),
                            pltpu.roll(zc, DA_DIM // 2, 1))
        o_ref[:, sl] = ((zc * cos + partner * sin) * scale_ref[:, sl]).astype(o_ref.dtype)
    o_ref[:, n_rope:] = z[:, n_rope:].astype(o_ref.dtype)


def _proj_rope_call(h, w, cos, sin, scale, tm, lp):
    m, d = h.shape
    n = w.shape[1]
    per_batch = lp // tm
    return pl.pallas_call(
        functools.partial(_proj_rope_kernel, n_rope=scale.shape[1]),
        grid=(m // tm,),
        in_specs=[pl.BlockSpec((tm, d), lambda i: (i, 0)),
                  pl.BlockSpec((d, n), lambda i: (0, 0)),
                  pl.BlockSpec((tm, LANES), lambda i: (i % per_batch, 0)),
                  pl.BlockSpec((tm, LANES), lambda i: (i % per_batch, 0)),
                  pl.BlockSpec(scale.shape, lambda i: (0, 0))],
        out_specs=pl.BlockSpec((tm, n), lambda i: (i, 0)),
        out_shape=jax.ShapeDtypeStruct((m, n), MXU_DTYPE),
        compiler_params=_params(1),
        name="proj_rope",
    )(h, w, cos, sin, scale)


def _proj_kernel(x_ref, w_ref, o_ref, *, sigmoid_from):
    z = _dot(x_ref[...].astype(MXU_DTYPE), w_ref[...])
    if sigmoid_from is None:
        o_ref[...] = z.astype(o_ref.dtype)
    else:
        o_ref[:, :sigmoid_from] = z[:, :sigmoid_from].astype(o_ref.dtype)
        o_ref[:, sigmoid_from:] = jax.nn.sigmoid(z[:, sigmoid_from:]).astype(o_ref.dtype)


def _proj_call(h, w, tm, out_dtype, sigmoid_from, name):
    m, d = h.shape
    n = w.shape[1]
    return pl.pallas_call(
        functools.partial(_proj_kernel, sigmoid_from=sigmoid_from),
        grid=(m // tm,),
        in_specs=[pl.BlockSpec((tm, d), lambda i: (i, 0)),
                  pl.BlockSpec((d, n), lambda i: (0, 0))],
        out_specs=pl.BlockSpec((tm, n), lambda i: (i, 0)),
        out_shape=jax.ShapeDtypeStruct((m, n), out_dtype),
        compiler_params=_params(1),
        name=name,
    )(h, w)


def _proj_t_kernel(x_ref, wt_ref, o_ref):
    o_ref[...] = _dot_nt(wt_ref[...], x_ref[...].astype(MXU_DTYPE)).astype(o_ref.dtype)


def _proj_t_call(h, wt, tm, name):
    m, d = h.shape
    n = wt.shape[0]
    return pl.pallas_call(
        _proj_t_kernel,
        grid=(m // tm,),
        in_specs=[pl.BlockSpec((tm, d), lambda i: (i, 0)),
                  pl.BlockSpec((n, d), lambda i: (0, 0))],
        out_specs=pl.BlockSpec((n, tm), lambda i: (0, i)),
        out_shape=jax.ShapeDtypeStruct((n, m), MXU_DTYPE),
        compiler_params=_params(1),
        name=name,
    )(h, wt)


def _gates_kernel(x_ref, w_ref, b_ref, gc_ref, gr_ref, *, tm, lp, l_end):
    z = _dot(x_ref[...].astype(MXU_DTYPE), w_ref[...]) + b_ref[...]
    lane = lax.broadcasted_iota(jnp.int32, (tm, LANES), 1)
    kind = lane // MC_HEADS
    row = lax.broadcasted_iota(jnp.int32, (tm, LANES), 0) + pl.program_id(0) * tm
    pos = row % lp
    unused = (pos < ROW_PAD) | (pos >= l_end)
    log_f = jnp.minimum(z, 0.0) - jnp.log1p(jnp.exp(-jnp.abs(z)))
    is_forget = (kind % 2) == 1
    base = jnp.where(is_forget, jnp.where(unused, 0.0, log_f), jnp.where(unused, NEG, z))
    r128 = lax.broadcasted_iota(jnp.int32, (BLOCK, LANES), 0)
    fwd_lane = lax.broadcasted_iota(jnp.int32, (BLOCK, LANES), 1) // MC_HEADS == 1
    forget128 = (lax.broadcasted_iota(jnp.int32, (BLOCK, LANES), 1) // MC_HEADS) % 2 == 1
    for c in range(tm // BLOCK):
        x = base[c * BLOCK:(c + 1) * BLOCK]
        pre = x
        suf = x
        s = 1
        while s < BLOCK:
            pre = pre + jnp.where(r128 >= s, pltpu.roll(pre, s, 0), 0.0)
            suf = suf + jnp.where(r128 < BLOCK - s, pltpu.roll(suf, BLOCK - s, 0), 0.0)
            s *= 2
        out = jnp.where(forget128, jnp.where(fwd_lane, pre, suf), x)
        gc_ref[c * BLOCK:(c + 1) * BLOCK, :] = out
        gr_ref[:, c * BLOCK:(c + 1) * BLOCK] = out.T[0:C_G, :]


def _gates_call(h, w, b, tm, lp, l_end):
    m, d = h.shape
    return pl.pallas_call(
        functools.partial(_gates_kernel, tm=tm, lp=lp, l_end=l_end),
        grid=(m // tm,),
        in_specs=[pl.BlockSpec((tm, d), lambda i: (i, 0)),
                  pl.BlockSpec((d, LANES), lambda i: (0, 0)),
                  pl.BlockSpec((1, LANES), lambda i: (0, 0))],
        out_specs=[pl.BlockSpec((tm, LANES), lambda i: (i, 0)),
                   pl.BlockSpec((C_G, tm), lambda i: (0, i))],
        out_shape=[jax.ShapeDtypeStruct((m, LANES), F32),
                   jax.ShapeDtypeStruct((C_G, m), F32)],
        compiler_params=_params(1),
        name="mlstm_gates",
    )(h, w, b)


def _conv_kernel(z_ref, w_ref, b_ref, scale_ref, o_ref, *, lp, l_end, tr):
    w0 = w_ref[0:1, :]
    w1 = w_ref[1:2, :]
    w2 = w_ref[2:3, :]
    row = lax.broadcasted_iota(jnp.int32, (tr, LANES), 0)
    for c in range(lp // tr):
        r0 = c * tr
        zc = z_ref[0, r0:r0 + tr, :]
        before = jnp.zeros((1, LANES), F32) if r0 == 0 else z_ref[0, r0 - 1:r0, :]
        after = jnp.zeros((1, LANES), F32) if r0 + tr == lp else z_ref[0, r0 + tr:r0 + tr + 1, :]
        prev = jnp.where(row == 0, before, pltpu.roll(zc, 1, 0))
        nxt = jnp.where(row == tr - 1, after, pltpu.roll(zc, tr - 1, 0))
        if r0 <= ROW_PAD < r0 + tr:
            prev = jnp.where(row == ROW_PAD - r0, 0.0, prev)
        if r0 <= l_end - 1 < r0 + tr:
            nxt = jnp.where(row == l_end - 1 - r0, 0.0, nxt)
        y = prev * w0 + zc * w1 + nxt * w2 + b_ref[...]
        o_ref[0, r0:r0 + tr, :] = (jax.nn.silu(y) * scale_ref[...]).astype(o_ref.dtype)


def _conv_call(z, w, b, scale, l_end, tr):
    bsz, lp, _ = z.shape
    n = w.shape[1]
    return pl.pallas_call(
        functools.partial(_conv_kernel, lp=lp, l_end=l_end, tr=tr),
        grid=(bsz, n // LANES),
        in_specs=[pl.BlockSpec((1, lp, LANES), lambda b_, j: (b_, 0, j)),
                  pl.BlockSpec((3, LANES), lambda b_, j: (0, j)),
                  pl.BlockSpec((1, LANES), lambda b_, j: (0, j)),
                  pl.BlockSpec((1, LANES), lambda b_, j: (0, j))],
        out_specs=pl.BlockSpec((1, lp, LANES), lambda b_, j: (b_, 0, j)),
        out_shape=jax.ShapeDtypeStruct((bsz, lp, n), MXU_DTYPE),
        compiler_params=_params(2),
        name="mlstm_conv",
    )(z, w, b, scale)


def _attn_a_kernel(lamv_ref, g_ref, q_ref, k_ref, vt_ref, o_ref, s0_scr, s1_scr, acc_scr, *,
                   tk, n_chunks, l_end, lam_init):
    q = q_ref[0]
    tq = q.shape[0]
    qt = q.astype(F32).T.astype(MXU_DTYPE)
    feat = lax.broadcasted_iota(jnp.int32, qt.shape, 0)
    key_row = lax.broadcasted_iota(jnp.int32, (tk, tq), 0)
    zero = jnp.zeros_like(qt)
    qz = (jnp.where(feat < DA_DIM, qt, zero), jnp.where(feat >= DA_DIM, qt, zero))
    s_bufs = (s0_scr, s1_scr)

    def scores(j, slot):
        start = pl.multiple_of(j * tk, tk)
        kj = k_ref[0, pl.ds(start, tk), :]
        for c in range(2):
            s_bufs[slot][c] = _dot(kj, qz[c])

    ones_rows = jnp.ones((ATTN_EXT, tk), MXU_DTYPE)

    def softmax_values(j, slot, stats, lo, hi):
        start = pl.multiple_of(j * tk, tk)
        vt = jnp.concatenate([vt_ref[:, pl.ds(start, tk)], ones_rows], axis=0)
        new_stats = []
        for c in range(2):
            m = stats[c]
            s = s_bufs[slot][c]
            if lo is not None:
                s = jnp.where(key_row >= lo, s, NEG)
            if hi is not None:
                s = jnp.where(key_row < hi, s, NEG)
            m_new = jnp.maximum(m, jnp.max(s, axis=0, keepdims=True))
            alpha = jnp.exp2(m - m_new)
            p = jnp.exp2((s - m_new).astype(MXU_DTYPE))
            acc_scr[c] = alpha * acc_scr[c] + _dot(vt, p)
            new_stats.append(m_new)
        return tuple(new_stats)

    one = jnp.full((1, tq), NEG, F32)
    acc_scr[...] = jnp.zeros_like(acc_scr)
    last = n_chunks - 1
    last_hi = l_end - last * tk
    last_hi = None if last_hi == tk else last_hi
    scores(0, 0)
    if n_chunks == 1:
        softmax_values(0, 0, (one, one), ROW_PAD, last_hi)
    else:
        def step(j, parity, stats, lo):
            scores(j + 1, 1 - parity)
            return softmax_values(j, parity, stats, lo, None)

        def trip(i, st):
            for u in range(ATTN_UNROLL):
                st = step(1 + ATTN_UNROLL * i + u, (1 + u) % 2, st, None)
            return st

        stats = step(0, 0, (one, one), ROW_PAD)
        n_trips = (last - 1) // ATTN_UNROLL
        stats = lax.fori_loop(0, n_trips, trip, stats)
        for j in range(1 + n_trips * ATTN_UNROLL, last):
            stats = step(j, j % 2, stats, None)
        softmax_values(last, last % 2, stats, None, last_hi)
    dv = 2 * DA_DIM
    o0 = acc_scr[0, 0:dv, :] / acc_scr[0, dv:dv + 1, :]
    o1 = acc_scr[1, 0:dv, :] / acc_scr[1, dv:dv + 1, :]
    lv = lamv_ref[...]
    lam = (jnp.exp(jnp.sum(lv[0:1] * lv[1:2], axis=-1, keepdims=True))
           - jnp.exp(jnp.sum(lv[2:3] * lv[3:4], axis=-1, keepdims=True)) + lam_init)
    o = o0 - lam * o1
    ms = jnp.mean(o * o, axis=0, keepdims=True)
    o = o * lax.rsqrt(ms + LN_EPS) * g_ref[...] * (1.0 - lam_init)
    o_ref[0] = o.T.astype(o_ref.dtype)


def _attn_a_call(rq, vt, lamv, g_col, lam_init, l_end, tq, tk):
    bsz, lp, _ = rq.shape
    k_blk = A_Q // LANES
    vt_blk = C_V // LANES
    return pl.pallas_call(
        functools.partial(_attn_a_kernel, tk=tk, n_chunks=lp // tk, l_end=l_end, lam_init=lam_init),
        grid=(bsz, DA_HEADS, lp // tq),
        in_specs=[pl.BlockSpec((4, DA_DIM), lambda b, h, i: (0, 0)),
                  pl.BlockSpec((2 * DA_DIM, 1), lambda b, h, i: (0, 0)),
                  pl.BlockSpec((1, tq, LANES), lambda b, h, i: (b, i, h)),
                  pl.BlockSpec((1, lp, LANES), lambda b, h, i: (b, 0, k_blk + h)),
                  pl.BlockSpec((2 * DA_DIM, lp), lambda b, h, i: (vt_blk + h, b))],
        out_specs=pl.BlockSpec((1, tq, LANES), lambda b, h, i: (b, i, h)),
        out_shape=jax.ShapeDtypeStruct((bsz, lp, A_V), MXU_DTYPE),
        scratch_shapes=[pltpu.VMEM((2, tk, tq), F32),
                        pltpu.VMEM((2, tk, tq), F32),
                        pltpu.VMEM((2, 2 * DA_DIM + ATTN_EXT, tq), F32)],
        compiler_params=_params(3),
        name="diff_attn",
    )(lamv, g_col, rq, rq, vt)


def _attn_b_kernel(sink_ref, band_ref, q_ref, k0_ref, k1_ref, v0_ref, v1_ref, o_ref, *, lp, l_end):
    n = pl.program_id(1)
    nb = lp // BLOCK
    grp = WB_HEADS // WB_KV

    def blocks(ref):
        parts = [ref[0, 0:BLOCK, :]]
        for d in (-1, 0, 1):
            idx = jnp.clip(n + d, 0, nb - 1)
            parts.append(ref[0, pl.ds(pl.multiple_of(idx * BLOCK, BLOCK), BLOCK), :])
        return jnp.concatenate(parts, axis=0)

    keys = (blocks(k0_ref), blocks(k1_ref))
    vals = (blocks(v0_ref), blocks(v1_ref))
    blk_bias = []
    for d in (-1, 0, 1):
        inside = jnp.logical_and(n + d >= 1, n + d <= l_end // BLOCK - 1)
        blk_bias.append(jnp.where(inside, 0.0, NEG))
    head_of_col = lax.broadcasted_iota(jnp.int32, (1, grp * BLOCK), 1) // BLOCK
    lane = lax.broadcasted_iota(jnp.int32, (BLOCK, LANES), 1)
    sinks, raw = [], []
    for g in range(WB_KV):
        qs = []
        sink = jnp.zeros((1, grp * BLOCK), F32)
        for j in range(grp):
            h = g * grp + j
            qt = q_ref[0, :, (h // 2) * LANES:(h // 2 + 1) * LANES]
            keep = (lane >= WB_DIM) if h % 2 else (lane < WB_DIM)
            qs.append(jnp.where(keep, qt, jnp.zeros_like(qt)))
            sink = jnp.where(head_of_col == j, sink_ref[:, h:h + 1], sink)
        sinks.append(sink)
        raw.append(_dot_nt(keys[g], jnp.concatenate(qs, axis=0)))
    weights, dens = [], []
    for g in range(WB_KV):
        s = raw[g] + band_ref[...]
        s = jnp.concatenate([s[0:BLOCK]] + [s[(i + 1) * BLOCK:(i + 2) * BLOCK] + blk_bias[i] for i in range(3)],
                            axis=0)
        m = jnp.maximum(jnp.max(s, axis=0, keepdims=True), sinks[g])
        p = jnp.exp(s - m)
        dens.append(jnp.sum(p, axis=0, keepdims=True) + jnp.exp(sinks[g] - m))
        weights.append(p.astype(MXU_DTYPE))
    outs = [_dot_tn(vals[g], weights[g]) for g in range(WB_KV)]
    for g in range(WB_KV):
        o = (outs[g] / dens[g]).T
        for jj in range(grp // 2):
            lo = o[(2 * jj) * BLOCK:(2 * jj + 1) * BLOCK]
            hi = o[(2 * jj + 1) * BLOCK:(2 * jj + 2) * BLOCK]
            t = (g * grp) // 2 + jj
            o_ref[0, :, t * LANES:(t + 1) * LANES] = jnp.where(lane < WB_DIM, lo, hi).astype(o_ref.dtype)


def _attn_b_call(rq, sink, l_end):
    bsz, lp, _ = rq.shape
    q_blk = (A_Q + A_K) // B_Q
    k_blk = (A_Q + A_K + B_Q) // LANES
    v_blk = k_blk + 2 * B_K // LANES
    grp = WB_HEADS // WB_KV
    krow = np.arange(4 * BLOCK)[:, None]
    qoff = np.arange(grp * BLOCK)[None, :] % BLOCK
    ok = np.where(krow < BLOCK, krow >= ROW_PAD, np.abs(qoff + 2 * BLOCK - krow) <= WINDOW)
    band = jnp.asarray(np.where(ok, 0.0, NEG), F32)
    seq = lambda c: pl.BlockSpec((1, lp, LANES), lambda b, n: (b, 0, c))
    return pl.pallas_call(
        functools.partial(_attn_b_kernel, lp=lp, l_end=l_end),
        grid=(bsz, lp // BLOCK),
        in_specs=[pl.BlockSpec((1, WB_HEADS), lambda b, n: (0, 0)),
                  pl.BlockSpec(band.shape, lambda b, n: (0, 0)),
                  pl.BlockSpec((1, BLOCK, B_Q), lambda b, n: (b, n, q_blk)),
                  seq(k_blk), seq(k_blk + 1), seq(v_blk), seq(v_blk + 1)],
        out_specs=pl.BlockSpec((1, BLOCK, B_Q), lambda b, n: (b, n, 0)),
        out_shape=jax.ShapeDtypeStruct((bsz, lp, B_Q), MXU_DTYPE),
        compiler_params=_params(2),
        name="window_attn",
    )(sink, band, rq, rq, rq, rq, rq)


def _mlstm_kernel(qkf_ref, vf_ref, gcf_ref, grf_ref, qkb_ref, vb_ref, gcb_ref, grb_ref,
                  hf_ref, hb_ref, c_scr, m_scr):
    t = pl.program_id(1)

    @pl.when(t == 0)
    def _():
        c_scr[...] = jnp.zeros_like(c_scr)
        m_scr[...] = jnp.zeros_like(m_scr)

    srow = lax.broadcasted_iota(jnp.int32, (BLOCK, BLOCK), 0)
    ccol = lax.broadcasted_iota(jnp.int32, (BLOCK, BLOCK), 1)
    ext_row = lax.broadcasted_iota(jnp.int32, (MLSTM_EXT, BLOCK), 0)
    ones_rows = jnp.where(ext_row == 0, 1.0, 0.0).astype(MXU_DTYPE)
    dirs = ((qkf_ref, vf_ref, gcf_ref, grf_ref, hf_ref, srow <= ccol, BLOCK - 1),
            (qkb_ref, vb_ref, gcb_ref, grb_ref, hb_ref, srow >= ccol, 0))
    chains = []
    for d, (qk_ref, vt_ref, gc_ref, gr_ref, h_ref, tri, last) in enumerate(dirs):
        for hd in range(MC_HEADS):
            ci = d * MC_HEADS + hd
            j_li = (2 * d) * MC_HEADS + hd
            j_b = (2 * d + 1) * MC_HEADS + hd
            vt = vt_ref[hd * MC_V:(hd + 1) * MC_V, :]
            b_row = gr_ref[j_b:j_b + 1, :]
            chains.append(dict(
                ci=ci, hd=hd, h_ref=h_ref, tri=tri,
                q=qk_ref[0, :, hd * MC_QK:(hd + 1) * MC_QK],
                k=qk_ref[0, :, C_Q + hd * MC_QK:C_Q + (hd + 1) * MC_QK],
                vext=jnp.concatenate([vt, ones_rows], axis=0),
                key_col=gc_ref[:, j_li:j_li + 1] - gc_ref[:, j_b:j_b + 1],
                li_row=gr_ref[j_li:j_li + 1, :], b_row=b_row,
                g=b_row[:, last:last + 1],
                m_prev=m_scr[ci, 0:1, 0:1],
                c_prev=c_scr[ci]))
    for ch in chains:
        ch["kq"] = _dot_nt(ch["k"], ch["q"])
        ch["cq"] = _dot_nt(ch["c_prev"].astype(MXU_DTYPE), ch["q"])
    for ch in chains:
        dmat = jnp.where(ch["tri"], ch["b_row"] + ch["key_col"], NEG)
        m_t = jnp.maximum(ch["b_row"] + ch["m_prev"], jnp.max(dmat, axis=0, keepdims=True))
        ch["m_t"] = m_t
        ch["inter"] = jnp.exp(ch["b_row"] + ch["m_prev"] - m_t)
        ch["s"] = (ch["kq"] * jnp.exp(dmat - m_t)).astype(MXU_DTYPE)
        a_row = ch["g"] - ch["b_row"] + ch["li_row"]
        m_new = jnp.maximum(ch["g"] + ch["m_prev"], jnp.max(a_row, axis=1, keepdims=True))
        ch["m_new"] = m_new
        ch["decay"] = jnp.exp(ch["g"] + ch["m_prev"] - m_new)
        ch["vw"] = (ch["vext"].astype(F32) * jnp.exp(a_row - m_new)).astype(MXU_DTYPE)
    for ch in chains:
        ch["vs"] = _dot(ch["vext"], ch["s"])
        ch["dc"] = _dot(ch["vw"], ch["k"])
    for ch in chains:
        nd = ch["inter"] * ch["cq"] + ch["vs"]
        den = nd[MC_V:MC_V + 1, :]
        h_t = nd[0:MC_V, :] / jnp.maximum(jnp.abs(den), jnp.exp(-ch["m_t"]))
        ch["h_ref"][0, :, ch["hd"] * MC_V:(ch["hd"] + 1) * MC_V] = h_t.T
        c_scr[ch["ci"]] = ch["decay"] * ch["c_prev"] + ch["dc"]
        m_scr[ch["ci"]] = jnp.broadcast_to(ch["m_new"], m_scr.shape[1:])


def _mlstm_call(qk, vt, gc, gr):
    bsz, lp, _ = qk.shape
    nch = lp // BLOCK
    fwd = lambda b, t: (b, t, 0)
    bwd = lambda b, t: (b, nch - 1 - t, 0)
    return pl.pallas_call(
        _mlstm_kernel,
        grid=(bsz, nch),
        in_specs=[pl.BlockSpec((1, BLOCK, C_Q + C_K), fwd),
                  pl.BlockSpec((C_V, BLOCK), lambda b, t: (0, b * nch + t)),
                  pl.BlockSpec((BLOCK, LANES), lambda b, t: (b * nch + t, 0)),
                  pl.BlockSpec((C_G, BLOCK), lambda b, t: (0, b * nch + t)),
                  pl.BlockSpec((1, BLOCK, C_Q + C_K), bwd),
                  pl.BlockSpec((C_V, BLOCK), lambda b, t: (0, b * nch + nch - 1 - t)),
                  pl.BlockSpec((BLOCK, LANES), lambda b, t: (b * nch + nch - 1 - t, 0)),
                  pl.BlockSpec((C_G, BLOCK), lambda b, t: (0, b * nch + nch - 1 - t))],
        out_specs=[pl.BlockSpec((1, BLOCK, C_V), fwd),
                   pl.BlockSpec((1, BLOCK, C_V), bwd)],
        out_shape=[jax.ShapeDtypeStruct((bsz, lp, C_V), F32),
                   jax.ShapeDtypeStruct((bsz, lp, C_V), F32)],
        scratch_shapes=[pltpu.VMEM((2 * MC_HEADS, MC_V + MLSTM_EXT, MC_QK), F32),
                        pltpu.VMEM((2 * MC_HEADS, 8, LANES), F32)],
        compiler_params=_params(2),
        name="mlstm_scan",
    )(qk, vt, gc, gr, qk, vt, gc, gr)


def _merge_kernel(h_ref, oa_ref, ob_ref, hf_ref, hb_ref, co_ref, mg_ref, wg_ref, wb_ref, wo_ref,
                  lg_ref, lb_ref, o_ref, *, alpha):
    h = h_ref[...]
    hx = h.astype(MXU_DTYPE)
    hc = hf_ref[...] + hb_ref[...]
    parts = []
    for hd in range(MC_HEADS):
        sl = slice(hd * MC_V, (hd + 1) * MC_V)
        x = hc[:, sl]
        mu = jnp.mean(x, axis=-1, keepdims=True)
        xc = x - mu
        var = jnp.mean(xc * xc, axis=-1, keepdims=True)
        parts.append(xc * lax.rsqrt(var + LN_EPS) * mg_ref[:, sl] * co_ref[:, sl])
    oc = jnp.concatenate(parts, axis=1).astype(MXU_DTYPE)
    branches = (oa_ref[...], ob_ref[...], oc)
    merged = None
    for br in range(N_BRANCH):
        gate = jax.nn.sigmoid(_dot(hx, wg_ref[:, br * D_MODEL:(br + 1) * D_MODEL]))
        term = gate * _dot(branches[br], wb_ref[br])
        merged = term if merged is None else merged + term
    y = _dot(merged.astype(MXU_DTYPE), wo_ref[...])
    o_ref[...] = _layer_norm(alpha * h + y, lg_ref[...], lb_ref[...])


def _merge_call(h, oa, ob, hf, hb, co, mg, wg, wb, wo, lg, lb, alpha, tm):
    m, d = h.shape
    rows = lambda n: pl.BlockSpec((tm, n), lambda i: (i, 0))
    full2 = lambda a: pl.BlockSpec(a.shape, lambda i: (0, 0))
    return pl.pallas_call(
        functools.partial(_merge_kernel, alpha=alpha),
        grid=(m // tm,),
        in_specs=[rows(d), rows(A_V), rows(B_Q), rows(C_V), rows(C_V),
                  pl.BlockSpec((tm, C_O), lambda i: (i, (C_Q + C_K) // C_O)),
                  full2(mg), full2(wg), pl.BlockSpec(wb.shape, lambda i: (0, 0, 0)), full2(wo),
                  full2(lg), full2(lb)],
        out_specs=rows(d),
        out_shape=jax.ShapeDtypeStruct((m, d), F32),
        compiler_params=_params(1),
        name="merge_ln1",
    )(h, oa, ob, hf, hb, co, mg, wg, wb, wo, lg, lb)


def _split3(x):
    hi = x.astype(MXU_DTYPE)
    lo = (x - hi.astype(F32)).astype(MXU_DTYPE)
    return hi, lo


def _router_kernel(h_ref, w_ref, b_ref, rt_ref, rc_ref, *, tm, lp, l_end):
    x_hi, x_lo = _split3(h_ref[...])
    w_hi, w_lo = _split3(w_ref[...])
    logits = (_dot_nt(w_hi, x_hi) + _dot_nt(w_hi, x_lo) + _dot_nt(w_lo, x_hi)) + b_ref[...]
    none = float(N_EXPERTS)
    gl = logits[N_EXPERTS:N_EXPERTS + 8]
    grow = lax.broadcasted_iota(jnp.int32, gl.shape, 0).astype(F32)
    gmax = jnp.max(gl, axis=0, keepdims=True)
    g_sel = jnp.min(jnp.where(gl == gmax, grow, none), axis=0, keepdims=True)
    p_grp = 1.0 / jnp.sum(jnp.exp(gl - gmax), axis=0, keepdims=True)
    el = logits[0:N_EXPERTS]
    erow_i = lax.broadcasted_iota(jnp.int32, el.shape, 0)
    erow = erow_i.astype(F32)
    cand = jnp.where((erow_i // EXP_PER_GROUP).astype(F32) == g_sel, el, -jnp.inf)
    top1 = jnp.max(cand, axis=0, keepdims=True)
    i1 = jnp.min(jnp.where(cand == top1, erow, none), axis=0, keepdims=True)
    cand2 = jnp.where(erow == i1, -jnp.inf, cand)
    top2 = jnp.max(cand2, axis=0, keepdims=True)
    i2 = jnp.min(jnp.where(cand2 == top2, erow, none), axis=0, keepdims=True)
    e = jnp.exp(top2 - top1)
    w1 = (1.0 / (1.0 + e)) * p_grp
    w2 = (e / (1.0 + e)) * p_grp
    pos = lax.broadcasted_iota(jnp.int32, (1, tm), 1) + pl.program_id(0) * tm
    real = ((pos % lp) >= ROW_PAD) & ((pos % lp) < l_end)
    e1 = jnp.where(real, i1, none)
    e2 = jnp.where(real, i2, none)
    r = lax.broadcasted_iota(jnp.int32, (LANES, tm), 0)
    table = jnp.where(r == 0, e1, jnp.where(r == 1, e2, jnp.where(r == 2, w1, jnp.where(r == 3, w2, 0.0))))
    rt_ref[...] = table[0:8]
    rc_ref[...] = table.T


def _router_call(h, w, b, tm, lp, l_end):
    m, d = h.shape
    return pl.pallas_call(
        functools.partial(_router_kernel, tm=tm, lp=lp, l_end=l_end),
        grid=(m // tm,),
        in_specs=[pl.BlockSpec((tm, d), lambda i: (i, 0)),
                  pl.BlockSpec((LANES, d), lambda i: (0, 0)),
                  pl.BlockSpec((LANES, 1), lambda i: (0, 0))],
        out_specs=[pl.BlockSpec((8, tm), lambda i: (0, i)),
                   pl.BlockSpec((tm, LANES), lambda i: (i, 0))],
        out_shape=[jax.ShapeDtypeStruct((8, m), F32),
                   jax.ShapeDtypeStruct((m, LANES), F32)],
        compiler_params=_params(1),
        name="moe_router",
    )(h, w, b)


def _rank_kernel(rt_ref, rk_ref, cnt_ref, carry, *, tm):
    @pl.when(pl.program_id(0) == 0)
    def _():
        carry[...] = jnp.zeros_like(carry)

    erow = lax.broadcasted_iota(jnp.int32, (N_EXPERTS, tm), 0).astype(F32)
    oh1 = jnp.where(erow == rt_ref[0:1, :], 1.0, 0.0)
    oh2 = jnp.where(erow == rt_ref[1:2, :], 1.0, 0.0)
    oh = oh1 + oh2
    earlier = (lax.broadcasted_iota(jnp.int32, (tm, tm), 0)
               < lax.broadcasted_iota(jnp.int32, (tm, tm), 1))
    before = _dot(oh.astype(MXU_DTYPE), jnp.where(earlier, 1.0, 0.0).astype(MXU_DTYPE)) + carry[:, 0:1]
    r1 = jnp.sum(oh1 * before, axis=0, keepdims=True)
    r2 = jnp.sum(oh2 * before, axis=0, keepdims=True)
    r = lax.broadcasted_iota(jnp.int32, (8, tm), 0)
    rk_ref[...] = jnp.where(r == 0, r1, jnp.where(r == 1, r2, 0.0))
    total = carry[...] + jnp.sum(oh, axis=1, keepdims=True)
    carry[...] = total
    cnt_ref[...] = total


def _rank_call(rt, tm):
    m = rt.shape[1]
    return pl.pallas_call(
        functools.partial(_rank_kernel, tm=tm),
        grid=(m // tm,),
        in_specs=[pl.BlockSpec((8, tm), lambda i: (0, i))],
        out_specs=[pl.BlockSpec((8, tm), lambda i: (0, i)),
                   pl.BlockSpec((N_EXPERTS, LANES), lambda i: (0, 0))],
        out_shape=[jax.ShapeDtypeStruct((8, m), F32),
                   jax.ShapeDtypeStruct((N_EXPERTS, LANES), F32)],
        scratch_shapes=[pltpu.VMEM((N_EXPERTS, LANES), F32)],
        compiler_params=_params(1),
        name="moe_rank",
    )(rt)


def _dispatch_kernel(dest_ref, h_ref, xs_in_ref, xs_ref, sem, *, tm, m):
    del xs_in_ref
    base = pl.program_id(0) * tm

    def body(r, c):
        for k in range(2):
            d = dest_ref[k * m + base + r]
            pltpu.make_async_copy(h_ref.at[pl.ds(r, 1)], xs_ref.at[pl.ds(d, 1)], sem).start()
        return c

    lax.fori_loop(0, tm, body, 0, unroll=DMA_ISSUE_UNROLL)
    for _ in range(2):
        pltpu.make_async_copy(h_ref, xs_ref.at[pl.ds(0, tm)], sem).wait()


def _dispatch_call(dest, h, xs0, tm):
    m, d = h.shape
    return pl.pallas_call(
        functools.partial(_dispatch_kernel, tm=tm, m=m),
        grid_spec=pltpu.PrefetchScalarGridSpec(
            num_scalar_prefetch=1,
            grid=(m // tm,),
            in_specs=[pl.BlockSpec((tm, d), lambda i, dest_: (i, 0)),
                      pl.BlockSpec(memory_space=pl.ANY)],
            out_specs=pl.BlockSpec(memory_space=pl.ANY),
            scratch_shapes=[pltpu.SemaphoreType.DMA(())]),
        out_shape=jax.ShapeDtypeStruct(xs0.shape, xs0.dtype),
        input_output_aliases={2: 0},
        compiler_params=_params(1),
        name="moe_dispatch",
    )(dest, h, xs0)


def _ffn_kernel(be_ref, nu_ref, xs_ref, wg_ref, wu_ref, wd_ref, ys_ref, wg_s, wu_s, wd_s):
    i = pl.program_id(0)
    new_expert = jnp.logical_or(i == 0, be_ref[i] != be_ref[jnp.maximum(i - 1, 0)])

    @pl.when(new_expert)
    def _():
        wg_s[...] = wg_ref[0, 0].astype(MXU_DTYPE)
        wu_s[...] = wu_ref[0, 0].astype(MXU_DTYPE)
        wd_s[...] = wd_ref[0, 0].astype(MXU_DTYPE)

    @pl.when(i < nu_ref[0])
    def _():
        xb = xs_ref[...].astype(MXU_DTYPE)
        act = jax.nn.silu(_dot(xb, wg_s[...])) * _dot(xb, wu_s[...])
        ys_ref[...] = _dot(act.astype(MXU_DTYPE), wd_s[...])

    @pl.when(i >= nu_ref[0])
    def _():
        ys_ref[...] = jnp.zeros_like(ys_ref)


def _ffn_call(block_e, n_used, xs, wg, wu, wd, layer, n_blocks):
    d = xs.shape[1]
    br = EXPERT_ROWS
    rows = lambda i, be, nu: (jnp.minimum(i, nu[0] - 1), 0)
    return pl.pallas_call(
        _ffn_kernel,
        grid_spec=pltpu.PrefetchScalarGridSpec(
            num_scalar_prefetch=2,
            grid=(n_blocks,),
            in_specs=[pl.BlockSpec((br, d), rows),
                      pl.BlockSpec((1, 1, d, D_EXPERT), lambda i, be, nu: (layer, be[i], 0, 0)),
                      pl.BlockSpec((1, 1, d, D_EXPERT), lambda i, be, nu: (layer, be[i], 0, 0)),
                      pl.BlockSpec((1, 1, D_EXPERT, d), lambda i, be, nu: (layer, be[i], 0, 0))],
            out_specs=pl.BlockSpec((br, d), lambda i, be, nu: (i, 0)),
            scratch_shapes=[pltpu.VMEM((d, D_EXPERT), MXU_DTYPE),
                            pltpu.VMEM((d, D_EXPERT), MXU_DTYPE),
                            pltpu.VMEM((D_EXPERT, d), MXU_DTYPE)]),
        out_shape=jax.ShapeDtypeStruct((n_blocks * br, d), F32),
        compiler_params=_params(1),
        name="moe_experts",
    )(block_e, n_used, xs, wg, wu, wd)


def _combine_kernel(src_ref, h_ref, rc_ref, lg_ref, lb_ref, ys_ref, o_ref, buf, sem, *, tm, m, alpha):
    i = pl.program_id(0)

    def issue(tile, slot):
        base = tile * tm

        def body(r, c):
            for k in range(2):
                s = src_ref[k * m + base + r]
                pltpu.make_async_copy(ys_ref.at[pl.ds(s, 1)], buf.at[slot, k, pl.ds(r, 1)], sem.at[slot]).start()
            return c

        lax.fori_loop(0, tm, body, 0, unroll=DMA_ISSUE_UNROLL)

    @pl.when(i == 0)
    def _():
        issue(0, 0)

    @pl.when(i + 1 < m // tm)
    def _():
        issue(i + 1, (i + 1) % 2)

    slot = i % 2
    for k in range(2):
        pltpu.make_async_copy(ys_ref.at[pl.ds(0, tm)], buf.at[slot, k], sem.at[slot]).wait()
    rc = rc_ref[...]
    real = rc[:, 0:1] < float(N_EXPERTS)
    y = jnp.where(real, rc[:, 2:3] * buf[slot, 0] + rc[:, 3:4] * buf[slot, 1], 0.0)
    o_ref[...] = _layer_norm(alpha * h_ref[...] + y, lg_ref[...], lb_ref[...])


def _combine_call(src, h, rc, lg, lb, ys, alpha, tm):
    m, d = h.shape
    n_tiles = m // tm
    return pl.pallas_call(
        functools.partial(_combine_kernel, tm=tm, m=m, alpha=alpha),
        grid_spec=pltpu.PrefetchScalarGridSpec(
            num_scalar_prefetch=1,
            grid=(n_tiles,),
            in_specs=[pl.BlockSpec((tm, d), lambda i, s: (i, 0)),
                      pl.BlockSpec((tm, LANES), lambda i, s: (i, 0)),
                      pl.BlockSpec((1, d), lambda i, s: (0, 0)),
                      pl.BlockSpec((1, d), lambda i, s: (0, 0)),
                      pl.BlockSpec(memory_space=pl.ANY)],
            out_specs=pl.BlockSpec((tm, d), lambda i, s: (i, 0)),
            scratch_shapes=[pltpu.VMEM((2, 2, tm, d), F32),
                            pltpu.SemaphoreType.DMA((2,))]),
        out_shape=jax.ShapeDtypeStruct((m, d), F32),
        compiler_params=_params(1),
        name="moe_combine_ln2",
    )(src, h, rc, lg, lb, ys)


def _moe(h1, w_rg, b_rg, w_re, b_re, w_gate, w_up, w_down, layer, lg, lb, alpha, bsz, lp, l_end, tm):
    m, d = h1.shape
    wr = jnp.zeros((LANES, d), F32).at[0:N_EXPERTS].set(w_re.T).at[N_EXPERTS:N_EXPERTS + N_GROUPS].set(w_rg.T)
    br_ = jnp.zeros((LANES,), F32).at[0:N_EXPERTS].set(b_re).at[N_EXPERTS:N_EXPERTS + N_GROUPS].set(b_rg)
    br_ = br_.at[N_EXPERTS + N_GROUPS:N_EXPERTS + 8].set(NEG).reshape(LANES, 1)
    rt, rc = _router_call(h1, wr, br_, tm, lp, l_end)
    rk, cnt = _rank_call(rt, tm)

    rows = EXPERT_ROWS
    n_assign = 2 * bsz * (l_end - ROW_PAD)
    n_unused = lp - (l_end - ROW_PAD)
    n_blocks = -(-(n_assign + N_EXPERTS * (rows - 1)) // rows)
    n_slots = n_blocks * rows
    counts = cnt[:, 0].astype(jnp.int32)
    pcounts = (counts + rows - 1) // rows * rows
    pend = jnp.cumsum(pcounts)
    pstart = pend - pcounts
    e = rt[0:2].astype(jnp.int32)
    rank = rk[0:2].astype(jnp.int32)
    real = e < N_EXPERTS
    expert_ids = jnp.arange(N_EXPERTS, dtype=jnp.int32)[:, None, None]
    slot = jnp.sum(jnp.where(e[None] == expert_ids, pstart[:, None, None], 0), axis=0) + rank
    tok = jnp.arange(m, dtype=jnp.int32)
    pos = tok % lp
    unused_idx = (tok // lp) * n_unused + jnp.where(pos < ROW_PAD, pos, pos - l_end + ROW_PAD)
    spare = n_slots + 2 * unused_idx[None, :] + jnp.arange(2, dtype=jnp.int32)[:, None]
    dest = jnp.where(real, slot, spare).reshape(-1)
    src = jnp.where(real, slot, 0).reshape(-1)
    block_start = jnp.arange(n_blocks, dtype=jnp.int32) * rows
    block_e = jnp.minimum(jnp.sum((pend[None, :] <= block_start[:, None]).astype(jnp.int32), axis=1),
                          N_EXPERTS - 1)
    n_used = (pend[-1:] // rows).astype(jnp.int32)

    n_spare = -(-(2 * bsz * n_unused) // rows) * rows
    xs = _dispatch_call(dest, h1, jnp.zeros((n_slots + n_spare, d), F32), tm)
    ys = _ffn_call(block_e, n_used, xs, w_gate, w_up, w_down, layer, n_blocks)
    return _combine_call(src, h1, rc, lg, lb, ys, alpha, _token_tile(m, 384))


def _rope_tables(lp):
    pos = jnp.arange(lp, dtype=F32) - float(ROW_PAD)
    inv = 1.0 / (ROPE_THETA ** (jnp.arange(0, DA_DIM, 2, dtype=F32) / DA_DIM))
    ang = pos[:, None] * inv[None, :]
    reps = LANES // (DA_DIM // 2)
    sign = jnp.tile(jnp.concatenate([-jnp.ones((DA_DIM // 2,), F32), jnp.ones((DA_DIM // 2,), F32)]), LANES // DA_DIM)
    return jnp.tile(jnp.cos(ang), (1, reps)), jnp.tile(jnp.sin(ang), (1, reps)) * sign[None, :]


def _dup_heads(w, n_heads, dim):
    d = w.shape[0]
    return jnp.broadcast_to(w.reshape(d, n_heads, 1, dim), (d, n_heads, 2, dim)).reshape(d, n_heads * 2 * dim)


def kernel(x, meta, ln_in_g, ln_in_b, w_in, conv_w, conv_b, gate_b, lam_q1, lam_k1, lam_q2, lam_k2, diff_g, sink, mlstm_g, w_branch, w_out, ln1_g, ln1_b, ln2_g, ln2_b, w_rg, b_rg, w_re, b_re, w_gate, w_up, w_down):
    bsz, seq, d = x.shape
    depth = w_in.shape[0]
    assert seq % BLOCK == 0 and d == D_MODEL
    l_end = seq + BLOCK
    lp = -(-l_end // MXU_TILE) * MXU_TILE
    m = bsz * lp
    alpha = (2.0 * depth) ** 0.25
    tm = _row_tile(lp, 768)
    tk_attn = _row_tile(lp, 768)

    hp = jnp.concatenate([jnp.zeros((bsz, ROW_PAD, d), x.dtype),
                          jnp.broadcast_to(meta.astype(x.dtype)[None], (bsz, N_META_TOK, d)), x,
                          jnp.zeros((bsz, lp - l_end, d), x.dtype)], axis=1)
    h = _ln_call(hp.reshape(m, d), ln_in_g, ln_in_b, tm)
    cos, sin = _rope_tables(lp)
    q_scale = DA_DIM ** -0.5
    rope_scale = jnp.concatenate([jnp.full((A_Q,), q_scale * math.log2(math.e), F32), jnp.ones((A_K,), F32),
                                  jnp.full((B_Q,), q_scale, F32), jnp.ones((2 * B_K,), F32)]).reshape(1, -1)
    conv_scale = jnp.concatenate([jnp.ones((C_Q,), F32), jnp.full((C_K,), MC_QK ** -0.5, F32)]).reshape(1, -1)

    for l in range(depth):
        lam_init = 0.8 - 0.6 * math.exp(-0.3 * l)
        wl = w_in[l]
        col = lambda i: wl[:, OFFS[i]:OFFS[i + 1]]
        w_rope = jnp.concatenate([col(0), col(1), col(3), _dup_heads(col(4), WB_KV, WB_DIM),
                                  _dup_heads(col(5), WB_KV, WB_DIM)], axis=1).astype(MXU_DTYPE)
        w_vt = jnp.concatenate([col(8), col(2)], axis=1).T.astype(MXU_DTYPE)
        w_conv = jnp.concatenate([col(6), col(7), col(9)], axis=1).astype(MXU_DTYPE)
        w_g = jnp.pad(col(10), ((0, 0), (0, LANES - C_G))).astype(MXU_DTYPE)
        b_g = jnp.pad(gate_b[l], (0, LANES - C_G)).reshape(1, LANES)
        w_mg = col(11).astype(MXU_DTYPE)

        rq = _proj_rope_call(h, w_rope, cos, sin, rope_scale, tm, lp).reshape(bsz, lp, -1)
        vt = _proj_t_call(h, w_vt, tm, "proj_val_t")
        zco = _proj_call(h, w_conv, tm, F32, C_Q + C_K, "proj_conv_gate")
        gc, gr = _gates_call(h, w_g, b_g, tm, lp, l_end)

        lamv = jnp.stack([lam_q1[l], lam_k1[l], lam_q2[l], lam_k2[l]])
        out_a = _attn_a_call(rq, vt, lamv, diff_g[l].reshape(-1, 1), lam_init, l_end, MXU_TILE, tk_attn)
        out_b = _attn_b_call(rq, sink[l].reshape(1, -1), l_end)
        qk = _conv_call(zco.reshape(bsz, lp, -1), conv_w[l], conv_b[l].reshape(1, -1), conv_scale, l_end,
                        _row_tile(lp, 768))
        h_f, h_b = _mlstm_call(qk, vt, gc, gr)

        h = _merge_call(h, out_a.reshape(m, -1), out_b.reshape(m, -1), h_f.reshape(m, -1), h_b.reshape(m, -1),
                        zco, mlstm_g[l].reshape(1, -1), w_mg, w_branch[l].astype(MXU_DTYPE),
                        w_out[l].astype(MXU_DTYPE), ln1_g[l].reshape(1, -1), ln1_b[l].reshape(1, -1),
                        alpha, _token_tile(m, 384))
        h = _moe(h, w_rg[l], b_rg[l], w_re[l], b_re[l], w_gate, w_up, w_down, l,
                 ln2_g[l].reshape(1, -1), ln2_b[l].reshape(1, -1), alpha, bsz, lp, l_end, tm)
    return h.reshape(bsz, lp, d)[:, BLOCK:l_end]
```

```python
import functools
import math

import numpy as np
import jax
import jax.numpy as jnp
from jax import lax
from jax.experimental import pallas as pl
from jax.experimental.pallas import tpu as pltpu

D_MODEL = 1024
N_META_TOK = 16
BLOCK = 128
ROW_PAD = BLOCK - N_META_TOK
ROPE_THETA = 10000.0
LN_EPS = 1e-5
NEG = -1e30

DA_HEADS = 4
DA_DIM = 64
WB_HEADS = 8
WB_KV = 2
WB_DIM = 64
WINDOW = 128
MC_HEADS = 4
MC_QK = 128
MC_V = 128
N_BRANCH = 3
BRANCH_W = 512
N_GROUPS = 4
EXP_PER_GROUP = 8
N_EXPERTS = N_GROUPS * EXP_PER_GROUP
D_EXPERT = 512

A_Q = DA_HEADS * 2 * DA_DIM
A_K = A_Q
A_V = A_Q
B_Q = WB_HEADS * WB_DIM
B_K = WB_KV * WB_DIM
B_V = B_K
C_Q = MC_HEADS * MC_QK
C_K = C_Q
C_V = MC_HEADS * MC_V
C_O = C_V
C_G = 4 * MC_HEADS
GATE_W = N_BRANCH * D_MODEL
SPLITS = (A_Q, A_K, A_V, B_Q, B_K, B_V, C_Q, C_K, C_V, C_O, C_G, GATE_W)
OFFS = tuple(int(v) for v in np.cumsum((0,) + SPLITS))

LANES = 128
MXU_TILE = 256
EXPERT_ROWS = 512
WINDOW_BLOCKS_PER_STEP = 6
MLSTM_CHUNKS_PER_STEP = 6
MLSTM_EXT = 16
DMA_ISSUE_UNROLL = 8
ATTN_Q_TILES = 3
ATTN_UNROLL = 2
ATTN_EXT = 16
VMEM_LIMIT = 56 * 1024 * 1024

F32 = jnp.float32
MXU_DTYPE = jnp.bfloat16


def _dot(a, b):
    return jnp.dot(a, b, preferred_element_type=F32)


def _dot_nt(a, b):
    return lax.dot_general(a, b, (((1,), (1,)), ((), ())), preferred_element_type=F32)


def _dot_tn(a, b):
    return lax.dot_general(a, b, (((0,), (0,)), ((), ())), preferred_element_type=F32)


def _params(n_axes, flags=None):
    return pltpu.CompilerParams(dimension_semantics=("arbitrary",) * n_axes,
                                vmem_limit_bytes=VMEM_LIMIT, flags=flags)


def _row_tile(n_rows, target):
    best = BLOCK
    for t in range(BLOCK, target + 1, BLOCK):
        if n_rows % t == 0:
            best = t
    return best


def _token_tile(n_rows, target):
    best = 8
    for t in range(8, target + 1, 8):
        if n_rows % t == 0:
            best = t
    return best


def _layer_norm(x, g, b):
    mu = jnp.mean(x, axis=-1, keepdims=True)
    xc = x - mu
    var = jnp.mean(xc * xc, axis=-1, keepdims=True)
    return xc * lax.rsqrt(var + LN_EPS) * g + b


def _ln_kernel(x_ref, g_ref, b_ref, o_ref):
    o_ref[...] = _layer_norm(x_ref[...], g_ref[...], b_ref[...])


def _ln_call(x, g, b, tm):
    m, d = x.shape
    return pl.pallas_call(
        _ln_kernel,
        grid=(m // tm,),
        in_specs=[pl.BlockSpec((tm, d), lambda i: (i, 0)),
                  pl.BlockSpec((1, d), lambda i: (0, 0)),
                  pl.BlockSpec((1, d), lambda i: (0, 0))],
        out_specs=pl.BlockSpec((tm, d), lambda i: (i, 0)),
        out_shape=jax.ShapeDtypeStruct((m, d), F32),
        compiler_params=_params(1),
        name="ln_in",
    )(x, g.reshape(1, d), b.reshape(1, d))


def _proj_rope_kernel(x_ref, w_ref, cos_ref, sin_ref, scale_ref, o_ref, *, n_rope):
    z = _dot(x_ref[...].astype(MXU_DTYPE), w_ref[...])
    cos = cos_ref[...]
    sin = sin_ref[...]
    lane = lax.broadcasted_iota(jnp.int32, cos.shape, 1)
    first_half = (lane % DA_DIM) < (DA_DIM // 2)
    for c in range(n_rope // LANES):
        sl = slice(c * LANES, (c + 1) * LANES)
        zc = z[:, sl]
        partner = jnp.where(first_half, pltpu.roll(zc, LANES - DA_DIM // 2, 1),
                            pltpu.roll(zc, DA_DIM // 2, 1))
        o_ref[:, sl] = ((zc * cos + partner * sin) * scale_ref[:, sl]).astype(o_ref.dtype)
    o_ref[:, n_rope:] = z[:, n_rope:].astype(o_ref.dtype)


def _proj_rope_call(h, w, cos, sin, scale, tm, lp):
    m, d = h.shape
    n = w.shape[1]
    per_batch = lp // tm
    return pl.pallas_call(
        functools.partial(_proj_rope_kernel, n_rope=scale.shape[1]),
        grid=(m // tm,),
        in_specs=[pl.BlockSpec((tm, d), lambda i: (i, 0)),
                  pl.BlockSpec((d, n), lambda i: (0, 0)),
                  pl.BlockSpec((tm, LANES), lambda i: (i % per_batch, 0)),
                  pl.BlockSpec((tm, LANES), lambda i: (i % per_batch, 0)),
                  pl.BlockSpec(scale.shape, lambda i: (0, 0))],
        out_specs=pl.BlockSpec((tm, n), lambda i: (i, 0)),
        out_shape=jax.ShapeDtypeStruct((m, n), MXU_DTYPE),
        compiler_params=_params(1),
        name="proj_rope",
    )(h, w, cos, sin, scale)


def _proj_kernel(x_ref, w_ref, o_ref, *, sigmoid_from):
    z = _dot(x_ref[...].astype(MXU_DTYPE), w_ref[...])
    if sigmoid_from is None:
        o_ref[...] = z.astype(o_ref.dtype)
    else:
        o_ref[:, :sigmoid_from] = z[:, :sigmoid_from].astype(o_ref.dtype)
        o_ref[:, sigmoid_from:] = jax.nn.sigmoid(z[:, sigmoid_from:]).astype(o_ref.dtype)


def _proj_call(h, w, tm, out_dtype, sigmoid_from, name):
    m, d = h.shape
    n = w.shape[1]
    return pl.pallas_call(
        functools.partial(_proj_kernel, sigmoid_from=sigmoid_from),
        grid=(m // tm,),
        in_specs=[pl.BlockSpec((tm, d), lambda i: (i, 0)),
                  pl.BlockSpec((d, n), lambda i: (0, 0))],
        out_specs=pl.BlockSpec((tm, n), lambda i: (i, 0)),
        out_shape=jax.ShapeDtypeStruct((m, n), out_dtype),
        compiler_params=_params(1),
        name=name,
    )(h, w)


def _proj_t_kernel(x_ref, wt_ref, o_ref):
    o_ref[...] = _dot_nt(wt_ref[...], x_ref[...].astype(MXU_DTYPE)).astype(o_ref.dtype)


def _proj_t_call(h, wt, tm, name):
    m, d = h.shape
    n = wt.shape[0]
    return pl.pallas_call(
        _proj_t_kernel,
        grid=(m // tm,),
        in_specs=[pl.BlockSpec((tm, d), lambda i: (i, 0)),
                  pl.BlockSpec((n, d), lambda i: (0, 0))],
        out_specs=pl.BlockSpec((n, tm), lambda i: (0, i)),
        out_shape=jax.ShapeDtypeStruct((n, m), MXU_DTYPE),
        compiler_params=_params(1),
        name=name,
    )(h, wt)


def _gates_kernel(x_ref, w_ref, b_ref, gc_ref, gr_ref, *, tm, lp, l_end):
    z = _dot(x_ref[...].astype(MXU_DTYPE), w_ref[...]) + b_ref[...]
    lane = lax.broadcasted_iota(jnp.int32, (tm, LANES), 1)
    kind = lane // MC_HEADS
    row = lax.broadcasted_iota(jnp.int32, (tm, LANES), 0) + pl.program_id(0) * tm
    pos = row % lp
    unused = (pos < ROW_PAD) | (pos >= l_end)
    log_f = jnp.minimum(z, 0.0) - jnp.log1p(jnp.exp(-jnp.abs(z)))
    is_forget = (kind % 2) == 1
    base = jnp.where(is_forget, jnp.where(unused, 0.0, log_f), jnp.where(unused, NEG, z))
    r128 = lax.broadcasted_iota(jnp.int32, (BLOCK, LANES), 0)
    fwd_lane = lax.broadcasted_iota(jnp.int32, (BLOCK, LANES), 1) // MC_HEADS == 1
    forget128 = (lax.broadcasted_iota(jnp.int32, (BLOCK, LANES), 1) // MC_HEADS) % 2 == 1
    for c in range(tm // BLOCK):
        x = base[c * BLOCK:(c + 1) * BLOCK]
        pre = x
        suf = x
        s = 1
        while s < BLOCK:
            pre = pre + jnp.where(r128 >= s, pltpu.roll(pre, s, 0), 0.0)
            suf = suf + jnp.where(r128 < BLOCK - s, pltpu.roll(suf, BLOCK - s, 0), 0.0)
            s *= 2
        out = jnp.where(forget128, jnp.where(fwd_lane, pre, suf), x)
        gc_ref[c * BLOCK:(c + 1) * BLOCK, :] = out
        gr_ref[:, c * BLOCK:(c + 1) * BLOCK] = out.T[0:C_G, :]


def _gates_call(h, w, b, tm, lp, l_end):
    m, d = h.shape
    return pl.pallas_call(
        functools.partial(_gates_kernel, tm=tm, lp=lp, l_end=l_end),
        grid=(m // tm,),
        in_specs=[pl.BlockSpec((tm, d), lambda i: (i, 0)),
                  pl.BlockSpec((d, LANES), lambda i: (0, 0)),
                  pl.BlockSpec((1, LANES), lambda i: (0, 0))],
        out_specs=[pl.BlockSpec((tm, LANES), lambda i: (i, 0)),
                   pl.BlockSpec((C_G, tm), lambda i: (0, i))],
        out_shape=[jax.ShapeDtypeStruct((m, LANES), F32),
                   jax.ShapeDtypeStruct((C_G, m), F32)],
        compiler_params=_params(1),
        name="mlstm_gates",
    )(h, w, b)


def _conv_kernel(z_ref, w_ref, b_ref, scale_ref, o_ref, *, lp, l_end, tr):
    w0 = w_ref[0:1, :]
    w1 = w_ref[1:2, :]
    w2 = w_ref[2:3, :]
    row = lax.broadcasted_iota(jnp.int32, (tr, LANES), 0)
    for c in range(lp // tr):
        r0 = c * tr
        zc = z_ref[0, r0:r0 + tr, :]
        before = jnp.zeros((1, LANES), F32) if r0 == 0 else z_ref[0, r0 - 1:r0, :]
        after = jnp.zeros((1, LANES), F32) if r0 + tr == lp else z_ref[0, r0 + tr:r0 + tr + 1, :]
        prev = jnp.where(row == 0, before, pltpu.roll(zc, 1, 0))
        nxt = jnp.where(row == tr - 1, after, pltpu.roll(zc, tr - 1, 0))
        if r0 <= ROW_PAD < r0 + tr:
            prev = jnp.where(row == ROW_PAD - r0, 0.0, prev)
        if r0 <= l_end - 1 < r0 + tr:
            nxt = jnp.where(row == l_end - 1 - r0, 0.0, nxt)
        y = prev * w0 + zc * w1 + nxt * w2 + b_ref[...]
        o_ref[0, r0:r0 + tr, :] = (jax.nn.silu(y) * scale_ref[...]).astype(o_ref.dtype)


def _conv_call(z, w, b, scale, l_end, tr):
    bsz, lp, _ = z.shape
    n = w.shape[1]
    return pl.pallas_call(
        functools.partial(_conv_kernel, lp=lp, l_end=l_end, tr=tr),
        grid=(bsz, n // LANES),
        in_specs=[pl.BlockSpec((1, lp, LANES), lambda b_, j: (b_, 0, j)),
                  pl.BlockSpec((3, LANES), lambda b_, j: (0, j)),
                  pl.BlockSpec((1, LANES), lambda b_, j: (0, j)),
                  pl.BlockSpec((1, LANES), lambda b_, j: (0, j))],
        out_specs=pl.BlockSpec((1, lp, LANES), lambda b_, j: (b_, 0, j)),
        out_shape=jax.ShapeDtypeStruct((bsz, lp, n), MXU_DTYPE),
        compiler_params=_params(2),
        name="mlstm_conv",
    )(z, w, b, scale)


def _attn_a_kernel(lamv_ref, g_ref, q_ref, k_ref, vt_ref, o_ref, s0_scr, s1_scr, acc_scr, *,
                   tk, n_chunks, l_end, lam_init):
    n_ch = acc_scr.shape[0]
    tq = acc_scr.shape[2]
    feat = lax.broadcasted_iota(jnp.int32, (2 * DA_DIM, tq), 0)
    key_row = lax.broadcasted_iota(jnp.int32, (tk, tq), 0)
    qz = []
    for t in range(n_ch // 2):
        qt = q_ref[0, t * tq:(t + 1) * tq, :].astype(F32).T.astype(MXU_DTYPE)
        zero = jnp.zeros_like(qt)
        qz += [jnp.where(feat < DA_DIM, qt, zero), jnp.where(feat >= DA_DIM, qt, zero)]
    s_bufs = (s0_scr, s1_scr)

    last = n_chunks - 1
    last_hi = l_end - last * tk

    def scores(j, slot):
        start = pl.multiple_of(j * tk, tk)
        kj = k_ref[0, pl.ds(start, tk), :]
        cmax = []
        for c in range(n_ch):
            s = _dot(kj, qz[c])
            if isinstance(j, int) and j == 0:
                s = jnp.where(key_row >= ROW_PAD, s, NEG)
            if isinstance(j, int) and j == last and last_hi < tk:
                s = jnp.where(key_row < last_hi, s, NEG)
            s_bufs[slot][c] = s
            cmax.append(jnp.max(s, axis=0, keepdims=True))
        return tuple(cmax)

    ones_rows = jnp.ones((ATTN_EXT, tk), MXU_DTYPE)

    def softmax_values(j, slot, stats, cmax):
        start = pl.multiple_of(j * tk, tk)
        vt = jnp.concatenate([vt_ref[:, pl.ds(start, tk)], ones_rows], axis=0)
        new_stats = []
        for c in range(n_ch):
            m_new = jnp.maximum(stats[c], cmax[c])
            alpha = jnp.exp2(stats[c] - m_new)
            p = jnp.exp2((s_bufs[slot][c] - m_new).astype(MXU_DTYPE))
            acc_scr[c] = alpha * acc_scr[c] + _dot(vt, p)
            new_stats.append(m_new)
        return tuple(new_stats)

    one = jnp.full((1, tq), NEG, F32)
    acc_scr[...] = jnp.zeros_like(acc_scr)
    cmax = scores(0, 0)
    stats = (one,) * n_ch
    if n_chunks > 1:
        def step(j, parity, state):
            stats, cmax = state
            nxt = scores(j + 1, 1 - parity)
            return softmax_values(j, parity, stats, cmax), nxt

        def trip(i, st):
            for u in range(ATTN_UNROLL):
                st = step(1 + ATTN_UNROLL * i + u, (1 + u) % 2, st)
            return st

        state = step(0, 0, (stats, cmax))
        n_trips = (last - 2) // ATTN_UNROLL if last >= 2 else 0
        state = lax.fori_loop(0, n_trips, trip, state)
        for j in range(1 + n_trips * ATTN_UNROLL, last):
            state = step(j, j % 2, state)
        stats, cmax = state
    softmax_values(last, last % 2, stats, cmax)
    dv = 2 * DA_DIM
    lv = lamv_ref[...]
    lam = (jnp.exp(jnp.sum(lv[0:1] * lv[1:2], axis=-1, keepdims=True))
           - jnp.exp(jnp.sum(lv[2:3] * lv[3:4], axis=-1, keepdims=True)) + lam_init)
    for t in range(n_ch // 2):
        o0 = acc_scr[2 * t, 0:dv, :] / acc_scr[2 * t, dv:dv + 1, :]
        o1 = acc_scr[2 * t + 1, 0:dv, :] / acc_scr[2 * t + 1, dv:dv + 1, :]
        o = o0 - lam * o1
        ms = jnp.mean(o * o, axis=0, keepdims=True)
        o = o * lax.rsqrt(ms + LN_EPS) * g_ref[...] * (1.0 - lam_init)
        o_ref[0, t * tq:(t + 1) * tq, :] = o.T.astype(o_ref.dtype)


def _attn_a_call(rq, vt, lamv, g_col, lam_init, l_end, tq, tk):
    bsz, lp, _ = rq.shape
    k_blk = A_Q // LANES
    vt_blk = C_V // LANES
    n_q = ATTN_Q_TILES if (lp // tq) % ATTN_Q_TILES == 0 else 1
    n_ch = 2 * n_q
    return pl.pallas_call(
        functools.partial(_attn_a_kernel, tk=tk, n_chunks=lp // tk, l_end=l_end, lam_init=lam_init),
        grid=(bsz, DA_HEADS, lp // (n_q * tq)),
        in_specs=[pl.BlockSpec((4, DA_DIM), lambda b, h, i: (0, 0)),
                  pl.BlockSpec((2 * DA_DIM, 1), lambda b, h, i: (0, 0)),
                  pl.BlockSpec((1, n_q * tq, LANES), lambda b, h, i: (b, i, h)),
                  pl.BlockSpec((1, lp, LANES), lambda b, h, i: (b, 0, k_blk + h)),
                  pl.BlockSpec((2 * DA_DIM, lp), lambda b, h, i: (vt_blk + h, b))],
        out_specs=pl.BlockSpec((1, n_q * tq, LANES), lambda b, h, i: (b, i, h)),
        out_shape=jax.ShapeDtypeStruct((bsz, lp, A_V), MXU_DTYPE),
        scratch_shapes=[pltpu.VMEM((n_ch, tk, tq), F32),
                        pltpu.VMEM((n_ch, tk, tq), F32),
                        pltpu.VMEM((n_ch, 2 * DA_DIM + ATTN_EXT, tq), F32)],
        compiler_params=_params(3),
        name="diff_attn",
    )(lamv, g_col, rq, rq, vt)


def _attn_b_kernel(sink_ref, band_ref, q_ref, k0_ref, k1_ref, v0_ref, v1_ref, o_ref, *, lp, l_end):
    n_sub = q_ref.shape[1] // BLOCK
    for sub in range(n_sub):
        _attn_b_block(sink_ref, band_ref, q_ref, k0_ref, k1_ref, v0_ref, v1_ref, o_ref,
                      pl.program_id(1) * n_sub + sub, slice(sub * BLOCK, (sub + 1) * BLOCK), lp, l_end)


def _attn_b_block(sink_ref, band_ref, q_ref, k0_ref, k1_ref, v0_ref, v1_ref, o_ref, n, rows, lp, l_end):
    nb = lp // BLOCK
    grp = WB_HEADS // WB_KV

    def blocks(ref):
        parts = [ref[0, 0:BLOCK, :]]
        for d in (-1, 0, 1):
            idx = jnp.clip(n + d, 0, nb - 1)
            parts.append(ref[0, pl.ds(pl.multiple_of(idx * BLOCK, BLOCK), BLOCK), :])
        return jnp.concatenate(parts, axis=0)

    keys = (blocks(k0_ref), blocks(k1_ref))
    vals = (blocks(v0_ref), blocks(v1_ref))
    blk_bias = []
    for d in (-1, 0, 1):
        inside = jnp.logical_and(n + d >= 1, n + d <= l_end // BLOCK - 1)
        blk_bias.append(jnp.where(inside, 0.0, NEG))
    head_of_col = lax.broadcasted_iota(jnp.int32, (1, grp * BLOCK), 1) // BLOCK
    lane = lax.broadcasted_iota(jnp.int32, (BLOCK, LANES), 1)
    sinks, raw = [], []
    for g in range(WB_KV):
        qs = []
        sink = jnp.zeros((1, grp * BLOCK), F32)
        for j in range(grp):
            h = g * grp + j
            qt = q_ref[0, rows, (h // 2) * LANES:(h // 2 + 1) * LANES]
            keep = (lane >= WB_DIM) if h % 2 else (lane < WB_DIM)
            qs.append(jnp.where(keep, qt, jnp.zeros_like(qt)))
            sink = jnp.where(head_of_col == j, sink_ref[:, h:h + 1], sink)
        sinks.append(sink)
        raw.append(_dot_nt(keys[g], jnp.concatenate(qs, axis=0)))
    weights, dens = [], []
    for g in range(WB_KV):
        s = raw[g] + band_ref[...]
        s = jnp.concatenate([s[0:BLOCK]] + [s[(i + 1) * BLOCK:(i + 2) * BLOCK] + blk_bias[i] for i in range(3)],
                            axis=0)
        m = jnp.maximum(jnp.max(s, axis=0, keepdims=True), sinks[g])
        p = jnp.exp(s - m)
        dens.append(jnp.sum(p, axis=0, keepdims=True) + jnp.exp(sinks[g] - m))
        weights.append(p.astype(MXU_DTYPE))
    outs = [_dot_tn(vals[g], weights[g]) for g in range(WB_KV)]
    for g in range(WB_KV):
        o = (outs[g] / dens[g]).T
        for jj in range(grp // 2):
            lo = o[(2 * jj) * BLOCK:(2 * jj + 1) * BLOCK]
            hi = o[(2 * jj + 1) * BLOCK:(2 * jj + 2) * BLOCK]
            t = (g * grp) // 2 + jj
            o_ref[0, rows, t * LANES:(t + 1) * LANES] = jnp.where(lane < WB_DIM, lo, hi).astype(o_ref.dtype)


def _attn_b_call(rq, sink, l_end):
    bsz, lp, _ = rq.shape
    q_blk = (A_Q + A_K) // B_Q
    k_blk = (A_Q + A_K + B_Q) // LANES
    v_blk = k_blk + 2 * B_K // LANES
    n_sub = WINDOW_BLOCKS_PER_STEP if (lp // BLOCK) % WINDOW_BLOCKS_PER_STEP == 0 else 1
    q_rows = n_sub * BLOCK
    grp = WB_HEADS // WB_KV
    krow = np.arange(4 * BLOCK)[:, None]
    qoff = np.arange(grp * BLOCK)[None, :] % BLOCK
    ok = np.where(krow < BLOCK, krow >= ROW_PAD, np.abs(qoff + 2 * BLOCK - krow) <= WINDOW)
    band = jnp.asarray(np.where(ok, 0.0, NEG), F32)
    seq = lambda c: pl.BlockSpec((1, lp, LANES), lambda b, n: (b, 0, c))
    return pl.pallas_call(
        functools.partial(_attn_b_kernel, lp=lp, l_end=l_end),
        grid=(bsz, lp // q_rows),
        in_specs=[pl.BlockSpec((1, WB_HEADS), lambda b, n: (0, 0)),
                  pl.BlockSpec(band.shape, lambda b, n: (0, 0)),
                  pl.BlockSpec((1, q_rows, B_Q), lambda b, n: (b, n, q_blk)),
                  seq(k_blk), seq(k_blk + 1), seq(v_blk), seq(v_blk + 1)],
        out_specs=pl.BlockSpec((1, q_rows, B_Q), lambda b, n: (b, n, 0)),
        out_shape=jax.ShapeDtypeStruct((bsz, lp, B_Q), MXU_DTYPE),
        compiler_params=_params(2),
        name="window_attn",
    )(sink, band, rq, rq, rq, rq, rq)


def _mlstm_kernel(qkf_ref, vf_ref, gcf_ref, grf_ref, qkb_ref, vb_ref, gcb_ref, grb_ref,
                  hf_ref, hb_ref, c_scr, m_scr):
    t = pl.program_id(1)

    @pl.when(t == 0)
    def _():
        c_scr[...] = jnp.zeros_like(c_scr)
        m_scr[...] = jnp.zeros_like(m_scr)

    srow = lax.broadcasted_iota(jnp.int32, (BLOCK, BLOCK), 0)
    ccol = lax.broadcasted_iota(jnp.int32, (BLOCK, BLOCK), 1)
    ext_row = lax.broadcasted_iota(jnp.int32, (MLSTM_EXT, BLOCK), 0)
    ones_rows = jnp.where(ext_row == 0, 1.0, 0.0).astype(MXU_DTYPE)
    n_sub = qkf_ref.shape[1] // BLOCK
    for sub in range(n_sub):
        rows_f = slice(sub * BLOCK, (sub + 1) * BLOCK)
        rows_b = slice((n_sub - 1 - sub) * BLOCK, (n_sub - sub) * BLOCK)
        dirs = ((qkf_ref, vf_ref, gcf_ref, grf_ref, hf_ref, rows_f, srow <= ccol, BLOCK - 1),
                (qkb_ref, vb_ref, gcb_ref, grb_ref, hb_ref, rows_b, srow >= ccol, 0))
        _mlstm_chunk(dirs, ones_rows, c_scr, m_scr)


def _mlstm_chunk(dirs, ones_rows, c_scr, m_scr):
    chains = []
    for d, (qk_ref, vt_ref, gc_ref, gr_ref, h_ref, rows, tri, last) in enumerate(dirs):
        for hd in range(MC_HEADS):
            ci = d * MC_HEADS + hd
            j_li = (2 * d) * MC_HEADS + hd
            j_b = (2 * d + 1) * MC_HEADS + hd
            vt = vt_ref[hd * MC_V:(hd + 1) * MC_V, rows]
            b_row = gr_ref[j_b:j_b + 1, rows]
            chains.append(dict(
                ci=ci, hd=hd, h_ref=h_ref, rows=rows, tri=tri,
                q=qk_ref[0, rows, hd * MC_QK:(hd + 1) * MC_QK],
                k=qk_ref[0, rows, C_Q + hd * MC_QK:C_Q + (hd + 1) * MC_QK],
                vext=jnp.concatenate([vt, ones_rows], axis=0),
                key_col=gc_ref[rows, j_li:j_li + 1] - gc_ref[rows, j_b:j_b + 1],
                li_row=gr_ref[j_li:j_li + 1, rows], b_row=b_row,
                g=b_row[:, last:last + 1],
                m_prev=m_scr[ci, 0:1, 0:1],
                c_prev=c_scr[ci]))
    for ch in chains:
        ch["kq"] = _dot_nt(ch["k"], ch["q"])
        ch["cq"] = _dot_nt(ch["c_prev"].astype(MXU_DTYPE), ch["q"])
    for ch in chains:
        dmat = jnp.where(ch["tri"], ch["b_row"] + ch["key_col"], NEG)
        m_t = jnp.maximum(ch["b_row"] + ch["m_prev"], jnp.max(dmat, axis=0, keepdims=True))
        ch["m_t"] = m_t
        ch["inter"] = jnp.exp(ch["b_row"] + ch["m_prev"] - m_t)
        ch["s"] = (ch["kq"] * jnp.exp(dmat - m_t)).astype(MXU_DTYPE)
        a_row = ch["g"] - ch["b_row"] + ch["li_row"]
        m_new = jnp.maximum(ch["g"] + ch["m_prev"], jnp.max(a_row, axis=1, keepdims=True))
        ch["m_new"] = m_new
        ch["decay"] = jnp.exp(ch["g"] + ch["m_prev"] - m_new)
        ch["vw"] = (ch["vext"].astype(F32) * jnp.exp(a_row - m_new)).astype(MXU_DTYPE)
    for ch in chains:
        ch["vs"] = _dot(ch["vext"], ch["s"])
        ch["dc"] = _dot(ch["vw"], ch["k"])
    for ch in chains:
        nd = ch["inter"] * ch["cq"] + ch["vs"]
        den = nd[MC_V:MC_V + 1, :]
        h_t = nd[0:MC_V, :] / jnp.maximum(jnp.abs(den), jnp.exp(-ch["m_t"]))
        ch["h_ref"][0, ch["rows"], ch["hd"] * MC_V:(ch["hd"] + 1) * MC_V] = h_t.T
        c_scr[ch["ci"]] = ch["decay"] * ch["c_prev"] + ch["dc"]
        m_scr[ch["ci"]] = jnp.broadcast_to(ch["m_new"], m_scr.shape[1:])


def _mlstm_call(qk, vt, gc, gr):
    bsz, lp, _ = qk.shape
    n_sub = MLSTM_CHUNKS_PER_STEP if (lp // BLOCK) % MLSTM_CHUNKS_PER_STEP == 0 else 1
    rows = n_sub * BLOCK
    nch = lp // rows
    fwd = lambda b, t: (b, t, 0)
    bwd = lambda b, t: (b, nch - 1 - t, 0)
    return pl.pallas_call(
        _mlstm_kernel,
        grid=(bsz, nch),
        in_specs=[pl.BlockSpec((1, rows, C_Q + C_K), fwd),
                  pl.BlockSpec((C_V, rows), lambda b, t: (0, b * nch + t)),
                  pl.BlockSpec((rows, LANES), lambda b, t: (b * nch + t, 0)),
                  pl.BlockSpec((C_G, rows), lambda b, t: (0, b * nch + t)),
                  pl.BlockSpec((1, rows, C_Q + C_K), bwd),
                  pl.BlockSpec((C_V, rows), lambda b, t: (0, b * nch + nch - 1 - t)),
                  pl.BlockSpec((rows, LANES), lambda b, t: (b * nch + nch - 1 - t, 0)),
                  pl.BlockSpec((C_G, rows), lambda b, t: (0, b * nch + nch - 1 - t))],
        out_specs=[pl.BlockSpec((1, rows, C_V), fwd),
                   pl.BlockSpec((1, rows, C_V), bwd)],
        out_shape=[jax.ShapeDtypeStruct((bsz, lp, C_V), F32),
                   jax.ShapeDtypeStruct((bsz, lp, C_V), F32)],
        scratch_shapes=[pltpu.VMEM((2 * MC_HEADS, MC_V + MLSTM_EXT, MC_QK), F32),
                        pltpu.VMEM((2 * MC_HEADS, 8, LANES), F32)],
        compiler_params=_params(2),
        name="mlstm_scan",
    )(qk, vt, gc, gr, qk, vt, gc, gr)


def _merge_kernel(h_ref, oa_ref, ob_ref, hf_ref, hb_ref, co_ref, mg_ref, wg_ref, wb_ref, wo_ref,
                  lg_ref, lb_ref, o_ref, *, alpha):
    h = h_ref[...]
    hx = h.astype(MXU_DTYPE)
    hc = hf_ref[...] + hb_ref[...]
    parts = []
    for hd in range(MC_HEADS):
        sl = slice(hd * MC_V, (hd + 1) * MC_V)
        x = hc[:, sl]
        mu = jnp.mean(x, axis=-1, keepdims=True)
        xc = x - mu
        var = jnp.mean(xc * xc, axis=-1, keepdims=True)
        parts.append(xc * lax.rsqrt(var + LN_EPS) * mg_ref[:, sl] * co_ref[:, sl])
    oc = jnp.concatenate(parts, axis=1).astype(MXU_DTYPE)
    branches = (oa_ref[...], ob_ref[...], oc)
    merged = None
    for br in range(N_BRANCH):
        gate = jax.nn.sigmoid(_dot(hx, wg_ref[:, br * D_MODEL:(br + 1) * D_MODEL]))
        term = gate * _dot(branches[br], wb_ref[br])
        merged = term if merged is None else merged + term
    y = _dot(merged.astype(MXU_DTYPE), wo_ref[...])
    o_ref[...] = _layer_norm(alpha * h + y, lg_ref[...], lb_ref[...])


def _merge_call(h, oa, ob, hf, hb, co, mg, wg, wb, wo, lg, lb, alpha, tm):
    m, d = h.shape
    rows = lambda n: pl.BlockSpec((tm, n), lambda i: (i, 0))
    full2 = lambda a: pl.BlockSpec(a.shape, lambda i: (0, 0))
    return pl.pallas_call(
        functools.partial(_merge_kernel, alpha=alpha),
        grid=(m // tm,),
        in_specs=[rows(d), rows(A_V), rows(B_Q), rows(C_V), rows(C_V),
                  pl.BlockSpec((tm, C_O), lambda i: (i, (C_Q + C_K) // C_O)),
                  full2(mg), full2(wg), pl.BlockSpec(wb.shape, lambda i: (0, 0, 0)), full2(wo),
                  full2(lg), full2(lb)],
        out_specs=rows(d),
        out_shape=jax.ShapeDtypeStruct((m, d), F32),
        compiler_params=_params(1),
        name="merge_ln1",
    )(h, oa, ob, hf, hb, co, mg, wg, wb, wo, lg, lb)


def _split3(x):
    hi = x.astype(MXU_DTYPE)
    lo = (x - hi.astype(F32)).astype(MXU_DTYPE)
    return hi, lo


def _router_kernel(h_ref, w_ref, b_ref, rt_ref, rc_ref, *, tm, lp, l_end):
    x_hi, x_lo = _split3(h_ref[...])
    w_hi, w_lo = _split3(w_ref[...])
    logits = (_dot_nt(w_hi, x_hi) + _dot_nt(w_hi, x_lo) + _dot_nt(w_lo, x_hi)) + b_ref[...]
    none = float(N_EXPERTS)
    gl = logits[N_EXPERTS:N_EXPERTS + 8]
    grow = lax.broadcasted_iota(jnp.int32, gl.shape, 0).astype(F32)
    gmax = jnp.max(gl, axis=0, keepdims=True)
    g_sel = jnp.min(jnp.where(gl == gmax, grow, none), axis=0, keepdims=True)
    p_grp = 1.0 / jnp.sum(jnp.exp(gl - gmax), axis=0, keepdims=True)
    el = logits[0:N_EXPERTS]
    erow_i = lax.broadcasted_iota(jnp.int32, el.shape, 0)
    erow = erow_i.astype(F32)
    cand = jnp.where((erow_i // EXP_PER_GROUP).astype(F32) == g_sel, el, -jnp.inf)
    top1 = jnp.max(cand, axis=0, keepdims=True)
    i1 = jnp.min(jnp.where(cand == top1, erow, none), axis=0, keepdims=True)
    cand2 = jnp.where(erow == i1, -jnp.inf, cand)
    top2 = jnp.max(cand2, axis=0, keepdims=True)
    i2 = jnp.min(jnp.where(cand2 == top2, erow, none), axis=0, keepdims=True)
    e = jnp.exp(top2 - top1)
    w1 = (1.0 / (1.0 + e)) * p_grp
    w2 = (e / (1.0 + e)) * p_grp
    pos = lax.broadcasted_iota(jnp.int32, (1, tm), 1) + pl.program_id(0) * tm
    real = ((pos % lp) >= ROW_PAD) & ((pos % lp) < l_end)
    e1 = jnp.where(real, i1, none)
    e2 = jnp.where(real, i2, none)
    r = lax.broadcasted_iota(jnp.int32, (LANES, tm), 0)
    table = jnp.where(r == 0, e1, jnp.where(r == 1, e2, jnp.where(r == 2, w1, jnp.where(r == 3, w2, 0.0))))
    rt_ref[...] = table[0:8]
    rc_ref[...] = table.T


def _router_call(h, w, b, tm, lp, l_end):
    m, d = h.shape
    return pl.pallas_call(
        functools.partial(_router_kernel, tm=tm, lp=lp, l_end=l_end),
        grid=(m // tm,),
        in_specs=[pl.BlockSpec((tm, d), lambda i: (i, 0)),
                  pl.BlockSpec((LANES, d), lambda i: (0, 0)),
                  pl.BlockSpec((LANES, 1), lambda i: (0, 0))],
        out_specs=[pl.BlockSpec((8, tm), lambda i: (0, i)),
                   pl.BlockSpec((tm, LANES), lambda i: (i, 0))],
        out_shape=[jax.ShapeDtypeStruct((8, m), F32),
                   jax.ShapeDtypeStruct((m, LANES), F32)],
        compiler_params=_params(1),
        name="moe_router",
    )(h, w, b)


def _rank_kernel(rt_ref, rk_ref, cnt_ref, carry, *, tm):
    @pl.when(pl.program_id(0) == 0)
    def _():
        carry[...] = jnp.zeros_like(carry)

    erow = lax.broadcasted_iota(jnp.int32, (N_EXPERTS, tm), 0).astype(F32)
    oh1 = jnp.where(erow == rt_ref[0:1, :], 1.0, 0.0)
    oh2 = jnp.where(erow == rt_ref[1:2, :], 1.0, 0.0)
    oh = oh1 + oh2
    earlier = (lax.broadcasted_iota(jnp.int32, (tm, tm), 0)
               < lax.broadcasted_iota(jnp.int32, (tm, tm), 1))
    before = _dot(oh.astype(MXU_DTYPE), jnp.where(earlier, 1.0, 0.0).astype(MXU_DTYPE)) + carry[:, 0:1]
    r1 = jnp.sum(oh1 * before, axis=0, keepdims=True)
    r2 = jnp.sum(oh2 * before, axis=0, keepdims=True)
    r = lax.broadcasted_iota(jnp.int32, (8, tm), 0)
    rk_ref[...] = jnp.where(r == 0, r1, jnp.where(r == 1, r2, 0.0))
    total = carry[...] + jnp.sum(oh, axis=1, keepdims=True)
    carry[...] = total
    cnt_ref[...] = total


def _rank_call(rt, tm):
    m = rt.shape[1]
    return pl.pallas_call(
        functools.partial(_rank_kernel, tm=tm),
        grid=(m // tm,),
        in_specs=[pl.BlockSpec((8, tm), lambda i: (0, i))],
        out_specs=[pl.BlockSpec((8, tm), lambda i: (0, i)),
                   pl.BlockSpec((N_EXPERTS, LANES), lambda i: (0, 0))],
        out_shape=[jax.ShapeDtypeStruct((8, m), F32),
                   jax.ShapeDtypeStruct((N_EXPERTS, LANES), F32)],
        scratch_shapes=[pltpu.VMEM((N_EXPERTS, LANES), F32)],
        compiler_params=_params(1),
        name="moe_rank",
    )(rt)


def _dispatch_kernel(dest_ref, h_ref, xs_in_ref, xs_ref, sem, *, tm, m):
    del xs_in_ref
    base = pl.program_id(0) * tm

    def body(r, c):
        for k in range(2):
            d = dest_ref[k * m + base + r]
            pltpu.make_async_copy(h_ref.at[pl.ds(r, 1)], xs_ref.at[pl.ds(d, 1)], sem).start()
        return c

    lax.fori_loop(0, tm, body, 0, unroll=DMA_ISSUE_UNROLL)
    for _ in range(2):
        pltpu.make_async_copy(h_ref, xs_ref.at[pl.ds(0, tm)], sem).wait()


def _dispatch_call(dest, h, xs0, tm):
    m, d = h.shape
    return pl.pallas_call(
        functools.partial(_dispatch_kernel, tm=tm, m=m),
        grid_spec=pltpu.PrefetchScalarGridSpec(
            num_scalar_prefetch=1,
            grid=(m // tm,),
            in_specs=[pl.BlockSpec((tm, d), lambda i, dest_: (i, 0)),
                      pl.BlockSpec(memory_space=pl.ANY)],
            out_specs=pl.BlockSpec(memory_space=pl.ANY),
            scratch_shapes=[pltpu.SemaphoreType.DMA(())]),
        out_shape=jax.ShapeDtypeStruct(xs0.shape, xs0.dtype),
        input_output_aliases={2: 0},
        compiler_params=_params(1),
        name="moe_dispatch",
    )(dest, h, xs0)


def _ffn_kernel(be_ref, nu_ref, xs_ref, wg_ref, wu_ref, wd_ref, ys_ref, wg_s, wu_s, wd_s):
    i = pl.program_id(0)
    new_expert = jnp.logical_or(i == 0, be_ref[i] != be_ref[jnp.maximum(i - 1, 0)])

    @pl.when(new_expert)
    def _():
        wg_s[...] = wg_ref[0, 0].astype(MXU_DTYPE)
        wu_s[...] = wu_ref[0, 0].astype(MXU_DTYPE)
        wd_s[...] = wd_ref[0, 0].astype(MXU_DTYPE)

    @pl.when(i < nu_ref[0])
    def _():
        xb = xs_ref[...].astype(MXU_DTYPE)
        act = jax.nn.silu(_dot(xb, wg_s[...])) * _dot(xb, wu_s[...])
        ys_ref[...] = _dot(act.astype(MXU_DTYPE), wd_s[...])

    @pl.when(i >= nu_ref[0])
    def _():
        ys_ref[...] = jnp.zeros_like(ys_ref)


def _ffn_call(block_e, n_used, xs, wg, wu, wd, layer, n_blocks):
    d = xs.shape[1]
    br = EXPERT_ROWS
    rows = lambda i, be, nu: (jnp.minimum(i, nu[0] - 1), 0)
    return pl.pallas_call(
        _ffn_kernel,
        grid_spec=pltpu.PrefetchScalarGridSpec(
            num_scalar_prefetch=2,
            grid=(n_blocks,),
            in_specs=[pl.BlockSpec((br, d), rows),
                      pl.BlockSpec((1, 1, d, D_EXPERT), lambda i, be, nu: (layer, be[i], 0, 0)),
                      pl.BlockSpec((1, 1, d, D_EXPERT), lambda i, be, nu: (layer, be[i], 0, 0)),
                      pl.BlockSpec((1, 1, D_EXPERT, d), lambda i, be, nu: (layer, be[i], 0, 0))],
            out_specs=pl.BlockSpec((br, d), lambda i, be, nu: (i, 0)),
            scratch_shapes=[pltpu.VMEM((d, D_EXPERT), MXU_DTYPE),
                            pltpu.VMEM((d, D_EXPERT), MXU_DTYPE),
                            pltpu.VMEM((D_EXPERT, d), MXU_DTYPE)]),
        out_shape=jax.ShapeDtypeStruct((n_blocks * br, d), F32),
        compiler_params=_params(1),
        name="moe_experts",
    )(block_e, n_used, xs, wg, wu, wd)


def _combine_kernel(src_ref, h_ref, rc_ref, lg_ref, lb_ref, ys_ref, o_ref, buf, sem, *, tm, m, alpha):
    i = pl.program_id(0)

    def issue(tile, slot):
        base = tile * tm

        def body(r, c):
            for k in range(2):
                s = src_ref[k * m + base + r]
                pltpu.make_async_copy(ys_ref.at[pl.ds(s, 1)], buf.at[slot, k, pl.ds(r, 1)], sem.at[slot]).start()
            return c

        lax.fori_loop(0, tm, body, 0, unroll=DMA_ISSUE_UNROLL)

    @pl.when(i == 0)
    def _():
        issue(0, 0)

    @pl.when(i + 1 < m // tm)
    def _():
        issue(i + 1, (i + 1) % 2)

    slot = i % 2
    for k in range(2):
        pltpu.make_async_copy(ys_ref.at[pl.ds(0, tm)], buf.at[slot, k], sem.at[slot]).wait()
    rc = rc_ref[...]
    real = rc[:, 0:1] < float(N_EXPERTS)
    y = jnp.where(real, rc[:, 2:3] * buf[slot, 0] + rc[:, 3:4] * buf[slot, 1], 0.0)
    o_ref[...] = _layer_norm(alpha * h_ref[...] + y, lg_ref[...], lb_ref[...])


def _combine_call(src, h, rc, lg, lb, ys, alpha, tm):
    m, d = h.shape
    n_tiles = m // tm
    return pl.pallas_call(
        functools.partial(_combine_kernel, tm=tm, m=m, alpha=alpha),
        grid_spec=pltpu.PrefetchScalarGridSpec(
            num_scalar_prefetch=1,
            grid=(n_tiles,),
            in_specs=[pl.BlockSpec((tm, d), lambda i, s: (i, 0)),
                      pl.BlockSpec((tm, LANES), lambda i, s: (i, 0)),
                      pl.BlockSpec((1, d), lambda i, s: (0, 0)),
                      pl.BlockSpec((1, d), lambda i, s: (0, 0)),
                      pl.BlockSpec(memory_space=pl.ANY)],
            out_specs=pl.BlockSpec((tm, d), lambda i, s: (i, 0)),
            scratch_shapes=[pltpu.VMEM((2, 2, tm, d), F32),
                            pltpu.SemaphoreType.DMA((2,))]),
        out_shape=jax.ShapeDtypeStruct((m, d), F32),
        compiler_params=_params(1),
        name="moe_combine_ln2",
    )(src, h, rc, lg, lb, ys)


def _moe(h1, w_rg, b_rg, w_re, b_re, w_gate, w_up, w_down, layer, lg, lb, alpha, bsz, lp, l_end, tm):
    m, d = h1.shape
    wr = jnp.zeros((LANES, d), F32).at[0:N_EXPERTS].set(w_re.T).at[N_EXPERTS:N_EXPERTS + N_GROUPS].set(w_rg.T)
    br_ = jnp.zeros((LANES,), F32).at[0:N_EXPERTS].set(b_re).at[N_EXPERTS:N_EXPERTS + N_GROUPS].set(b_rg)
    br_ = br_.at[N_EXPERTS + N_GROUPS:N_EXPERTS + 8].set(NEG).reshape(LANES, 1)
    rt, rc = _router_call(h1, wr, br_, tm, lp, l_end)
    rk, cnt = _rank_call(rt, tm)

    rows = EXPERT_ROWS
    n_assign = 2 * bsz * (l_end - ROW_PAD)
    n_unused = lp - (l_end - ROW_PAD)
    n_blocks = -(-(n_assign + N_EXPERTS * (rows - 1)) // rows)
    n_slots = n_blocks * rows
    counts = cnt[:, 0].astype(jnp.int32)
    pcounts = (counts + rows - 1) // rows * rows
    pend = jnp.cumsum(pcounts)
    pstart = pend - pcounts
    e = rt[0:2].astype(jnp.int32)
    rank = rk[0:2].astype(jnp.int32)
    real = e < N_EXPERTS
    expert_ids = jnp.arange(N_EXPERTS, dtype=jnp.int32)[:, None, None]
    slot = jnp.sum(jnp.where(e[None] == expert_ids, pstart[:, None, None], 0), axis=0) + rank
    tok = jnp.arange(m, dtype=jnp.int32)
    pos = tok % lp
    unused_idx = (tok // lp) * n_unused + jnp.where(pos < ROW_PAD, pos, pos - l_end + ROW_PAD)
    spare = n_slots + 2 * unused_idx[None, :] + jnp.arange(2, dtype=jnp.int32)[:, None]
    dest = jnp.where(real, slot, spare).reshape(-1)
    src = jnp.where(real, slot, 0).reshape(-1)
    block_start = jnp.arange(n_blocks, dtype=jnp.int32) * rows
    block_e = jnp.minimum(jnp.sum((pend[None, :] <= block_start[:, None]).astype(jnp.int32), axis=1),
                          N_EXPERTS - 1)
    n_used = (pend[-1:] // rows).astype(jnp.int32)

    n_spare = -(-(2 * bsz * n_unused) // rows) * rows
    xs = _dispatch_call(dest, h1, jnp.zeros((n_slots + n_spare, d), F32), tm)
    ys = _ffn_call(block_e, n_used, xs, w_gate, w_up, w_down, layer, n_blocks)
    return _combine_call(src, h1, rc, lg, lb, ys, alpha, _token_tile(m, 384))


def _rope_tables(lp):
    pos = jnp.arange(lp, dtype=F32) - float(ROW_PAD)
    inv = 1.0 / (ROPE_THETA ** (jnp.arange(0, DA_DIM, 2, dtype=F32) / DA_DIM))
    ang = pos[:, None] * inv[None, :]
    reps = LANES // (DA_DIM // 2)
    sign = jnp.tile(jnp.concatenate([-jnp.ones((DA_DIM // 2,), F32), jnp.ones((DA_DIM // 2,), F32)]), LANES // DA_DIM)
    return jnp.tile(jnp.cos(ang), (1, reps)), jnp.tile(jnp.sin(ang), (1, reps)) * sign[None, :]


def _dup_heads(w, n_heads, dim):
    d = w.shape[0]
    return jnp.broadcast_to(w.reshape(d, n_heads, 1, dim), (d, n_heads, 2, dim)).reshape(d, n_heads * 2 * dim)


def kernel(x, meta, ln_in_g, ln_in_b, w_in, conv_w, conv_b, gate_b, lam_q1, lam_k1, lam_q2, lam_k2, diff_g, sink, mlstm_g, w_branch, w_out, ln1_g, ln1_b, ln2_g, ln2_b, w_rg, b_rg, w_re, b_re, w_gate, w_up, w_down):
    bsz, seq, d = x.shape
    depth = w_in.shape[0]
    assert seq % BLOCK == 0 and d == D_MODEL
    l_end = seq + BLOCK
    lp = -(-l_end // MXU_TILE) * MXU_TILE
    m = bsz * lp
    alpha = (2.0 * depth) ** 0.25
    tm = _row_tile(lp, 768)
    tk_attn = _row_tile(lp, 768)

    hp = jnp.concatenate([jnp.zeros((bsz, ROW_PAD, d), x.dtype),
                          jnp.broadcast_to(meta.astype(x.dtype)[None], (bsz, N_META_TOK, d)), x,
                          jnp.zeros((bsz, lp - l_end, d), x.dtype)], axis=1)
    h = _ln_call(hp.reshape(m, d), ln_in_g, ln_in_b, tm)
    cos, sin = _rope_tables(lp)
    q_scale = DA_DIM ** -0.5
    rope_scale = jnp.concatenate([jnp.full((A_Q,), q_scale * math.log2(math.e), F32), jnp.ones((A_K,), F32),
                                  jnp.full((B_Q,), q_scale, F32), jnp.ones((2 * B_K,), F32)]).reshape(1, -1)
    conv_scale = jnp.concatenate([jnp.ones((C_Q,), F32), jnp.full((C_K,), MC_QK ** -0.5, F32)]).reshape(1, -1)

    for l in range(depth):
        lam_init = 0.8 - 0.6 * math.exp(-0.3 * l)
        wl = w_in[l]
        col = lambda i: wl[:, OFFS[i]:OFFS[i + 1]]
        w_rope = jnp.concatenate([col(0), col(1), col(3), _dup_heads(col(4), WB_KV, WB_DIM),
                                  _dup_heads(col(5), WB_KV, WB_DIM)], axis=1).astype(MXU_DTYPE)
        w_vt = jnp.concatenate([col(8), col(2)], axis=1).T.astype(MXU_DTYPE)
        w_conv = jnp.concatenate([col(6), col(7), col(9)], axis=1).astype(MXU_DTYPE)
        w_g = jnp.pad(col(10), ((0, 0), (0, LANES - C_G))).astype(MXU_DTYPE)
        b_g = jnp.pad(gate_b[l], (0, LANES - C_G)).reshape(1, LANES)
        w_mg = col(11).astype(MXU_DTYPE)

        rq = _proj_rope_call(h, w_rope, cos, sin, rope_scale, tm, lp).reshape(bsz, lp, -1)
        vt = _proj_t_call(h, w_vt, tm, "proj_val_t")
        zco = _proj_call(h, w_conv, tm, F32, C_Q + C_K, "proj_conv_gate")
        gc, gr = _gates_call(h, w_g, b_g, tm, lp, l_end)

        lamv = jnp.stack([lam_q1[l], lam_k1[l], lam_q2[l], lam_k2[l]])
        out_a = _attn_a_call(rq, vt, lamv, diff_g[l].reshape(-1, 1), lam_init, l_end, MXU_TILE, tk_attn)
        out_b = _attn_b_call(rq, sink[l].reshape(1, -1), l_end)
        qk = _conv_call(zco.reshape(bsz, lp, -1), conv_w[l], conv_b[l].reshape(1, -1), conv_scale, l_end,
                        _row_tile(lp, 768))
        h_f, h_b = _mlstm_call(qk, vt, gc, gr)

        h = _merge_call(h, out_a.reshape(m, -1), out_b.reshape(m, -1), h_f.reshape(m, -1), h_b.reshape(m, -1),
                        zco, mlstm_g[l].reshape(1, -1), w_mg, w_branch[l].astype(MXU_DTYPE),
                        w_out[l].astype(MXU_DTYPE), ln1_g[l].reshape(1, -1), ln1_b[l].reshape(1, -1),
                        alpha, _token_tile(m, 384))
        h = _moe(h, w_rg[l], b_rg[l], w_re[l], b_re[l], w_gate, w_up, w_down, l,
                 ln2_g[l].reshape(1, -1), ln2_b[l].reshape(1, -1), alpha, bsz, lp, l_end, tm)
    return h.reshape(bsz, lp, d)[:, BLOCK:l_end]
```

```python
import functools
import math

import numpy as np
import jax
import jax.numpy as jnp
from jax import lax
from jax.experimental import pallas as pl
from jax.experimental.pallas import tpu as pltpu

D_MODEL = 1024
N_META_TOK = 16
BLOCK = 128
ROW_PAD = BLOCK - N_META_TOK
ROPE_THETA = 10000.0
LN_EPS = 1e-5
NEG = -1e30

DA_HEADS = 4
DA_DIM = 64
WB_HEADS = 8
WB_KV = 2
WB_DIM = 64
WINDOW = 128
MC_HEADS = 4
MC_QK = 128
MC_V = 128
N_BRANCH = 3
BRANCH_W = 512
N_GROUPS = 4
EXP_PER_GROUP = 8
N_EXPERTS = N_GROUPS * EXP_PER_GROUP
D_EXPERT = 512

A_Q = DA_HEADS * 2 * DA_DIM
A_K = A_Q
A_V = A_Q
B_Q = WB_HEADS * WB_DIM
B_K = WB_KV * WB_DIM
B_V = B_K
C_Q = MC_HEADS * MC_QK
C_K = C_Q
C_V = MC_HEADS * MC_V
C_O = C_V
C_G = 4 * MC_HEADS
GATE_W = N_BRANCH * D_MODEL
SPLITS = (A_Q, A_K, A_V, B_Q, B_K, B_V, C_Q, C_K, C_V, C_O, C_G, GATE_W)
OFFS = tuple(int(v) for v in np.cumsum((0,) + SPLITS))

LANES = 128
MXU_TILE = 256
EXPERT_ROWS = 512
WINDOW_BLOCKS_PER_STEP = 6
MLSTM_CHUNKS_PER_STEP = 6
MLSTM_EXT = 16
DMA_ISSUE_UNROLL = 8
ATTN_Q_TILES = 3
ATTN_UNROLL = 2
ATTN_EXT = 16
VMEM_LIMIT = 56 * 1024 * 1024

F32 = jnp.float32
MXU_DTYPE = jnp.bfloat16


def _dot(a, b):
    return jnp.dot(a, b, preferred_element_type=F32)


def _dot_nt(a, b):
    return lax.dot_general(a, b, (((1,), (1,)), ((), ())), preferred_element_type=F32)


def _dot_tn(a, b):
    return lax.dot_general(a, b, (((0,), (0,)), ((), ())), preferred_element_type=F32)


def _params(n_axes, flags=None):
    return pltpu.CompilerParams(dimension_semantics=("arbitrary",) * n_axes,
                                vmem_limit_bytes=VMEM_LIMIT, flags=flags)


def _row_tile(n_rows, target):
    best = BLOCK
    for t in range(BLOCK, target + 1, BLOCK):
        if n_rows % t == 0:
            best = t
    return best


def _token_tile(n_rows, target):
    best = 8
    for t in range(8, target + 1, 8):
        if n_rows % t == 0:
            best = t
    return best


def _layer_norm(x, g, b):
    mu = jnp.mean(x, axis=-1, keepdims=True)
    xc = x - mu
    var = jnp.mean(xc * xc, axis=-1, keepdims=True)
    return xc * lax.rsqrt(var + LN_EPS) * g + b


def _ln_kernel(x_ref, g_ref, b_ref, o_ref):
    o_ref[...] = _layer_norm(x_ref[...], g_ref[...], b_ref[...])


def _ln_call(x, g, b, tm):
    m, d = x.shape
    return pl.pallas_call(
        _ln_kernel,
        grid=(m // tm,),
        in_specs=[pl.BlockSpec((tm, d), lambda i: (i, 0)),
                  pl.BlockSpec((1, d), lambda i: (0, 0)),
                  pl.BlockSpec((1, d), lambda i: (0, 0))],
        out_specs=pl.BlockSpec((tm, d), lambda i: (i, 0)),
        out_shape=jax.ShapeDtypeStruct((m, d), F32),
        compiler_params=_params(1),
        name="ln_in",
    )(x, g.reshape(1, d), b.reshape(1, d))


def _proj_rope_kernel(x_ref, w_ref, cos_ref, sin_ref, scale_ref, o_ref, *, n_rope):
    z = _dot(x_ref[...].astype(MXU_DTYPE), w_ref[...])
    cos = cos_ref[...]
    sin = sin_ref[...]
    lane = lax.broadcasted_iota(jnp.int32, cos.shape, 1)
    first_half = (lane % DA_DIM) < (DA_DIM // 2)
    for c in range(n_rope // LANES):
        sl = slice(c * LANES, (c + 1) * LANES)
        zc = z[:, sl]
        partner = jnp.where(first_half, pltpu.roll(zc, LANES - DA_DIM // 2, 1),
                            pltpu.roll(zc, DA_DIM // 2, 1))
        o_ref[:, sl] = ((zc * cos + partner * sin) * scale_ref[:, sl]).astype(o_ref.dtype)
    o_ref[:, n_rope:] = z[:, n_rope:].astype(o_ref.dtype)


def _proj_rope_call(h, w, cos, sin, scale, tm, lp):
    m, d = h.shape
    n = w.shape[1]
    per_batch = lp // tm
    return pl.pallas_call(
        functools.partial(_proj_rope_kernel, n_rope=scale.shape[1]),
        grid=(m // tm,),
        in_specs=[pl.BlockSpec((tm, d), lambda i: (i, 0)),
                  pl.BlockSpec((d, n), lambda i: (0, 0)),
                  pl.BlockSpec((tm, LANES), lambda i: (i % per_batch, 0)),
                  pl.BlockSpec((tm, LANES), lambda i: (i % per_batch, 0)),
                  pl.BlockSpec(scale.shape, lambda i: (0, 0))],
        out_specs=pl.BlockSpec((tm, n), lambda i: (i, 0)),
        out_shape=jax.ShapeDtypeStruct((m, n), MXU_DTYPE),
        compiler_params=_params(1),
        name="proj_rope",
    )(h, w, cos, sin, scale)


def _proj_kernel(x_ref, w_ref, o_ref, *, sigmoid_from):
    z = _dot(x_ref[...].astype(MXU_DTYPE), w_ref[...])
    if sigmoid_from is None:
        o_ref[...] = z.astype(o_ref.dtype)
    else:
        o_ref[:, :sigmoid_from] = z[:, :sigmoid_from].astype(o_ref.dtype)
        o_ref[:, sigmoid_from:] = jax.nn.sigmoid(z[:, sigmoid_from:]).astype(o_ref.dtype)


def _proj_call(h, w, tm, out_dtype, sigmoid_from, name):
    m, d = h.shape
    n = w.shape[1]
    return pl.pallas_call(
        functools.partial(_proj_kernel, sigmoid_from=sigmoid_from),
        grid=(m // tm,),
        in_specs=[pl.BlockSpec((tm, d), lambda i: (i, 0)),
                  pl.BlockSpec((d, n), lambda i: (0, 0))],
        out_specs=pl.BlockSpec((tm, n), lambda i: (i, 0)),
        out_shape=jax.ShapeDtypeStruct((m, n), out_dtype),
        compiler_params=_params(1),
        name=name,
    )(h, w)


def _proj_t_kernel(x_ref, wt_ref, o_ref):
    o_ref[...] = _dot_nt(wt_ref[...], x_ref[...].astype(MXU_DTYPE)).astype(o_ref.dtype)


def _proj_t_call(h, wt, tm, name):
    m, d = h.shape
    n = wt.shape[0]
    return pl.pallas_call(
        _proj_t_kernel,
        grid=(m // tm,),
        in_specs=[pl.BlockSpec((tm, d), lambda i: (i, 0)),
                  pl.BlockSpec((n, d), lambda i: (0, 0))],
        out_specs=pl.BlockSpec((n, tm), lambda i: (0, i)),
        out_shape=jax.ShapeDtypeStruct((n, m), MXU_DTYPE),
        compiler_params=_params(1),
        name=name,
    )(h, wt)


def _gates_kernel(x_ref, w_ref, b_ref, gc_ref, gr_ref, *, tm, lp, l_end):
    z = _dot(x_ref[...].astype(MXU_DTYPE), w_ref[...]) + b_ref[...]
    lane = lax.broadcasted_iota(jnp.int32, (tm, LANES), 1)
    kind = lane // MC_HEADS
    row = lax.broadcasted_iota(jnp.int32, (tm, LANES), 0) + pl.program_id(0) * tm
    pos = row % lp
    unused = (pos < ROW_PAD) | (pos >= l_end)
    log_f = jnp.minimum(z, 0.0) - jnp.log1p(jnp.exp(-jnp.abs(z)))
    is_forget = (kind % 2) == 1
    base = jnp.where(is_forget, jnp.where(unused, 0.0, log_f), jnp.where(unused, NEG, z))
    r128 = lax.broadcasted_iota(jnp.int32, (BLOCK, LANES), 0)
    fwd_lane = lax.broadcasted_iota(jnp.int32, (BLOCK, LANES), 1) // MC_HEADS == 1
    forget128 = (lax.broadcasted_iota(jnp.int32, (BLOCK, LANES), 1) // MC_HEADS) % 2 == 1
    for c in range(tm // BLOCK):
        x = base[c * BLOCK:(c + 1) * BLOCK]
        pre = x
        suf = x
        s = 1
        while s < BLOCK:
            pre = pre + jnp.where(r128 >= s, pltpu.roll(pre, s, 0), 0.0)
            suf = suf + jnp.where(r128 < BLOCK - s, pltpu.roll(suf, BLOCK - s, 0), 0.0)
            s *= 2
        out = jnp.where(forget128, jnp.where(fwd_lane, pre, suf), x)
        gc_ref[c * BLOCK:(c + 1) * BLOCK, :] = out
        gr_ref[:, c * BLOCK:(c + 1) * BLOCK] = out.T[0:C_G, :]


def _gates_call(h, w, b, tm, lp, l_end):
    m, d = h.shape
    return pl.pallas_call(
        functools.partial(_gates_kernel, tm=tm, lp=lp, l_end=l_end),
        grid=(m // tm,),
        in_specs=[pl.BlockSpec((tm, d), lambda i: (i, 0)),
                  pl.BlockSpec((d, LANES), lambda i: (0, 0)),
                  pl.BlockSpec((1, LANES), lambda i: (0, 0))],
        out_specs=[pl.BlockSpec((tm, LANES), lambda i: (i, 0)),
                   pl.BlockSpec((C_G, tm), lambda i: (0, i))],
        out_shape=[jax.ShapeDtypeStruct((m, LANES), F32),
                   jax.ShapeDtypeStruct((C_G, m), F32)],
        compiler_params=_params(1),
        name="mlstm_gates",
    )(h, w, b)


def _conv_kernel(z_ref, w_ref, b_ref, scale_ref, o_ref, *, lp, l_end, tr):
    w0 = w_ref[0:1, :]
    w1 = w_ref[1:2, :]
    w2 = w_ref[2:3, :]
    row = lax.broadcasted_iota(jnp.int32, (tr, LANES), 0)
    for c in range(lp // tr):
        r0 = c * tr
        zc = z_ref[0, r0:r0 + tr, :]
        before = jnp.zeros((1, LANES), F32) if r0 == 0 else z_ref[0, r0 - 1:r0, :]
        after = jnp.zeros((1, LANES), F32) if r0 + tr == lp else z_ref[0, r0 + tr:r0 + tr + 1, :]
        prev = jnp.where(row == 0, before, pltpu.roll(zc, 1, 0))
        nxt = jnp.where(row == tr - 1, after, pltpu.roll(zc, tr - 1, 0))
        if r0 <= ROW_PAD < r0 + tr:
            prev = jnp.where(row == ROW_PAD - r0, 0.0, prev)
        if r0 <= l_end - 1 < r0 + tr:
            nxt = jnp.where(row == l_end - 1 - r0, 0.0, nxt)
        y = prev * w0 + zc * w1 + nxt * w2 + b_ref[...]
        o_ref[0, r0:r0 + tr, :] = (jax.nn.silu(y) * scale_ref[...]).astype(o_ref.dtype)


def _conv_call(z, w, b, scale, l_end, tr):
    bsz, lp, _ = z.shape
    n = w.shape[1]
    return pl.pallas_call(
        functools.partial(_conv_kernel, lp=lp, l_end=l_end, tr=tr),
        grid=(bsz, n // LANES),
        in_specs=[pl.BlockSpec((1, lp, LANES), lambda b_, j: (b_, 0, j)),
                  pl.BlockSpec((3, LANES), lambda b_, j: (0, j)),
                  pl.BlockSpec((1, LANES), lambda b_, j: (0, j)),
                  pl.BlockSpec((1, LANES), lambda b_, j: (0, j))],
        out_specs=pl.BlockSpec((1, lp, LANES), lambda b_, j: (b_, 0, j)),
        out_shape=jax.ShapeDtypeStruct((bsz, lp, n), MXU_DTYPE),
        compiler_params=_params(2),
        name="mlstm_conv",
    )(z, w, b, scale)


def _attn_a_kernel(lamv_ref, g_ref, q_ref, k_ref, vt_ref, o_ref, s0_scr, s1_scr, acc_scr, *,
                   tk, n_chunks, l_end, lam_init):
    n_ch = acc_scr.shape[0]
    tq = acc_scr.shape[2]
    feat = lax.broadcasted_iota(jnp.int32, (2 * DA_DIM, tq), 0)
    key_row = lax.broadcasted_iota(jnp.int32, (tk, tq), 0)
    qz = []
    for t in range(n_ch // 2):
        qt = q_ref[0, t * tq:(t + 1) * tq, :].astype(F32).T.astype(MXU_DTYPE)
        zero = jnp.zeros_like(qt)
        qz += [jnp.where(feat < DA_DIM, qt, zero), jnp.where(feat >= DA_DIM, qt, zero)]
    s_bufs = (s0_scr, s1_scr)

    last = n_chunks - 1
    last_hi = l_end - last * tk

    def scores(j, slot):
        start = pl.multiple_of(j * tk, tk)
        kj = k_ref[0, pl.ds(start, tk), :]
        cmax = []
        for c in range(n_ch):
            s = _dot(kj, qz[c])
            if isinstance(j, int) and j == 0:
                s = jnp.where(key_row >= ROW_PAD, s, NEG)
            if isinstance(j, int) and j == last and last_hi < tk:
                s = jnp.where(key_row < last_hi, s, NEG)
            s_bufs[slot][c] = s
            cmax.append(jnp.max(s, axis=0, keepdims=True))
        return tuple(cmax)

    ones_rows = jnp.ones((ATTN_EXT, tk), MXU_DTYPE)

    def softmax_values(j, slot, stats, cmax):
        start = pl.multiple_of(j * tk, tk)
        vt = jnp.concatenate([vt_ref[:, pl.ds(start, tk)], ones_rows], axis=0)
        new_stats = []
        for c in range(n_ch):
            m_new = jnp.maximum(stats[c], cmax[c])
            alpha = jnp.exp2(stats[c] - m_new)
            p = jnp.exp2((s_bufs[slot][c] - m_new).astype(MXU_DTYPE))
            acc_scr[c] = alpha * acc_scr[c] + _dot(vt, p)
            new_stats.append(m_new)
        return tuple(new_stats)

    one = jnp.full((1, tq), NEG, F32)
    acc_scr[...] = jnp.zeros_like(acc_scr)
    cmax = scores(0, 0)
    stats = (one,) * n_ch
    if n_chunks > 1:
        def step(j, parity, state):
            stats, cmax = state
            nxt = scores(j + 1, 1 - parity)
            return softmax_values(j, parity, stats, cmax), nxt

        def trip(i, st):
            for u in range(ATTN_UNROLL):
                st = step(1 + ATTN_UNROLL * i + u, (1 + u) % 2, st)
            return st

        state = step(0, 0, (stats, cmax))
        n_trips = (last - 2) // ATTN_UNROLL if last >= 2 else 0
        state = lax.fori_loop(0, n_trips, trip, state)
        for j in range(1 + n_trips * ATTN_UNROLL, last):
            state = step(j, j % 2, state)
        stats, cmax = state
    softmax_values(last, last % 2, stats, cmax)
    dv = 2 * DA_DIM
    lv = lamv_ref[...]
    lam = (jnp.exp(jnp.sum(lv[0:1] * lv[1:2], axis=-1, keepdims=True))
           - jnp.exp(jnp.sum(lv[2:3] * lv[3:4], axis=-1, keepdims=True)) + lam_init)
    for t in range(n_ch // 2):
        o0 = acc_scr[2 * t, 0:dv, :] / acc_scr[2 * t, dv:dv + 1, :]
        o1 = acc_scr[2 * t + 1, 0:dv, :] / acc_scr[2 * t + 1, dv:dv + 1, :]
        o = o0 - lam * o1
        ms = jnp.mean(o * o, axis=0, keepdims=True)
        o = o * lax.rsqrt(ms + LN_EPS) * g_ref[...] * (1.0 - lam_init)
        o_ref[0, t * tq:(t + 1) * tq, :] = o.T.astype(o_ref.dtype)


def _attn_a_call(rq, vt, lamv, g_col, lam_init, l_end, tq, tk):
    bsz, lp, _ = rq.shape
    k_blk = A_Q // LANES
    vt_blk = C_V // LANES
    n_q = ATTN_Q_TILES if (lp // tq) % ATTN_Q_TILES == 0 else 1
    n_ch = 2 * n_q
    return pl.pallas_call(
        functools.partial(_attn_a_kernel, tk=tk, n_chunks=lp // tk, l_end=l_end, lam_init=lam_init),
        grid=(bsz, DA_HEADS, lp // (n_q * tq)),
        in_specs=[pl.BlockSpec((4, DA_DIM), lambda b, h, i: (0, 0)),
                  pl.BlockSpec((2 * DA_DIM, 1), lambda b, h, i: (0, 0)),
                  pl.BlockSpec((1, n_q * tq, LANES), lambda b, h, i: (b, i, h)),
                  pl.BlockSpec((1, lp, LANES), lambda b, h, i: (b, 0, k_blk + h)),
                  pl.BlockSpec((2 * DA_DIM, lp), lambda b, h, i: (vt_blk + h, b))],
        out_specs=pl.BlockSpec((1, n_q * tq, LANES), lambda b, h, i: (b, i, h)),
        out_shape=jax.ShapeDtypeStruct((bsz, lp, A_V), MXU_DTYPE),
        scratch_shapes=[pltpu.VMEM((n_ch, tk, tq), F32),
                        pltpu.VMEM((n_ch, tk, tq), F32),
                        pltpu.VMEM((n_ch, 2 * DA_DIM + ATTN_EXT, tq), F32)],
        compiler_params=_params(3),
        name="diff_attn",
    )(lamv, g_col, rq, rq, vt)


def _attn_b_kernel(sink_ref, band_ref, q_ref, k0_ref, k1_ref, v0_ref, v1_ref, o_ref, *, lp, l_end):
    n_sub = q_ref.shape[1] // BLOCK
    for sub in range(n_sub):
        _attn_b_block(sink_ref, band_ref, q_ref, k0_ref, k1_ref, v0_ref, v1_ref, o_ref,
                      pl.program_id(1) * n_sub + sub, slice(sub * BLOCK, (sub + 1) * BLOCK), lp, l_end)


def _attn_b_block(sink_ref, band_ref, q_ref, k0_ref, k1_ref, v0_ref, v1_ref, o_ref, n, rows, lp, l_end):
    nb = lp // BLOCK
    grp = WB_HEADS // WB_KV

    def blocks(ref):
        parts = [ref[0, 0:BLOCK, :]]
        for d in (-1, 0, 1):
            idx = jnp.clip(n + d, 0, nb - 1)
            parts.append(ref[0, pl.ds(pl.multiple_of(idx * BLOCK, BLOCK), BLOCK), :])
        return jnp.concatenate(parts, axis=0)

    keys = (blocks(k0_ref), blocks(k1_ref))
    vals = (blocks(v0_ref), blocks(v1_ref))
    blk_bias = []
    for d in (-1, 0, 1):
        inside = jnp.logical_and(n + d >= 1, n + d <= l_end // BLOCK - 1)
        blk_bias.append(jnp.where(inside, 0.0, NEG))
    head_of_col = lax.broadcasted_iota(jnp.int32, (1, grp * BLOCK), 1) // BLOCK
    lane = lax.broadcasted_iota(jnp.int32, (BLOCK, LANES), 1)
    sinks, raw = [], []
    for g in range(WB_KV):
        qs = []
        sink = jnp.zeros((1, grp * BLOCK), F32)
        for j in range(grp):
            h = g * grp + j
            qt = q_ref[0, rows, (h // 2) * LANES:(h // 2 + 1) * LANES]
            keep = (lane >= WB_DIM) if h % 2 else (lane < WB_DIM)
            qs.append(jnp.where(keep, qt, jnp.zeros_like(qt)))
            sink = jnp.where(head_of_col == j, sink_ref[:, h:h + 1], sink)
        sinks.append(sink)
        raw.append(_dot_nt(keys[g], jnp.concatenate(qs, axis=0)))
    weights, dens = [], []
    for g in range(WB_KV):
        s = raw[g] + band_ref[...]
        s = jnp.concatenate([s[0:BLOCK]] + [s[(i + 1) * BLOCK:(i + 2) * BLOCK] + blk_bias[i] for i in range(3)],
                            axis=0)
        m = jnp.maximum(jnp.max(s, axis=0, keepdims=True), sinks[g])
        p = jnp.exp(s - m)
        dens.append(jnp.sum(p, axis=0, keepdims=True) + jnp.exp(sinks[g] - m))
        weights.append(p.astype(MXU_DTYPE))
    outs = [_dot_tn(vals[g], weights[g]) for g in range(WB_KV)]
    for g in range(WB_KV):
        o = (outs[g] / dens[g]).T
        for jj in range(grp // 2):
            lo = o[(2 * jj) * BLOCK:(2 * jj + 1) * BLOCK]
            hi = o[(2 * jj + 1) * BLOCK:(2 * jj + 2) * BLOCK]
            t = (g * grp) // 2 + jj
            o_ref[0, rows, t * LANES:(t + 1) * LANES] = jnp.where(lane < WB_DIM, lo, hi).astype(o_ref.dtype)


def _attn_b_call(rq, sink, l_end):
    bsz, lp, _ = rq.shape
    q_blk = (A_Q + A_K) // B_Q
    k_blk = (A_Q + A_K + B_Q) // LANES
    v_blk = k_blk + 2 * B_K // LANES
    n_sub = WINDOW_BLOCKS_PER_STEP if (lp // BLOCK) % WINDOW_BLOCKS_PER_STEP == 0 else 1
    q_rows = n_sub * BLOCK
    grp = WB_HEADS // WB_KV
    krow = np.arange(4 * BLOCK)[:, None]
    qoff = np.arange(grp * BLOCK)[None, :] % BLOCK
    ok = np.where(krow < BLOCK, krow >= ROW_PAD, np.abs(qoff + 2 * BLOCK - krow) <= WINDOW)
    band = jnp.asarray(np.where(ok, 0.0, NEG), F32)
    seq = lambda c: pl.BlockSpec((1, lp, LANES), lambda b, n: (b, 0, c))
    return pl.pallas_call(
        functools.partial(_attn_b_kernel, lp=lp, l_end=l_end),
        grid=(bsz, lp // q_rows),
        in_specs=[pl.BlockSpec((1, WB_HEADS), lambda b, n: (0, 0)),
                  pl.BlockSpec(band.shape, lambda b, n: (0, 0)),
                  pl.BlockSpec((1, q_rows, B_Q), lambda b, n: (b, n, q_blk)),
                  seq(k_blk), seq(k_blk + 1), seq(v_blk), seq(v_blk + 1)],
        out_specs=pl.BlockSpec((1, q_rows, B_Q), lambda b, n: (b, n, 0)),
        out_shape=jax.ShapeDtypeStruct((bsz, lp, B_Q), MXU_DTYPE),
        compiler_params=_params(2),
        name="window_attn",
    )(sink, band, rq, rq, rq, rq, rq)


def _mlstm_kernel(qkf_ref, vf_ref, gcf_ref, grf_ref, qkb_ref, vb_ref, gcb_ref, grb_ref,
                  hf_ref, hb_ref, c_scr, m_scr):
    t = pl.program_id(1)

    @pl.when(t == 0)
    def _():
        c_scr[...] = jnp.zeros_like(c_scr)
        m_scr[...] = jnp.zeros_like(m_scr)

    srow = lax.broadcasted_iota(jnp.int32, (BLOCK, BLOCK), 0)
    ccol = lax.broadcasted_iota(jnp.int32, (BLOCK, BLOCK), 1)
    ext_row = lax.broadcasted_iota(jnp.int32, (MLSTM_EXT, BLOCK), 0)
    ones_rows = jnp.where(ext_row == 0, 1.0, 0.0).astype(MXU_DTYPE)
    n_sub = qkf_ref.shape[1] // BLOCK
    prepared = []
    for sub in range(n_sub):
        rows_f = slice(sub * BLOCK, (sub + 1) * BLOCK)
        rows_b = slice((n_sub - 1 - sub) * BLOCK, (n_sub - sub) * BLOCK)
        dirs = ((qkf_ref, vf_ref, gcf_ref, grf_ref, hf_ref, rows_f, srow <= ccol, BLOCK - 1),
                (qkb_ref, vb_ref, gcb_ref, grb_ref, hb_ref, rows_b, srow >= ccol, 0))
        prepared.append(_mlstm_prepare(dirs, ones_rows))
    for chains in prepared:
        _mlstm_update(chains, c_scr, m_scr)


def _mlstm_prepare(dirs, ones_rows):
    chains = []
    for d, (qk_ref, vt_ref, gc_ref, gr_ref, h_ref, rows, tri, last) in enumerate(dirs):
        for hd in range(MC_HEADS):
            ci = d * MC_HEADS + hd
            j_li = (2 * d) * MC_HEADS + hd
            j_b = (2 * d + 1) * MC_HEADS + hd
            vt = vt_ref[hd * MC_V:(hd + 1) * MC_V, rows]
            b_row = gr_ref[j_b:j_b + 1, rows]
            key_col = gc_ref[rows, j_li:j_li + 1] - gc_ref[rows, j_b:j_b + 1]
            g = b_row[:, last:last + 1]
            dmat = jnp.where(tri, b_row + key_col, NEG)
            a_row = g - b_row + gr_ref[j_li:j_li + 1, rows]
            q = qk_ref[0, rows, hd * MC_QK:(hd + 1) * MC_QK]
            k = qk_ref[0, rows, C_Q + hd * MC_QK:C_Q + (hd + 1) * MC_QK]
            chains.append(dict(
                ci=ci, hd=hd, h_ref=h_ref, rows=rows, q=q, k=k, b_row=b_row, g=g, dmat=dmat, a_row=a_row,
                vext=jnp.concatenate([vt, ones_rows], axis=0),
                kq=_dot_nt(k, q),
                dmax=jnp.max(dmat, axis=0, keepdims=True),
                amax=jnp.max(a_row, axis=1, keepdims=True)))
    return chains


def _mlstm_update(chains, c_scr, m_scr):
    for ch in chains:
        ch["m_prev"] = m_scr[ch["ci"], 0:1, 0:1]
        ch["c_prev"] = c_scr[ch["ci"]]
        ch["cq"] = _dot_nt(ch["c_prev"].astype(MXU_DTYPE), ch["q"])
    for ch in chains:
        m_t = jnp.maximum(ch["b_row"] + ch["m_prev"], ch["dmax"])
        ch["m_t"] = m_t
        ch["inter"] = jnp.exp(ch["b_row"] + ch["m_prev"] - m_t)
        ch["s"] = (ch["kq"] * jnp.exp(ch["dmat"] - m_t)).astype(MXU_DTYPE)
        m_new = jnp.maximum(ch["g"] + ch["m_prev"], ch["amax"])
        ch["m_new"] = m_new
        ch["decay"] = jnp.exp(ch["g"] + ch["m_prev"] - m_new)
        ch["vw"] = (ch["vext"].astype(F32) * jnp.exp(ch["a_row"] - m_new)).astype(MXU_DTYPE)
    for ch in chains:
        ch["vs"] = _dot(ch["vext"], ch["s"])
        ch["dc"] = _dot(ch["vw"], ch["k"])
    for ch in chains:
        nd = ch["inter"] * ch["cq"] + ch["vs"]
        den = nd[MC_V:MC_V + 1, :]
        h_t = nd[0:MC_V, :] / jnp.maximum(jnp.abs(den), jnp.exp(-ch["m_t"]))
        ch["h_ref"][0, ch["rows"], ch["hd"] * MC_V:(ch["hd"] + 1) * MC_V] = h_t.T
        c_scr[ch["ci"]] = ch["decay"] * ch["c_prev"] + ch["dc"]
        m_scr[ch["ci"]] = jnp.broadcast_to(ch["m_new"], m_scr.shape[1:])


def _mlstm_call(qk, vt, gc, gr):
    bsz, lp, _ = qk.shape
    n_sub = MLSTM_CHUNKS_PER_STEP if (lp // BLOCK) % MLSTM_CHUNKS_PER_STEP == 0 else 1
    rows = n_sub * BLOCK
    nch = lp // rows
    fwd = lambda b, t: (b, t, 0)
    bwd = lambda b, t: (b, nch - 1 - t, 0)
    return pl.pallas_call(
        _mlstm_kernel,
        grid=(bsz, nch),
        in_specs=[pl.BlockSpec((1, rows, C_Q + C_K), fwd),
                  pl.BlockSpec((C_V, rows), lambda b, t: (0, b * nch + t)),
                  pl.BlockSpec((rows, LANES), lambda b, t: (b * nch + t, 0)),
                  pl.BlockSpec((C_G, rows), lambda b, t: (0, b * nch + t)),
                  pl.BlockSpec((1, rows, C_Q + C_K), bwd),
                  pl.BlockSpec((C_V, rows), lambda b, t: (0, b * nch + nch - 1 - t)),
                  pl.BlockSpec((rows, LANES), lambda b, t: (b * nch + nch - 1 - t, 0)),
                  pl.BlockSpec((C_G, rows), lambda b, t: (0, b * nch + nch - 1 - t))],
        out_specs=[pl.BlockSpec((1, rows, C_V), fwd),
                   pl.BlockSpec((1, rows, C_V), bwd)],
        out_shape=[jax.ShapeDtypeStruct((bsz, lp, C_V), F32),
                   jax.ShapeDtypeStruct((bsz, lp, C_V), F32)],
        scratch_shapes=[pltpu.VMEM((2 * MC_HEADS, MC_V + MLSTM_EXT, MC_QK), F32),
                        pltpu.VMEM((2 * MC_HEADS, 8, LANES), F32)],
        compiler_params=_params(2),
        name="mlstm_scan",
    )(qk, vt, gc, gr, qk, vt, gc, gr)


def _merge_kernel(h_ref, oa_ref, ob_ref, hf_ref, hb_ref, co_ref, mg_ref, wg_ref, wb_ref, wo_ref,
                  lg_ref, lb_ref, o_ref, *, alpha):
    h = h_ref[...]
    hx = h.astype(MXU_DTYPE)
    hc = hf_ref[...] + hb_ref[...]
    parts = []
    for hd in range(MC_HEADS):
        sl = slice(hd * MC_V, (hd + 1) * MC_V)
        x = hc[:, sl]
        mu = jnp.mean(x, axis=-1, keepdims=True)
        xc = x - mu
        var = jnp.mean(xc * xc, axis=-1, keepdims=True)
        parts.append(xc * lax.rsqrt(var + LN_EPS) * mg_ref[:, sl] * co_ref[:, sl])
    oc = jnp.concatenate(parts, axis=1).astype(MXU_DTYPE)
    branches = (oa_ref[...], ob_ref[...], oc)
    merged = None
    for br in range(N_BRANCH):
        gate = jax.nn.sigmoid(_dot(hx, wg_ref[:, br * D_MODEL:(br + 1) * D_MODEL]))
        term = gate * _dot(branches[br], wb_ref[br])
        merged = term if merged is None else merged + term
    y = _dot(merged.astype(MXU_DTYPE), wo_ref[...])
    o_ref[...] = _layer_norm(alpha * h + y, lg_ref[...], lb_ref[...])


def _merge_call(h, oa, ob, hf, hb, co, mg, wg, wb, wo, lg, lb, alpha, tm):
    m, d = h.shape
    rows = lambda n: pl.BlockSpec((tm, n), lambda i: (i, 0))
    full2 = lambda a: pl.BlockSpec(a.shape, lambda i: (0, 0))
    return pl.pallas_call(
        functools.partial(_merge_kernel, alpha=alpha),
        grid=(m // tm,),
        in_specs=[rows(d), rows(A_V), rows(B_Q), rows(C_V), rows(C_V),
                  pl.BlockSpec((tm, C_O), lambda i: (i, (C_Q + C_K) // C_O)),
                  full2(mg), full2(wg), pl.BlockSpec(wb.shape, lambda i: (0, 0, 0)), full2(wo),
                  full2(lg), full2(lb)],
        out_specs=rows(d),
        out_shape=jax.ShapeDtypeStruct((m, d), F32),
        compiler_params=_params(1),
        name="merge_ln1",
    )(h, oa, ob, hf, hb, co, mg, wg, wb, wo, lg, lb)


def _split3(x):
    hi = x.astype(MXU_DTYPE)
    lo = (x - hi.astype(F32)).astype(MXU_DTYPE)
    return hi, lo


def _router_kernel(h_ref, w_ref, b_ref, rt_ref, rc_ref, *, tm, lp, l_end):
    x_hi, x_lo = _split3(h_ref[...])
    w_hi, w_lo = _split3(w_ref[...])
    logits = (_dot_nt(w_hi, x_hi) + _dot_nt(w_hi, x_lo) + _dot_nt(w_lo, x_hi)) + b_ref[...]
    none = float(N_EXPERTS)
    gl = logits[N_EXPERTS:N_EXPERTS + 8]
    grow = lax.broadcasted_iota(jnp.int32, gl.shape, 0).astype(F32)
    gmax = jnp.max(gl, axis=0, keepdims=True)
    g_sel = jnp.min(jnp.where(gl == gmax, grow, none), axis=0, keepdims=True)
    p_grp = 1.0 / jnp.sum(jnp.exp(gl - gmax), axis=0, keepdims=True)
    el = logits[0:N_EXPERTS]
    erow_i = lax.broadcasted_iota(jnp.int32, el.shape, 0)
    erow = erow_i.astype(F32)
    cand = jnp.where((erow_i // EXP_PER_GROUP).astype(F32) == g_sel, el, -jnp.inf)
    top1 = jnp.max(cand, axis=0, keepdims=True)
    i1 = jnp.min(jnp.where(cand == top1, erow, none), axis=0, keepdims=True)
    cand2 = jnp.where(erow == i1, -jnp.inf, cand)
    top2 = jnp.max(cand2, axis=0, keepdims=True)
    i2 = jnp.min(jnp.where(cand2 == top2, erow, none), axis=0, keepdims=True)
    e = jnp.exp(top2 - top1)
    w1 = (1.0 / (1.0 + e)) * p_grp
    w2 = (e / (1.0 + e)) * p_grp
    pos = lax.broadcasted_iota(jnp.int32, (1, tm), 1) + pl.program_id(0) * tm
    real = ((pos % lp) >= ROW_PAD) & ((pos % lp) < l_end)
    e1 = jnp.where(real, i1, none)
    e2 = jnp.where(real, i2, none)
    r = lax.broadcasted_iota(jnp.int32, (LANES, tm), 0)
    table = jnp.where(r == 0, e1, jnp.where(r == 1, e2, jnp.where(r == 2, w1, jnp.where(r == 3, w2, 0.0))))
    rt_ref[...] = table[0:8]
    rc_ref[...] = table.T


def _router_call(h, w, b, tm, lp, l_end):
    m, d = h.shape
    return pl.pallas_call(
        functools.partial(_router_kernel, tm=tm, lp=lp, l_end=l_end),
        grid=(m // tm,),
        in_specs=[pl.BlockSpec((tm, d), lambda i: (i, 0)),
                  pl.BlockSpec((LANES, d), lambda i: (0, 0)),
                  pl.BlockSpec((LANES, 1), lambda i: (0, 0))],
        out_specs=[pl.BlockSpec((8, tm), lambda i: (0, i)),
                   pl.BlockSpec((tm, LANES), lambda i: (i, 0))],
        out_shape=[jax.ShapeDtypeStruct((8, m), F32),
                   jax.ShapeDtypeStruct((m, LANES), F32)],
        compiler_params=_params(1),
        name="moe_router",
    )(h, w, b)


def _rank_kernel(rt_ref, rk_ref, cnt_ref, carry, *, tm):
    @pl.when(pl.program_id(0) == 0)
    def _():
        carry[...] = jnp.zeros_like(carry)

    erow = lax.broadcasted_iota(jnp.int32, (N_EXPERTS, tm), 0).astype(F32)
    oh1 = jnp.where(erow == rt_ref[0:1, :], 1.0, 0.0)
    oh2 = jnp.where(erow == rt_ref[1:2, :], 1.0, 0.0)
    oh = oh1 + oh2
    earlier = (lax.broadcasted_iota(jnp.int32, (tm, tm), 0)
               < lax.broadcasted_iota(jnp.int32, (tm, tm), 1))
    before = _dot(oh.astype(MXU_DTYPE), jnp.where(earlier, 1.0, 0.0).astype(MXU_DTYPE)) + carry[:, 0:1]
    r1 = jnp.sum(oh1 * before, axis=0, keepdims=True)
    r2 = jnp.sum(oh2 * before, axis=0, keepdims=True)
    r = lax.broadcasted_iota(jnp.int32, (8, tm), 0)
    rk_ref[...] = jnp.where(r == 0, r1, jnp.where(r == 1, r2, 0.0))
    total = carry[...] + jnp.sum(oh, axis=1, keepdims=True)
    carry[...] = total
    cnt_ref[...] = total


def _rank_call(rt, tm):
    m = rt.shape[1]
    return pl.pallas_call(
        functools.partial(_rank_kernel, tm=tm),
        grid=(m // tm,),
        in_specs=[pl.BlockSpec((8, tm), lambda i: (0, i))],
        out_specs=[pl.BlockSpec((8, tm), lambda i: (0, i)),
                   pl.BlockSpec((N_EXPERTS, LANES), lambda i: (0, 0))],
        out_shape=[jax.ShapeDtypeStruct((8, m), F32),
                   jax.ShapeDtypeStruct((N_EXPERTS, LANES), F32)],
        scratch_shapes=[pltpu.VMEM((N_EXPERTS, LANES), F32)],
        compiler_params=_params(1),
        name="moe_rank",
    )(rt)


def _dispatch_kernel(dest_ref, h_ref, xs_in_ref, xs_ref, sem, *, tm, m):
    del xs_in_ref
    base = pl.program_id(0) * tm

    def body(r, c):
        for k in range(2):
            d = dest_ref[k * m + base + r]
            pltpu.make_async_copy(h_ref.at[pl.ds(r, 1)], xs_ref.at[pl.ds(d, 1)], sem).start()
        return c

    lax.fori_loop(0, tm, body, 0, unroll=DMA_ISSUE_UNROLL)
    for _ in range(2):
        pltpu.make_async_copy(h_ref, xs_ref.at[pl.ds(0, tm)], sem).wait()


def _dispatch_call(dest, h, xs0, tm):
    m, d = h.shape
    return pl.pallas_call(
        functools.partial(_dispatch_kernel, tm=tm, m=m),
        grid_spec=pltpu.PrefetchScalarGridSpec(
            num_scalar_prefetch=1,
            grid=(m // tm,),
            in_specs=[pl.BlockSpec((tm, d), lambda i, dest_: (i, 0)),
                      pl.BlockSpec(memory_space=pl.ANY)],
            out_specs=pl.BlockSpec(memory_space=pl.ANY),
            scratch_shapes=[pltpu.SemaphoreType.DMA(())]),
        out_shape=jax.ShapeDtypeStruct(xs0.shape, xs0.dtype),
        input_output_aliases={2: 0},
        compiler_params=_params(1),
        name="moe_dispatch",
    )(dest, h, xs0)


def _ffn_kernel(be_ref, nu_ref, xs_ref, wg_ref, wu_ref, wd_ref, ys_ref, wg_s, wu_s, wd_s):
    i = pl.program_id(0)
    new_expert = jnp.logical_or(i == 0, be_ref[i] != be_ref[jnp.maximum(i - 1, 0)])

    @pl.when(new_expert)
    def _():
        wg_s[...] = wg_ref[0, 0].astype(MXU_DTYPE)
        wu_s[...] = wu_ref[0, 0].astype(MXU_DTYPE)
        wd_s[...] = wd_ref[0, 0].astype(MXU_DTYPE)

    @pl.when(i < nu_ref[0])
    def _():
        xb = xs_ref[...].astype(MXU_DTYPE)
        act = jax.nn.silu(_dot(xb, wg_s[...])) * _dot(xb, wu_s[...])
        ys_ref[...] = _dot(act.astype(MXU_DTYPE), wd_s[...])

    @pl.when(i >= nu_ref[0])
    def _():
        ys_ref[...] = jnp.zeros_like(ys_ref)


def _ffn_call(block_e, n_used, xs, wg, wu, wd, layer, n_blocks):
    d = xs.shape[1]
    br = EXPERT_ROWS
    rows = lambda i, be, nu: (jnp.minimum(i, nu[0] - 1), 0)
    return pl.pallas_call(
        _ffn_kernel,
        grid_spec=pltpu.PrefetchScalarGridSpec(
            num_scalar_prefetch=2,
            grid=(n_blocks,),
            in_specs=[pl.BlockSpec((br, d), rows),
                      pl.BlockSpec((1, 1, d, D_EXPERT), lambda i, be, nu: (layer, be[i], 0, 0)),
                      pl.BlockSpec((1, 1, d, D_EXPERT), lambda i, be, nu: (layer, be[i], 0, 0)),
                      pl.BlockSpec((1, 1, D_EXPERT, d), lambda i, be, nu: (layer, be[i], 0, 0))],
            out_specs=pl.BlockSpec((br, d), lambda i, be, nu: (i, 0)),
            scratch_shapes=[pltpu.VMEM((d, D_EXPERT), MXU_DTYPE),
                            pltpu.VMEM((d, D_EXPERT), MXU_DTYPE),
                            pltpu.VMEM((D_EXPERT, d), MXU_DTYPE)]),
        out_shape=jax.ShapeDtypeStruct((n_blocks * br, d), F32),
        compiler_params=_params(1),
        name="moe_experts",
    )(block_e, n_used, xs, wg, wu, wd)


def _combine_kernel(src_ref, h_ref, rc_ref, lg_ref, lb_ref, ys_ref, o_ref, buf, sem, *, tm, m, alpha):
    i = pl.program_id(0)

    def issue(tile, slot):
        base = tile * tm

        def body(r, c):
            for k in range(2):
                s = src_ref[k * m + base + r]
                pltpu.make_async_copy(ys_ref.at[pl.ds(s, 1)], buf.at[slot, k, pl.ds(r, 1)], sem.at[slot]).start()
            return c

        lax.fori_loop(0, tm, body, 0, unroll=DMA_ISSUE_UNROLL)

    @pl.when(i == 0)
    def _():
        issue(0, 0)

    @pl.when(i + 1 < m // tm)
    def _():
        issue(i + 1, (i + 1) % 2)

    slot = i % 2
    for k in range(2):
        pltpu.make_async_copy(ys_ref.at[pl.ds(0, tm)], buf.at[slot, k], sem.at[slot]).wait()
    rc = rc_ref[...]
    real = rc[:, 0:1] < float(N_EXPERTS)
    y = jnp.where(real, rc[:, 2:3] * buf[slot, 0] + rc[:, 3:4] * buf[slot, 1], 0.0)
    o_ref[...] = _layer_norm(alpha * h_ref[...] + y, lg_ref[...], lb_ref[...])


def _combine_call(src, h, rc, lg, lb, ys, alpha, tm):
    m, d = h.shape
    n_tiles = m // tm
    return pl.pallas_call(
        functools.partial(_combine_kernel, tm=tm, m=m, alpha=alpha),
        grid_spec=pltpu.PrefetchScalarGridSpec(
            num_scalar_prefetch=1,
            grid=(n_tiles,),
            in_specs=[pl.BlockSpec((tm, d), lambda i, s: (i, 0)),
                      pl.BlockSpec((tm, LANES), lambda i, s: (i, 0)),
                      pl.BlockSpec((1, d), lambda i, s: (0, 0)),
                      pl.BlockSpec((1, d), lambda i, s: (0, 0)),
                      pl.BlockSpec(memory_space=pl.ANY)],
            out_specs=pl.BlockSpec((tm, d), lambda i, s: (i, 0)),
            scratch_shapes=[pltpu.VMEM((2, 2, tm, d), F32),
                            pltpu.SemaphoreType.DMA((2,))]),
        out_shape=jax.ShapeDtypeStruct((m, d), F32),
        compiler_params=_params(1),
        name="moe_combine_ln2",
    )(src, h, rc, lg, lb, ys)


def _moe(h1, w_rg, b_rg, w_re, b_re, w_gate, w_up, w_down, layer, lg, lb, alpha, bsz, lp, l_end, tm, slot_buf):
    m, d = h1.shape
    wr = jnp.zeros((LANES, d), F32).at[0:N_EXPERTS].set(w_re.T).at[N_EXPERTS:N_EXPERTS + N_GROUPS].set(w_rg.T)
    br_ = jnp.zeros((LANES,), F32).at[0:N_EXPERTS].set(b_re).at[N_EXPERTS:N_EXPERTS + N_GROUPS].set(b_rg)
    br_ = br_.at[N_EXPERTS + N_GROUPS:N_EXPERTS + 8].set(NEG).reshape(LANES, 1)
    rt, rc = _router_call(h1, wr, br_, tm, lp, l_end)
    rk, cnt = _rank_call(rt, tm)

    rows = EXPERT_ROWS
    n_assign = 2 * bsz * (l_end - ROW_PAD)
    n_unused = lp - (l_end - ROW_PAD)
    n_blocks = -(-(n_assign + N_EXPERTS * (rows - 1)) // rows)
    n_slots = n_blocks * rows
    counts = cnt[:, 0].astype(jnp.int32)
    pcounts = (counts + rows - 1) // rows * rows
    pend = jnp.cumsum(pcounts)
    pstart = pend - pcounts
    e = rt[0:2].astype(jnp.int32)
    rank = rk[0:2].astype(jnp.int32)
    real = e < N_EXPERTS
    expert_ids = jnp.arange(N_EXPERTS, dtype=jnp.int32)[:, None, None]
    slot = jnp.sum(jnp.where(e[None] == expert_ids, pstart[:, None, None], 0), axis=0) + rank
    tok = jnp.arange(m, dtype=jnp.int32)
    pos = tok % lp
    unused_idx = (tok // lp) * n_unused + jnp.where(pos < ROW_PAD, pos, pos - l_end + ROW_PAD)
    spare = n_slots + 2 * unused_idx[None, :] + jnp.arange(2, dtype=jnp.int32)[:, None]
    dest = jnp.where(real, slot, spare).reshape(-1)
    src = jnp.where(real, slot, 0).reshape(-1)
    block_start = jnp.arange(n_blocks, dtype=jnp.int32) * rows
    block_e = jnp.minimum(jnp.sum((pend[None, :] <= block_start[:, None]).astype(jnp.int32), axis=1),
                          N_EXPERTS - 1)
    n_used = (pend[-1:] // rows).astype(jnp.int32)

    n_spare = -(-(2 * bsz * n_unused) // rows) * rows
    if slot_buf is None:
        slot_buf = jnp.zeros((n_slots + n_spare, d), F32)
    xs = _dispatch_call(dest, h1, slot_buf, tm)
    ys = _ffn_call(block_e, n_used, xs, w_gate, w_up, w_down, layer, n_blocks)
    return _combine_call(src, h1, rc, lg, lb, ys, alpha, _token_tile(m, 384)), xs


def _rope_tables(lp):
    pos = jnp.arange(lp, dtype=F32) - float(ROW_PAD)
    inv = 1.0 / (ROPE_THETA ** (jnp.arange(0, DA_DIM, 2, dtype=F32) / DA_DIM))
    ang = pos[:, None] * inv[None, :]
    reps = LANES // (DA_DIM // 2)
    sign = jnp.tile(jnp.concatenate([-jnp.ones((DA_DIM // 2,), F32), jnp.ones((DA_DIM // 2,), F32)]), LANES // DA_DIM)
    return jnp.tile(jnp.cos(ang), (1, reps)), jnp.tile(jnp.sin(ang), (1, reps)) * sign[None, :]


def _dup_heads(w, n_heads, dim):
    d = w.shape[0]
    return jnp.broadcast_to(w.reshape(d, n_heads, 1, dim), (d, n_heads, 2, dim)).reshape(d, n_heads * 2 * dim)


def kernel(x, meta, ln_in_g, ln_in_b, w_in, conv_w, conv_b, gate_b, lam_q1, lam_k1, lam_q2, lam_k2, diff_g, sink, mlstm_g, w_branch, w_out, ln1_g, ln1_b, ln2_g, ln2_b, w_rg, b_rg, w_re, b_re, w_gate, w_up, w_down):
    bsz, seq, d = x.shape
    depth = w_in.shape[0]
    assert seq % BLOCK == 0 and d == D_MODEL
    l_end = seq + BLOCK
    lp = -(-l_end // MXU_TILE) * MXU_TILE
    m = bsz * lp
    alpha = (2.0 * depth) ** 0.25
    tm = _row_tile(lp, 768)
    tk_attn = _row_tile(lp, 768)

    hp = jnp.concatenate([jnp.zeros((bsz, ROW_PAD, d), x.dtype),
                          jnp.broadcast_to(meta.astype(x.dtype)[None], (bsz, N_META_TOK, d)), x,
                          jnp.zeros((bsz, lp - l_end, d), x.dtype)], axis=1)
    h = _ln_call(hp.reshape(m, d), ln_in_g, ln_in_b, tm)
    cos, sin = _rope_tables(lp)
    q_scale = DA_DIM ** -0.5
    rope_scale = jnp.concatenate([jnp.full((A_Q,), q_scale * math.log2(math.e), F32), jnp.ones((A_K,), F32),
                                  jnp.full((B_Q,), q_scale, F32), jnp.ones((2 * B_K,), F32)]).reshape(1, -1)
    conv_scale = jnp.concatenate([jnp.ones((C_Q,), F32), jnp.full((C_K,), MC_QK ** -0.5, F32)]).reshape(1, -1)

    slot_buf = None
    for l in range(depth):
        lam_init = 0.8 - 0.6 * math.exp(-0.3 * l)
        wl = w_in[l]
        col = lambda i: wl[:, OFFS[i]:OFFS[i + 1]]
        w_rope = jnp.concatenate([col(0), col(1), col(3), _dup_heads(col(4), WB_KV, WB_DIM),
                                  _dup_heads(col(5), WB_KV, WB_DIM)], axis=1).astype(MXU_DTYPE)
        w_vt = jnp.concatenate([col(8), col(2)], axis=1).T.astype(MXU_DTYPE)
        w_conv = jnp.concatenate([col(6), col(7), col(9)], axis=1).astype(MXU_DTYPE)
        w_g = jnp.pad(col(10), ((0, 0), (0, LANES - C_G))).astype(MXU_DTYPE)
        b_g = jnp.pad(gate_b[l], (0, LANES - C_G)).reshape(1, LANES)
        w_mg = col(11).astype(MXU_DTYPE)

        rq = _proj_rope_call(h, w_rope, cos, sin, rope_scale, tm, lp).reshape(bsz, lp, -1)
        vt = _proj_t_call(h, w_vt, tm, "proj_val_t")
        zco = _proj_call(h, w_conv, tm, F32, C_Q + C_K, "proj_conv_gate")
        gc, gr = _gates_call(h, w_g, b_g, tm, lp, l_end)

        lamv = jnp.stack([lam_q1[l], lam_k1[l], lam_q2[l], lam_k2[l]])
        out_a = _attn_a_call(rq, vt, lamv, diff_g[l].reshape(-1, 1), lam_init, l_end, MXU_TILE, tk_attn)
        out_b = _attn_b_call(rq, sink[l].reshape(1, -1), l_end)
        qk = _conv_call(zco.reshape(bsz, lp, -1), conv_w[l], conv_b[l].reshape(1, -1), conv_scale, l_end,
                        _row_tile(lp, 768))
        h_f, h_b = _mlstm_call(qk, vt, gc, gr)

        h = _merge_call(h, out_a.reshape(m, -1), out_b.reshape(m, -1), h_f.reshape(m, -1), h_b.reshape(m, -1),
                        zco, mlstm_g[l].reshape(1, -1), w_mg, w_branch[l].astype(MXU_DTYPE),
                        w_out[l].astype(MXU_DTYPE), ln1_g[l].reshape(1, -1), ln1_b[l].reshape(1, -1),
                        alpha, _token_tile(m, 384))
        h, slot_buf = _moe(h, w_rg[l], b_rg[l], w_re[l], b_re[l], w_gate, w_up, w_down, l,
                           ln2_g[l].reshape(1, -1), ln2_b[l].reshape(1, -1), alpha, bsz, lp, l_end, tm, slot_buf)
    return h.reshape(bsz, lp, d)[:, BLOCK:l_end]
```

```python
import functools
import math

import numpy as np
import jax
import jax.numpy as jnp
from jax import lax
from jax.experimental import pallas as pl
from jax.experimental.pallas import tpu as pltpu

D_MODEL = 1024
N_META_TOK = 16
BLOCK = 128
ROW_PAD = BLOCK - N_META_TOK
ROPE_THETA = 10000.0
LN_EPS = 1e-5
NEG = -1e30

DA_HEADS = 4
DA_DIM = 64
WB_HEADS = 8
WB_KV = 2
WB_DIM = 64
WINDOW = 128
MC_HEADS = 4
MC_QK = 128
MC_V = 128
N_BRANCH = 3
BRANCH_W = 512
N_GROUPS = 4
EXP_PER_GROUP = 8
N_EXPERTS = N_GROUPS * EXP_PER_GROUP
D_EXPERT = 512

A_Q = DA_HEADS * 2 * DA_DIM
A_K = A_Q
A_V = A_Q
B_Q = WB_HEADS * WB_DIM
B_K = WB_KV * WB_DIM
B_V = B_K
C_Q = MC_HEADS * MC_QK
C_K = C_Q
C_V = MC_HEADS * MC_V
C_O = C_V
C_G = 4 * MC_HEADS
GATE_W = N_BRANCH * D_MODEL
SPLITS = (A_Q, A_K, A_V, B_Q, B_K, B_V, C_Q, C_K, C_V, C_O, C_G, GATE_W)
OFFS = tuple(int(v) for v in np.cumsum((0,) + SPLITS))

LANES = 128
MXU_TILE = 256
EXPERT_ROWS = 512
WINDOW_BLOCKS_PER_STEP = 6
MLSTM_CHUNKS_PER_STEP = 6
MLSTM_EXT = 16
DMA_ISSUE_UNROLL = 8
ATTN_Q_TILES = 3
ATTN_UNROLL = 2
ATTN_EXT = 16
VMEM_LIMIT = 56 * 1024 * 1024

F32 = jnp.float32
MXU_DTYPE = jnp.bfloat16


def _dot(a, b):
    return jnp.dot(a, b, preferred_element_type=F32)


def _dot_nt(a, b):
    return lax.dot_general(a, b, (((1,), (1,)), ((), ())), preferred_element_type=F32)


def _dot_tn(a, b):
    return lax.dot_general(a, b, (((0,), (0,)), ((), ())), preferred_element_type=F32)


def _params(n_axes, flags=None):
    return pltpu.CompilerParams(dimension_semantics=("arbitrary",) * n_axes,
                                vmem_limit_bytes=VMEM_LIMIT, flags=flags)


def _row_tile(n_rows, target):
    best = BLOCK
    for t in range(BLOCK, target + 1, BLOCK):
        if n_rows % t == 0:
            best = t
    return best


def _token_tile(n_rows, target):
    best = 8
    for t in range(8, target + 1, 8):
        if n_rows % t == 0:
            best = t
    return best


def _layer_norm(x, g, b):
    mu = jnp.mean(x, axis=-1, keepdims=True)
    xc = x - mu
    var = jnp.mean(xc * xc, axis=-1, keepdims=True)
    return xc * lax.rsqrt(var + LN_EPS) * g + b


def _ln_kernel(x_ref, g_ref, b_ref, o_ref):
    o_ref[...] = _layer_norm(x_ref[...], g_ref[...], b_ref[...])


def _ln_call(x, g, b, tm):
    m, d = x.shape
    return pl.pallas_call(
        _ln_kernel,
        grid=(m // tm,),
        in_specs=[pl.BlockSpec((tm, d), lambda i: (i, 0)),
                  pl.BlockSpec((1, d), lambda i: (0, 0)),
                  pl.BlockSpec((1, d), lambda i: (0, 0))],
        out_specs=pl.BlockSpec((tm, d), lambda i: (i, 0)),
        out_shape=jax.ShapeDtypeStruct((m, d), F32),
        compiler_params=_params(1),
        name="ln_in",
    )(x, g.reshape(1, d), b.reshape(1, d))


def _proj_rope_kernel(x_ref, w_ref, cos_ref, sin_ref, scale_ref, o_ref, *, n_rope):
    z = _dot(x_ref[...].astype(MXU_DTYPE), w_ref[...])
    cos = cos_ref[...]
    sin = sin_ref[...]
    lane = lax.broadcasted_iota(jnp.int32, cos.shape, 1)
    first_half = (lane % DA_DIM) < (DA_DIM // 2)
    for c in range(n_rope // LANES):
        sl = slice(c * LANES, (c + 1) * LANES)
        zc = z[:, sl]
        partner = jnp.where(first_half, pltpu.roll(zc, LANES - DA_DIM // 2, 1),
                            pltpu.roll(zc, DA_DIM // 2, 1))
        o_ref[:, sl] = ((zc * cos + partner * sin) * scale_ref[:, sl]).astype(o_ref.dtype)
    o_ref[:, n_rope:] = z[:, n_rope:].astype(o_ref.dtype)


def _proj_rope_call(h, w, cos, sin, scale, tm, lp):
    m, d = h.shape
    n = w.shape[1]
    per_batch = lp // tm
    return pl.pallas_call(
        functools.partial(_proj_rope_kernel, n_rope=scale.shape[1]),
        grid=(m // tm,),
        in_specs=[pl.BlockSpec((tm, d), lambda i: (i, 0)),
                  pl.BlockSpec((d, n), lambda i: (0, 0)),
                  pl.BlockSpec((tm, LANES), lambda i: (i % per_batch, 0)),
                  pl.BlockSpec((tm, LANES), lambda i: (i % per_batch, 0)),
                  pl.BlockSpec(scale.shape, lambda i: (0, 0))],
        out_specs=pl.BlockSpec((tm, n), lambda i: (i, 0)),
        out_shape=jax.ShapeDtypeStruct((m, n), MXU_DTYPE),
        compiler_params=_params(1),
        name="proj_rope",
    )(h, w, cos, sin, scale)


def _proj_kernel(x_ref, w_ref, o_ref, *, sigmoid_from):
    z = _dot(x_ref[...].astype(MXU_DTYPE), w_ref[...])
    if sigmoid_from is None:
        o_ref[...] = z.astype(o_ref.dtype)
    else:
        o_ref[:, :sigmoid_from] = z[:, :sigmoid_from].astype(o_ref.dtype)
        o_ref[:, sigmoid_from:] = jax.nn.sigmoid(z[:, sigmoid_from:]).astype(o_ref.dtype)


def _proj_call(h, w, tm, out_dtype, sigmoid_from, name):
    m, d = h.shape
    n = w.shape[1]
    return pl.pallas_call(
        functools.partial(_proj_kernel, sigmoid_from=sigmoid_from),
        grid=(m // tm,),
        in_specs=[pl.BlockSpec((tm, d), lambda i: (i, 0)),
                  pl.BlockSpec((d, n), lambda i: (0, 0))],
        out_specs=pl.BlockSpec((tm, n), lambda i: (i, 0)),
        out_shape=jax.ShapeDtypeStruct((m, n), out_dtype),
        compiler_params=_params(1),
        name=name,
    )(h, w)


def _proj_t_kernel(x_ref, wt_ref, o_ref):
    o_ref[...] = _dot_nt(wt_ref[...], x_ref[...].astype(MXU_DTYPE)).astype(o_ref.dtype)


def _proj_t_call(h, wt, tm, name):
    m, d = h.shape
    n = wt.shape[0]
    return pl.pallas_call(
        _proj_t_kernel,
        grid=(m // tm,),
        in_specs=[pl.BlockSpec((tm, d), lambda i: (i, 0)),
                  pl.BlockSpec((n, d), lambda i: (0, 0))],
        out_specs=pl.BlockSpec((n, tm), lambda i: (0, i)),
        out_shape=jax.ShapeDtypeStruct((n, m), MXU_DTYPE),
        compiler_params=_params(1),
        name=name,
    )(h, wt)


def _gates_kernel(x_ref, w_ref, b_ref, gc_ref, gr_ref, *, tm, lp, l_end):
    z = _dot(x_ref[...].astype(MXU_DTYPE), w_ref[...]) + b_ref[...]
    lane = lax.broadcasted_iota(jnp.int32, (tm, LANES), 1)
    kind = lane // MC_HEADS
    row = lax.broadcasted_iota(jnp.int32, (tm, LANES), 0) + pl.program_id(0) * tm
    pos = row % lp
    unused = (pos < ROW_PAD) | (pos >= l_end)
    log_f = jnp.minimum(z, 0.0) - jnp.log1p(jnp.exp(-jnp.abs(z)))
    is_forget = (kind % 2) == 1
    base = jnp.where(is_forget, jnp.where(unused, 0.0, log_f), jnp.where(unused, NEG, z))
    r128 = lax.broadcasted_iota(jnp.int32, (BLOCK, LANES), 0)
    fwd_lane = lax.broadcasted_iota(jnp.int32, (BLOCK, LANES), 1) // MC_HEADS == 1
    forget128 = (lax.broadcasted_iota(jnp.int32, (BLOCK, LANES), 1) // MC_HEADS) % 2 == 1
    for c in range(tm // BLOCK):
        x = base[c * BLOCK:(c + 1) * BLOCK]
        pre = x
        suf = x
        s = 1
        while s < BLOCK:
            pre = pre + jnp.where(r128 >= s, pltpu.roll(pre, s, 0), 0.0)
            suf = suf + jnp.where(r128 < BLOCK - s, pltpu.roll(suf, BLOCK - s, 0), 0.0)
            s *= 2
        out = jnp.where(forget128, jnp.where(fwd_lane, pre, suf), x)
        gc_ref[c * BLOCK:(c + 1) * BLOCK, :] = out
        gr_ref[:, c * BLOCK:(c + 1) * BLOCK] = out.T[0:C_G, :]


def _gates_call(h, w, b, tm, lp, l_end):
    m, d = h.shape
    return pl.pallas_call(
        functools.partial(_gates_kernel, tm=tm, lp=lp, l_end=l_end),
        grid=(m // tm,),
        in_specs=[pl.BlockSpec((tm, d), lambda i: (i, 0)),
                  pl.BlockSpec((d, LANES), lambda i: (0, 0)),
                  pl.BlockSpec((1, LANES), lambda i: (0, 0))],
        out_specs=[pl.BlockSpec((tm, LANES), lambda i: (i, 0)),
                   pl.BlockSpec((C_G, tm), lambda i: (0, i))],
        out_shape=[jax.ShapeDtypeStruct((m, LANES), F32),
                   jax.ShapeDtypeStruct((C_G, m), F32)],
        compiler_params=_params(1),
        name="mlstm_gates",
    )(h, w, b)


def _conv_kernel(z_ref, w_ref, b_ref, scale_ref, o_ref, *, lp, l_end, tr):
    w0 = w_ref[0:1, :]
    w1 = w_ref[1:2, :]
    w2 = w_ref[2:3, :]
    row = lax.broadcasted_iota(jnp.int32, (tr, LANES), 0)
    for c in range(lp // tr):
        r0 = c * tr
        zc = z_ref[0, r0:r0 + tr, :]
        before = jnp.zeros((1, LANES), F32) if r0 == 0 else z_ref[0, r0 - 1:r0, :]
        after = jnp.zeros((1, LANES), F32) if r0 + tr == lp else z_ref[0, r0 + tr:r0 + tr + 1, :]
        prev = jnp.where(row == 0, before, pltpu.roll(zc, 1, 0))
        nxt = jnp.where(row == tr - 1, after, pltpu.roll(zc, tr - 1, 0))
        if r0 <= ROW_PAD < r0 + tr:
            prev = jnp.where(row == ROW_PAD - r0, 0.0, prev)
        if r0 <= l_end - 1 < r0 + tr:
            nxt = jnp.where(row == l_end - 1 - r0, 0.0, nxt)
        y = prev * w0 + zc * w1 + nxt * w2 + b_ref[...]
        o_ref[0, r0:r0 + tr, :] = (jax.nn.silu(y) * scale_ref[...]).astype(o_ref.dtype)


def _conv_call(z, w, b, scale, l_end, tr):
    bsz, lp, _ = z.shape
    n = w.shape[1]
    return pl.pallas_call(
        functools.partial(_conv_kernel, lp=lp, l_end=l_end, tr=tr),
        grid=(bsz, n // LANES),
        in_specs=[pl.BlockSpec((1, lp, LANES), lambda b_, j: (b_, 0, j)),
                  pl.BlockSpec((3, LANES), lambda b_, j: (0, j)),
                  pl.BlockSpec((1, LANES), lambda b_, j: (0, j)),
                  pl.BlockSpec((1, LANES), lambda b_, j: (0, j))],
        out_specs=pl.BlockSpec((1, lp, LANES), lambda b_, j: (b_, 0, j)),
        out_shape=jax.ShapeDtypeStruct((bsz, lp, n), MXU_DTYPE),
        compiler_params=_params(2),
        name="mlstm_conv",
    )(z, w, b, scale)


def _attn_a_kernel(lamv_ref, g_ref, q_ref, k_ref, vt_ref, o_ref, s0_scr, s1_scr, acc_scr, *,
                   tk, n_chunks, l_end, lam_init):
    n_ch = acc_scr.shape[0]
    tq = acc_scr.shape[2]
    feat = lax.broadcasted_iota(jnp.int32, (2 * DA_DIM, tq), 0)
    key_row = lax.broadcasted_iota(jnp.int32, (tk, tq), 0)
    qz = []
    for t in range(n_ch // 2):
        qt = q_ref[0, t * tq:(t + 1) * tq, :].astype(F32).T.astype(MXU_DTYPE)
        zero = jnp.zeros_like(qt)
        qz += [jnp.where(feat < DA_DIM, qt, zero), jnp.where(feat >= DA_DIM, qt, zero)]
    s_bufs = (s0_scr, s1_scr)

    last = n_chunks - 1
    last_hi = l_end - last * tk

    def keys(j):
        return k_ref[0, pl.ds(pl.multiple_of(j * tk, tk), tk), :]

    def values(j):
        vt = vt_ref[:, pl.ds(pl.multiple_of(j * tk, tk), tk)]
        return jnp.concatenate([vt, jnp.ones((ATTN_EXT, tk), MXU_DTYPE)], axis=0)

    def score_chain(j, kj, slot, c):
        s = _dot(kj, qz[c])
        if isinstance(j, int) and j == 0:
            s = jnp.where(key_row >= ROW_PAD, s, NEG)
        if isinstance(j, int) and j == last and last_hi < tk:
            s = jnp.where(key_row < last_hi, s, NEG)
        s_bufs[slot][c] = s
        return jnp.max(s, axis=0, keepdims=True)

    def softmax_chain(vt, slot, c, m, cmax):
        m_new = jnp.maximum(m, cmax)
        alpha = jnp.exp2(m - m_new)
        p = jnp.exp2((s_bufs[slot][c] - m_new).astype(MXU_DTYPE))
        acc_scr[c] = alpha * acc_scr[c] + _dot(vt, p)
        return m_new

    def scores(j, slot):
        kj = keys(j)
        return tuple(score_chain(j, kj, slot, c) for c in range(n_ch))

    def softmax_values(j, slot, stats, cmax):
        vt = values(j)
        return tuple(softmax_chain(vt, slot, c, stats[c], cmax[c]) for c in range(n_ch))

    one = jnp.full((1, tq), NEG, F32)
    acc_scr[...] = jnp.zeros_like(acc_scr)
    cmax = scores(0, 0)
    stats = (one,) * n_ch
    if n_chunks > 1:
        def step(j, parity, state):
            stats, cmax = state
            kj, vt = keys(j + 1), values(j)
            new_stats, nxt = [], []
            for c in range(n_ch):
                nxt.append(score_chain(j + 1, kj, 1 - parity, c))
                new_stats.append(softmax_chain(vt, parity, c, stats[c], cmax[c]))
            return tuple(new_stats), tuple(nxt)

        def trip(i, st):
            for u in range(ATTN_UNROLL):
                st = step(1 + ATTN_UNROLL * i + u, (1 + u) % 2, st)
            return st

        state = step(0, 0, (stats, cmax))
        n_trips = (last - 2) // ATTN_UNROLL if last >= 2 else 0
        state = lax.fori_loop(0, n_trips, trip, state)
        for j in range(1 + n_trips * ATTN_UNROLL, last):
            state = step(j, j % 2, state)
        stats, cmax = state
    softmax_values(last, last % 2, stats, cmax)
    dv = 2 * DA_DIM
    lv = lamv_ref[...]
    lam = (jnp.exp(jnp.sum(lv[0:1] * lv[1:2], axis=-1, keepdims=True))
           - jnp.exp(jnp.sum(lv[2:3] * lv[3:4], axis=-1, keepdims=True)) + lam_init)
    for t in range(n_ch // 2):
        o0 = acc_scr[2 * t, 0:dv, :] / acc_scr[2 * t, dv:dv + 1, :]
        o1 = acc_scr[2 * t + 1, 0:dv, :] / acc_scr[2 * t + 1, dv:dv + 1, :]
        o = o0 - lam * o1
        ms = jnp.mean(o * o, axis=0, keepdims=True)
        o = o * lax.rsqrt(ms + LN_EPS) * g_ref[...] * (1.0 - lam_init)
        o_ref[0, t * tq:(t + 1) * tq, :] = o.T.astype(o_ref.dtype)


def _attn_a_call(rq, vt, lamv, g_col, lam_init, l_end, tq, tk):
    bsz, lp, _ = rq.shape
    k_blk = A_Q // LANES
    vt_blk = C_V // LANES
    n_q = ATTN_Q_TILES if (lp // tq) % ATTN_Q_TILES == 0 else 1
    n_ch = 2 * n_q
    return pl.pallas_call(
        functools.partial(_attn_a_kernel, tk=tk, n_chunks=lp // tk, l_end=l_end, lam_init=lam_init),
        grid=(bsz, DA_HEADS, lp // (n_q * tq)),
        in_specs=[pl.BlockSpec((4, DA_DIM), lambda b, h, i: (0, 0)),
                  pl.BlockSpec((2 * DA_DIM, 1), lambda b, h, i: (0, 0)),
                  pl.BlockSpec((1, n_q * tq, LANES), lambda b, h, i: (b, i, h)),
                  pl.BlockSpec((1, lp, LANES), lambda b, h, i: (b, 0, k_blk + h)),
                  pl.BlockSpec((2 * DA_DIM, lp), lambda b, h, i: (vt_blk + h, b))],
        out_specs=pl.BlockSpec((1, n_q * tq, LANES), lambda b, h, i: (b, i, h)),
        out_shape=jax.ShapeDtypeStruct((bsz, lp, A_V), MXU_DTYPE),
        scratch_shapes=[pltpu.VMEM((n_ch, tk, tq), F32),
                        pltpu.VMEM((n_ch, tk, tq), F32),
                        pltpu.VMEM((n_ch, 2 * DA_DIM + ATTN_EXT, tq), F32)],
        compiler_params=_params(3),
        name="diff_attn",
    )(lamv, g_col, rq, rq, vt)


def _attn_b_kernel(sink_ref, band_ref, q_ref, k0_ref, k1_ref, v0_ref, v1_ref, o_ref, *, lp, l_end):
    n_sub = q_ref.shape[1] // BLOCK
    for sub in range(n_sub):
        _attn_b_block(sink_ref, band_ref, q_ref, k0_ref, k1_ref, v0_ref, v1_ref, o_ref,
                      pl.program_id(1) * n_sub + sub, slice(sub * BLOCK, (sub + 1) * BLOCK), lp, l_end)


def _attn_b_block(sink_ref, band_ref, q_ref, k0_ref, k1_ref, v0_ref, v1_ref, o_ref, n, rows, lp, l_end):
    nb = lp // BLOCK
    grp = WB_HEADS // WB_KV

    def blocks(ref):
        parts = [ref[0, 0:BLOCK, :]]
        for d in (-1, 0, 1):
            idx = jnp.clip(n + d, 0, nb - 1)
            parts.append(ref[0, pl.ds(pl.multiple_of(idx * BLOCK, BLOCK), BLOCK), :])
        return jnp.concatenate(parts, axis=0)

    keys = (blocks(k0_ref), blocks(k1_ref))
    vals = (blocks(v0_ref), blocks(v1_ref))
    blk_bias = []
    for d in (-1, 0, 1):
        inside = jnp.logical_and(n + d >= 1, n + d <= l_end // BLOCK - 1)
        blk_bias.append(jnp.where(inside, 0.0, NEG))
    head_of_col = lax.broadcasted_iota(jnp.int32, (1, grp * BLOCK), 1) // BLOCK
    lane = lax.broadcasted_iota(jnp.int32, (BLOCK, LANES), 1)
    sinks, raw = [], []
    for g in range(WB_KV):
        qs = []
        sink = jnp.zeros((1, grp * BLOCK), F32)
        for j in range(grp):
            h = g * grp + j
            qt = q_ref[0, rows, (h // 2) * LANES:(h // 2 + 1) * LANES]
            keep = (lane >= WB_DIM) if h % 2 else (lane < WB_DIM)
            qs.append(jnp.where(keep, qt, jnp.zeros_like(qt)))
            sink = jnp.where(head_of_col == j, sink_ref[:, h:h + 1], sink)
        sinks.append(sink)
        raw.append(_dot_nt(keys[g], jnp.concatenate(qs, axis=0)))
    weights, dens = [], []
    for g in range(WB_KV):
        s = raw[g] + band_ref[...]
        s = jnp.concatenate([s[0:BLOCK]] + [s[(i + 1) * BLOCK:(i + 2) * BLOCK] + blk_bias[i] for i in range(3)],
                            axis=0)
        m = jnp.maximum(jnp.max(s, axis=0, keepdims=True), sinks[g])
        p = jnp.exp(s - m)
        dens.append(jnp.sum(p, axis=0, keepdims=True) + jnp.exp(sinks[g] - m))
        weights.append(p.astype(MXU_DTYPE))
    outs = [_dot_tn(vals[g], weights[g]) for g in range(WB_KV)]
    for g in range(WB_KV):
        o = (outs[g] / dens[g]).T
        for jj in range(grp // 2):
            lo = o[(2 * jj) * BLOCK:(2 * jj + 1) * BLOCK]
            hi = o[(2 * jj + 1) * BLOCK:(2 * jj + 2) * BLOCK]
            t = (g * grp) // 2 + jj
            o_ref[0, rows, t * LANES:(t + 1) * LANES] = jnp.where(lane < WB_DIM, lo, hi).astype(o_ref.dtype)


def _attn_b_call(rq, sink, l_end):
    bsz, lp, _ = rq.shape
    q_blk = (A_Q + A_K) // B_Q
    k_blk = (A_Q + A_K + B_Q) // LANES
    v_blk = k_blk + 2 * B_K // LANES
    n_sub = WINDOW_BLOCKS_PER_STEP if (lp // BLOCK) % WINDOW_BLOCKS_PER_STEP == 0 else 1
    q_rows = n_sub * BLOCK
    grp = WB_HEADS // WB_KV
    krow = np.arange(4 * BLOCK)[:, None]
    qoff = np.arange(grp * BLOCK)[None, :] % BLOCK
    ok = np.where(krow < BLOCK, krow >= ROW_PAD, np.abs(qoff + 2 * BLOCK - krow) <= WINDOW)
    band = jnp.asarray(np.where(ok, 0.0, NEG), F32)
    seq = lambda c: pl.BlockSpec((1, lp, LANES), lambda b, n: (b, 0, c))
    return pl.pallas_call(
        functools.partial(_attn_b_kernel, lp=lp, l_end=l_end),
        grid=(bsz, lp // q_rows),
        in_specs=[pl.BlockSpec((1, WB_HEADS), lambda b, n: (0, 0)),
                  pl.BlockSpec(band.shape, lambda b, n: (0, 0)),
                  pl.BlockSpec((1, q_rows, B_Q), lambda b, n: (b, n, q_blk)),
                  seq(k_blk), seq(k_blk + 1), seq(v_blk), seq(v_blk + 1)],
        out_specs=pl.BlockSpec((1, q_rows, B_Q), lambda b, n: (b, n, 0)),
        out_shape=jax.ShapeDtypeStruct((bsz, lp, B_Q), MXU_DTYPE),
        compiler_params=_params(2),
        name="window_attn",
    )(sink, band, rq, rq, rq, rq, rq)


def _mlstm_kernel(qkf_ref, vf_ref, gcf_ref, grf_ref, qkb_ref, vb_ref, gcb_ref, grb_ref,
                  hf_ref, hb_ref, c_scr, m_scr):
    t = pl.program_id(1)

    @pl.when(t == 0)
    def _():
        c_scr[...] = jnp.zeros_like(c_scr)
        m_scr[...] = jnp.zeros_like(m_scr)

    srow = lax.broadcasted_iota(jnp.int32, (BLOCK, BLOCK), 0)
    ccol = lax.broadcasted_iota(jnp.int32, (BLOCK, BLOCK), 1)
    ext_row = lax.broadcasted_iota(jnp.int32, (MLSTM_EXT, BLOCK), 0)
    ones_rows = jnp.where(ext_row == 0, 1.0, 0.0).astype(MXU_DTYPE)
    n_sub = qkf_ref.shape[1] // BLOCK
    prepared = []
    for sub in range(n_sub):
        rows_f = slice(sub * BLOCK, (sub + 1) * BLOCK)
        rows_b = slice((n_sub - 1 - sub) * BLOCK, (n_sub - sub) * BLOCK)
        dirs = ((qkf_ref, vf_ref, gcf_ref, grf_ref, hf_ref, rows_f, srow <= ccol, BLOCK - 1),
                (qkb_ref, vb_ref, gcb_ref, grb_ref, hb_ref, rows_b, srow >= ccol, 0))
        prepared.append(_mlstm_prepare(dirs, ones_rows))
    for chains in prepared:
        _mlstm_update(chains, c_scr, m_scr)


def _mlstm_prepare(dirs, ones_rows):
    chains = []
    for d, (qk_ref, vt_ref, gc_ref, gr_ref, h_ref, rows, tri, last) in enumerate(dirs):
        for hd in range(MC_HEADS):
            ci = d * MC_HEADS + hd
            j_li = (2 * d) * MC_HEADS + hd
            j_b = (2 * d + 1) * MC_HEADS + hd
            vt = vt_ref[hd * MC_V:(hd + 1) * MC_V, rows]
            b_row = gr_ref[j_b:j_b + 1, rows]
            key_col = gc_ref[rows, j_li:j_li + 1] - gc_ref[rows, j_b:j_b + 1]
            g = b_row[:, last:last + 1]
            dmat = jnp.where(tri, b_row + key_col, NEG)
            a_row = g - b_row + gr_ref[j_li:j_li + 1, rows]
            q = qk_ref[0, rows, hd * MC_QK:(hd + 1) * MC_QK]
            k = qk_ref[0, rows, C_Q + hd * MC_QK:C_Q + (hd + 1) * MC_QK]
            chains.append(dict(
                ci=ci, hd=hd, h_ref=h_ref, rows=rows, q=q, k=k, b_row=b_row, g=g, dmat=dmat, a_row=a_row,
                vext=jnp.concatenate([vt, ones_rows], axis=0),
                kq=_dot_nt(k, q),
                dmax=jnp.max(dmat, axis=0, keepdims=True),
                amax=jnp.max(a_row, axis=1, keepdims=True)))
    return chains


def _mlstm_update(chains, c_scr, m_scr):
    def read_state(ch):
        ch["m_prev"] = m_scr[ch["ci"], 0:1, 0:1]
        ch["c_prev"] = c_scr[ch["ci"]]
        ch["cq"] = _dot_nt(ch["c_prev"].astype(MXU_DTYPE), ch["q"])

    def weights(ch):
        m_t = jnp.maximum(ch["b_row"] + ch["m_prev"], ch["dmax"])
        ch["m_t"] = m_t
        ch["inter"] = jnp.exp(ch["b_row"] + ch["m_prev"] - m_t)
        ch["s"] = (ch["kq"] * jnp.exp(ch["dmat"] - m_t)).astype(MXU_DTYPE)
        m_new = jnp.maximum(ch["g"] + ch["m_prev"], ch["amax"])
        ch["m_new"] = m_new
        ch["decay"] = jnp.exp(ch["g"] + ch["m_prev"] - m_new)
        ch["vw"] = (ch["vext"].astype(F32) * jnp.exp(ch["a_row"] - m_new)).astype(MXU_DTYPE)

    def products(ch):
        ch["vs"] = _dot(ch["vext"], ch["s"])
        ch["dc"] = _dot(ch["vw"], ch["k"])

    def write_back(ch):
        nd = ch["inter"] * ch["cq"] + ch["vs"]
        den = nd[MC_V:MC_V + 1, :]
        h_t = nd[0:MC_V, :] / jnp.maximum(jnp.abs(den), jnp.exp(-ch["m_t"]))
        ch["h_ref"][0, ch["rows"], ch["hd"] * MC_V:(ch["hd"] + 1) * MC_V] = h_t.T
        c_scr[ch["ci"]] = ch["decay"] * ch["c_prev"] + ch["dc"]
        m_scr[ch["ci"]] = jnp.broadcast_to(ch["m_new"], m_scr.shape[1:])

    stages = (read_state, weights, products, write_back)
    for t in range(len(chains) + len(stages) - 1):
        for s, stage in enumerate(stages):
            if 0 <= t - s < len(chains):
                stage(chains[t - s])


def _mlstm_call(qk, vt, gc, gr):
    bsz, lp, _ = qk.shape
    n_sub = MLSTM_CHUNKS_PER_STEP if (lp // BLOCK) % MLSTM_CHUNKS_PER_STEP == 0 else 1
    rows = n_sub * BLOCK
    nch = lp // rows
    fwd = lambda b, t: (b, t, 0)
    bwd = lambda b, t: (b, nch - 1 - t, 0)
    return pl.pallas_call(
        _mlstm_kernel,
        grid=(bsz, nch),
        in_specs=[pl.BlockSpec((1, rows, C_Q + C_K), fwd),
                  pl.BlockSpec((C_V, rows), lambda b, t: (0, b * nch + t)),
                  pl.BlockSpec((rows, LANES), lambda b, t: (b * nch + t, 0)),
                  pl.BlockSpec((C_G, rows), lambda b, t: (0, b * nch + t)),
                  pl.BlockSpec((1, rows, C_Q + C_K), bwd),
                  pl.BlockSpec((C_V, rows), lambda b, t: (0, b * nch + nch - 1 - t)),
                  pl.BlockSpec((rows, LANES), lambda b, t: (b * nch + nch - 1 - t, 0)),
                  pl.BlockSpec((C_G, rows), lambda b, t: (0, b * nch + nch - 1 - t))],
        out_specs=[pl.BlockSpec((1, rows, C_V), fwd),
                   pl.BlockSpec((1, rows, C_V), bwd)],
        out_shape=[jax.ShapeDtypeStruct((bsz, lp, C_V), F32),
                   jax.ShapeDtypeStruct((bsz, lp, C_V), F32)],
        scratch_shapes=[pltpu.VMEM((2 * MC_HEADS, MC_V + MLSTM_EXT, MC_QK), F32),
                        pltpu.VMEM((2 * MC_HEADS, 8, LANES), F32)],
        compiler_params=_params(2),
        name="mlstm_scan",
    )(qk, vt, gc, gr, qk, vt, gc, gr)


def _merge_kernel(h_ref, oa_ref, ob_ref, hf_ref, hb_ref, co_ref, mg_ref, wg_ref, wb_ref, wo_ref,
                  lg_ref, lb_ref, o_ref, *, alpha):
    h = h_ref[...]
    hx = h.astype(MXU_DTYPE)
    hc = hf_ref[...] + hb_ref[...]
    parts = []
    for hd in range(MC_HEADS):
        sl = slice(hd * MC_V, (hd + 1) * MC_V)
        x = hc[:, sl]
        mu = jnp.mean(x, axis=-1, keepdims=True)
        xc = x - mu
        var = jnp.mean(xc * xc, axis=-1, keepdims=True)
        parts.append(xc * lax.rsqrt(var + LN_EPS) * mg_ref[:, sl] * co_ref[:, sl])
    oc = jnp.concatenate(parts, axis=1).astype(MXU_DTYPE)
    branches = (oa_ref[...], ob_ref[...], oc)
    merged = None
    for br in range(N_BRANCH):
        gate = jax.nn.sigmoid(_dot(hx, wg_ref[:, br * D_MODEL:(br + 1) * D_MODEL]))
        term = gate * _dot(branches[br], wb_ref[br])
        merged = term if merged is None else merged + term
    y = _dot(merged.astype(MXU_DTYPE), wo_ref[...])
    o_ref[...] = _layer_norm(alpha * h + y, lg_ref[...], lb_ref[...])


def _merge_call(h, oa, ob, hf, hb, co, mg, wg, wb, wo, lg, lb, alpha, tm):
    m, d = h.shape
    rows = lambda n: pl.BlockSpec((tm, n), lambda i: (i, 0))
    full2 = lambda a: pl.BlockSpec(a.shape, lambda i: (0, 0))
    return pl.pallas_call(
        functools.partial(_merge_kernel, alpha=alpha),
        grid=(m // tm,),
        in_specs=[rows(d), rows(A_V), rows(B_Q), rows(C_V), rows(C_V),
                  pl.BlockSpec((tm, C_O), lambda i: (i, (C_Q + C_K) // C_O)),
                  full2(mg), full2(wg), pl.BlockSpec(wb.shape, lambda i: (0, 0, 0)), full2(wo),
                  full2(lg), full2(lb)],
        out_specs=rows(d),
        out_shape=jax.ShapeDtypeStruct((m, d), F32),
        compiler_params=_params(1),
        name="merge_ln1",
    )(h, oa, ob, hf, hb, co, mg, wg, wb, wo, lg, lb)


def _split3(x):
    hi = x.astype(MXU_DTYPE)
    lo = (x - hi.astype(F32)).astype(MXU_DTYPE)
    return hi, lo


def _router_kernel(h_ref, w_ref, b_ref, rt_ref, rc_ref, *, tm, lp, l_end):
    x_hi, x_lo = _split3(h_ref[...])
    w_hi, w_lo = _split3(w_ref[...])
    logits = (_dot_nt(w_hi, x_hi) + _dot_nt(w_hi, x_lo) + _dot_nt(w_lo, x_hi)) + b_ref[...]
    none = float(N_EXPERTS)
    gl = logits[N_EXPERTS:N_EXPERTS + 8]
    grow = lax.broadcasted_iota(jnp.int32, gl.shape, 0).astype(F32)
    gmax = jnp.max(gl, axis=0, keepdims=True)
    g_sel = jnp.min(jnp.where(gl == gmax, grow, none), axis=0, keepdims=True)
    p_grp = 1.0 / jnp.sum(jnp.exp(gl - gmax), axis=0, keepdims=True)
    el = logits[0:N_EXPERTS]
    erow_i = lax.broadcasted_iota(jnp.int32, el.shape, 0)
    erow = erow_i.astype(F32)
    cand = jnp.where((erow_i // EXP_PER_GROUP).astype(F32) == g_sel, el, -jnp.inf)
    top1 = jnp.max(cand, axis=0, keepdims=True)
    i1 = jnp.min(jnp.where(cand == top1, erow, none), axis=0, keepdims=True)
    cand2 = jnp.where(erow == i1, -jnp.inf, cand)
    top2 = jnp.max(cand2, axis=0, keepdims=True)
    i2 = jnp.min(jnp.where(cand2 == top2, erow, none), axis=0, keepdims=True)
    e = jnp.exp(top2 - top1)
    w1 = (1.0 / (1.0 + e)) * p_grp
    w2 = (e / (1.0 + e)) * p_grp
    pos = lax.broadcasted_iota(jnp.int32, (1, tm), 1) + pl.program_id(0) * tm
    real = ((pos % lp) >= ROW_PAD) & ((pos % lp) < l_end)
    e1 = jnp.where(real, i1, none)
    e2 = jnp.where(real, i2, none)
    r = lax.broadcasted_iota(jnp.int32, (LANES, tm), 0)
    table = jnp.where(r == 0, e1, jnp.where(r == 1, e2, jnp.where(r == 2, w1, jnp.where(r == 3, w2, 0.0))))
    rt_ref[...] = table[0:8]
    rc_ref[...] = table.T


def _router_call(h, w, b, tm, lp, l_end):
    m, d = h.shape
    return pl.pallas_call(
        functools.partial(_router_kernel, tm=tm, lp=lp, l_end=l_end),
        grid=(m // tm,),
        in_specs=[pl.BlockSpec((tm, d), lambda i: (i, 0)),
                  pl.BlockSpec((LANES, d), lambda i: (0, 0)),
                  pl.BlockSpec((LANES, 1), lambda i: (0, 0))],
        out_specs=[pl.BlockSpec((8, tm), lambda i: (0, i)),
                   pl.BlockSpec((tm, LANES), lambda i: (i, 0))],
        out_shape=[jax.ShapeDtypeStruct((8, m), F32),
                   jax.ShapeDtypeStruct((m, LANES), F32)],
        compiler_params=_params(1),
        name="moe_router",
    )(h, w, b)


def _rank_kernel(rt_ref, rk_ref, cnt_ref, carry, *, tm):
    @pl.when(pl.program_id(0) == 0)
    def _():
        carry[...] = jnp.zeros_like(carry)

    erow = lax.broadcasted_iota(jnp.int32, (N_EXPERTS, tm), 0).astype(F32)
    oh1 = jnp.where(erow == rt_ref[0:1, :], 1.0, 0.0)
    oh2 = jnp.where(erow == rt_ref[1:2, :], 1.0, 0.0)
    oh = oh1 + oh2
    earlier = (lax.broadcasted_iota(jnp.int32, (tm, tm), 0)
               < lax.broadcasted_iota(jnp.int32, (tm, tm), 1))
    before = _dot(oh.astype(MXU_DTYPE), jnp.where(earlier, 1.0, 0.0).astype(MXU_DTYPE)) + carry[:, 0:1]
    r1 = jnp.sum(oh1 * before, axis=0, keepdims=True)
    r2 = jnp.sum(oh2 * before, axis=0, keepdims=True)
    r = lax.broadcasted_iota(jnp.int32, (8, tm), 0)
    rk_ref[...] = jnp.where(r == 0, r1, jnp.where(r == 1, r2, 0.0))
    total = carry[...] + jnp.sum(oh, axis=1, keepdims=True)
    carry[...] = total
    cnt_ref[...] = total


def _rank_call(rt, tm):
    m = rt.shape[1]
    return pl.pallas_call(
        functools.partial(_rank_kernel, tm=tm),
        grid=(m // tm,),
        in_specs=[pl.BlockSpec((8, tm), lambda i: (0, i))],
        out_specs=[pl.BlockSpec((8, tm), lambda i: (0, i)),
                   pl.BlockSpec((N_EXPERTS, LANES), lambda i: (0, 0))],
        out_shape=[jax.ShapeDtypeStruct((8, m), F32),
                   jax.ShapeDtypeStruct((N_EXPERTS, LANES), F32)],
        scratch_shapes=[pltpu.VMEM((N_EXPERTS, LANES), F32)],
        compiler_params=_params(1),
        name="moe_rank",
    )(rt)


def _dispatch_kernel(dest_ref, h_ref, xs_in_ref, xs_ref, sem, *, tm, m):
    del xs_in_ref
    base = pl.program_id(0) * tm

    def body(r, c):
        for k in range(2):
            d = dest_ref[k * m + base + r]
            pltpu.make_async_copy(h_ref.at[pl.ds(r, 1)], xs_ref.at[pl.ds(d, 1)], sem).start()
        return c

    lax.fori_loop(0, tm, body, 0, unroll=DMA_ISSUE_UNROLL)
    for _ in range(2):
        pltpu.make_async_copy(h_ref, xs_ref.at[pl.ds(0, tm)], sem).wait()


def _dispatch_call(dest, h, xs0, tm):
    m, d = h.shape
    return pl.pallas_call(
        functools.partial(_dispatch_kernel, tm=tm, m=m),
        grid_spec=pltpu.PrefetchScalarGridSpec(
            num_scalar_prefetch=1,
            grid=(m // tm,),
            in_specs=[pl.BlockSpec((tm, d), lambda i, dest_: (i, 0)),
                      pl.BlockSpec(memory_space=pl.ANY)],
            out_specs=pl.BlockSpec(memory_space=pl.ANY),
            scratch_shapes=[pltpu.SemaphoreType.DMA(())]),
        out_shape=jax.ShapeDtypeStruct(xs0.shape, xs0.dtype),
        input_output_aliases={2: 0},
        compiler_params=_params(1),
        name="moe_dispatch",
    )(dest, h, xs0)


def _ffn_kernel(be_ref, nu_ref, xs_ref, wg_ref, wu_ref, wd_ref, ys_ref, wg_s, wu_s, wd_s):
    i = pl.program_id(0)
    new_expert = jnp.logical_or(i == 0, be_ref[i] != be_ref[jnp.maximum(i - 1, 0)])

    @pl.when(new_expert)
    def _():
        wg_s[...] = wg_ref[0, 0].astype(MXU_DTYPE)
        wu_s[...] = wu_ref[0, 0].astype(MXU_DTYPE)
        wd_s[...] = wd_ref[0, 0].astype(MXU_DTYPE)

    @pl.when(i < nu_ref[0])
    def _():
        xb = xs_ref[...].astype(MXU_DTYPE)
        act = jax.nn.silu(_dot(xb, wg_s[...])) * _dot(xb, wu_s[...])
        ys_ref[...] = _dot(act.astype(MXU_DTYPE), wd_s[...])

    @pl.when(i >= nu_ref[0])
    def _():
        ys_ref[...] = jnp.zeros_like(ys_ref)


def _ffn_call(block_e, n_used, xs, wg, wu, wd, layer, n_blocks):
    d = xs.shape[1]
    br = EXPERT_ROWS
    rows = lambda i, be, nu: (jnp.minimum(i, nu[0] - 1), 0)
    return pl.pallas_call(
        _ffn_kernel,
        grid_spec=pltpu.PrefetchScalarGridSpec(
            num_scalar_prefetch=2,
            grid=(n_blocks,),
            in_specs=[pl.BlockSpec((br, d), rows),
                      pl.BlockSpec((1, 1, d, D_EXPERT), lambda i, be, nu: (layer, be[i], 0, 0)),
                      pl.BlockSpec((1, 1, d, D_EXPERT), lambda i, be, nu: (layer, be[i], 0, 0)),
                      pl.BlockSpec((1, 1, D_EXPERT, d), lambda i, be, nu: (layer, be[i], 0, 0))],
            out_specs=pl.BlockSpec((br, d), lambda i, be, nu: (i, 0)),
            scratch_shapes=[pltpu.VMEM((d, D_EXPERT), MXU_DTYPE),
                            pltpu.VMEM((d, D_EXPERT), MXU_DTYPE),
                            pltpu.VMEM((D_EXPERT, d), MXU_DTYPE)]),
        out_shape=jax.ShapeDtypeStruct((n_blocks * br, d), F32),
        compiler_params=_params(1),
        name="moe_experts",
    )(block_e, n_used, xs, wg, wu, wd)


def _combine_kernel(src_ref, h_ref, rc_ref, lg_ref, lb_ref, ys_ref, o_ref, buf, sem, *, tm, m, alpha):
    i = pl.program_id(0)

    def issue(tile, slot):
        base = tile * tm

        def body(r, c):
            for k in range(2):
                s = src_ref[k * m + base + r]
                pltpu.make_async_copy(ys_ref.at[pl.ds(s, 1)], buf.at[slot, k, pl.ds(r, 1)], sem.at[slot]).start()
            return c

        lax.fori_loop(0, tm, body, 0, unroll=DMA_ISSUE_UNROLL)

    @pl.when(i == 0)
    def _():
        issue(0, 0)

    @pl.when(i + 1 < m // tm)
    def _():
        issue(i + 1, (i + 1) % 2)

    slot = i % 2
    for k in range(2):
        pltpu.make_async_copy(ys_ref.at[pl.ds(0, tm)], buf.at[slot, k], sem.at[slot]).wait()
    rc = rc_ref[...]
    real = rc[:, 0:1] < float(N_EXPERTS)
    y = jnp.where(real, rc[:, 2:3] * buf[slot, 0] + rc[:, 3:4] * buf[slot, 1], 0.0)
    o_ref[...] = _layer_norm(alpha * h_ref[...] + y, lg_ref[...], lb_ref[...])


def _combine_call(src, h, rc, lg, lb, ys, alpha, tm):
    m, d = h.shape
    n_tiles = m // tm
    return pl.pallas_call(
        functools.partial(_combine_kernel, tm=tm, m=m, alpha=alpha),
        grid_spec=pltpu.PrefetchScalarGridSpec(
            num_scalar_prefetch=1,
            grid=(n_tiles,),
            in_specs=[pl.BlockSpec((tm, d), lambda i, s: (i, 0)),
                      pl.BlockSpec((tm, LANES), lambda i, s: (i, 0)),
                      pl.BlockSpec((1, d), lambda i, s: (0, 0)),
                      pl.BlockSpec((1, d), lambda i, s: (0, 0)),
                      pl.BlockSpec(memory_space=pl.ANY)],
            out_specs=pl.BlockSpec((tm, d), lambda i, s: (i, 0)),
            scratch_shapes=[pltpu.VMEM((2, 2, tm, d), F32),
                            pltpu.SemaphoreType.DMA((2,))]),
        out_shape=jax.ShapeDtypeStruct((m, d), F32),
        compiler_params=_params(1),
        name="moe_combine_ln2",
    )(src, h, rc, lg, lb, ys)


def _moe(h1, w_rg, b_rg, w_re, b_re, w_gate, w_up, w_down, layer, lg, lb, alpha, bsz, lp, l_end, tm, slot_buf):
    m, d = h1.shape
    wr = jnp.zeros((LANES, d), F32).at[0:N_EXPERTS].set(w_re.T).at[N_EXPERTS:N_EXPERTS + N_GROUPS].set(w_rg.T)
    br_ = jnp.zeros((LANES,), F32).at[0:N_EXPERTS].set(b_re).at[N_EXPERTS:N_EXPERTS + N_GROUPS].set(b_rg)
    br_ = br_.at[N_EXPERTS + N_GROUPS:N_EXPERTS + 8].set(NEG).reshape(LANES, 1)
    rt, rc = _router_call(h1, wr, br_, tm, lp, l_end)
    rk, cnt = _rank_call(rt, tm)

    rows = EXPERT_ROWS
    n_assign = 2 * bsz * (l_end - ROW_PAD)
    n_unused = lp - (l_end - ROW_PAD)
    n_blocks = -(-(n_assign + N_EXPERTS * (rows - 1)) // rows)
    n_slots = n_blocks * rows
    counts = cnt[:, 0].astype(jnp.int32)
    pcounts = (counts + rows - 1) // rows * rows
    pend = jnp.cumsum(pcounts)
    pstart = pend - pcounts
    e = rt[0:2].astype(jnp.int32)
    rank = rk[0:2].astype(jnp.int32)
    real = e < N_EXPERTS
    expert_ids = jnp.arange(N_EXPERTS, dtype=jnp.int32)[:, None, None]
    slot = jnp.sum(jnp.where(e[None] == expert_ids, pstart[:, None, None], 0), axis=0) + rank
    tok = jnp.arange(m, dtype=jnp.int32)
    pos = tok % lp
    unused_idx = (tok // lp) * n_unused + jnp.where(pos < ROW_PAD, pos, pos - l_end + ROW_PAD)
    spare = n_slots + 2 * unused_idx[None, :] + jnp.arange(2, dtype=jnp.int32)[:, None]
    dest = jnp.where(real, slot, spare).reshape(-1)
    src = jnp.where(real, slot, 0).reshape(-1)
    block_start = jnp.arange(n_blocks, dtype=jnp.int32) * rows
    block_e = jnp.minimum(jnp.sum((pend[None, :] <= block_start[:, None]).astype(jnp.int32), axis=1),
                          N_EXPERTS - 1)
    n_used = (pend[-1:] // rows).astype(jnp.int32)

    n_spare = -(-(2 * bsz * n_unused) // rows) * rows
    if slot_buf is None:
        slot_buf = jnp.zeros((n_slots + n_spare, d), F32)
    xs = _dispatch_call(dest, h1, slot_buf, tm)
    ys = _ffn_call(block_e, n_used, xs, w_gate, w_up, w_down, layer, n_blocks)
    return _combine_call(src, h1, rc, lg, lb, ys, alpha, _token_tile(m, 384)), xs


def _rope_tables(lp):
    pos = jnp.arange(lp, dtype=F32) - float(ROW_PAD)
    inv = 1.0 / (ROPE_THETA ** (jnp.arange(0, DA_DIM, 2, dtype=F32) / DA_DIM))
    ang = pos[:, None] * inv[None, :]
    reps = LANES // (DA_DIM // 2)
    sign = jnp.tile(jnp.concatenate([-jnp.ones((DA_DIM // 2,), F32), jnp.ones((DA_DIM // 2,), F32)]), LANES // DA_DIM)
    return jnp.tile(jnp.cos(ang), (1, reps)), jnp.tile(jnp.sin(ang), (1, reps)) * sign[None, :]


def _dup_heads(w, n_heads, dim):
    d = w.shape[0]
    return jnp.broadcast_to(w.reshape(d, n_heads, 1, dim), (d, n_heads, 2, dim)).reshape(d, n_heads * 2 * dim)


def kernel(x, meta, ln_in_g, ln_in_b, w_in, conv_w, conv_b, gate_b, lam_q1, lam_k1, lam_q2, lam_k2, diff_g, sink, mlstm_g, w_branch, w_out, ln1_g, ln1_b, ln2_g, ln2_b, w_rg, b_rg, w_re, b_re, w_gate, w_up, w_down):
    bsz, seq, d = x.shape
    depth = w_in.shape[0]
    assert seq % BLOCK == 0 and d == D_MODEL
    l_end = seq + BLOCK
    lp = -(-l_end // MXU_TILE) * MXU_TILE
    m = bsz * lp
    alpha = (2.0 * depth) ** 0.25
    tm = _row_tile(lp, 768)
    tk_attn = _row_tile(lp, 768)

    hp = jnp.concatenate([jnp.zeros((bsz, ROW_PAD, d), x.dtype),
                          jnp.broadcast_to(meta.astype(x.dtype)[None], (bsz, N_META_TOK, d)), x,
                          jnp.zeros((bsz, lp - l_end, d), x.dtype)], axis=1)
    h = _ln_call(hp.reshape(m, d), ln_in_g, ln_in_b, tm)
    cos, sin = _rope_tables(lp)
    q_scale = DA_DIM ** -0.5
    rope_scale = jnp.concatenate([jnp.full((A_Q,), q_scale * math.log2(math.e), F32), jnp.ones((A_K,), F32),
                                  jnp.full((B_Q,), q_scale, F32), jnp.ones((2 * B_K,), F32)]).reshape(1, -1)
    conv_scale = jnp.concatenate([jnp.ones((C_Q,), F32), jnp.full((C_K,), MC_QK ** -0.5, F32)]).reshape(1, -1)

    slot_buf = None
    for l in range(depth):
        lam_init = 0.8 - 0.6 * math.exp(-0.3 * l)
        wl = w_in[l]
        col = lambda i: wl[:, OFFS[i]:OFFS[i + 1]]
        w_rope = jnp.concatenate([col(0), col(1), col(3), _dup_heads(col(4), WB_KV, WB_DIM),
                                  _dup_heads(col(5), WB_KV, WB_DIM)], axis=1).astype(MXU_DTYPE)
        w_vt = jnp.concatenate([col(8), col(2)], axis=1).T.astype(MXU_DTYPE)
        w_conv = jnp.concatenate([col(6), col(7), col(9)], axis=1).astype(MXU_DTYPE)
        w_g = jnp.pad(col(10), ((0, 0), (0, LANES - C_G))).astype(MXU_DTYPE)
        b_g = jnp.pad(gate_b[l], (0, LANES - C_G)).reshape(1, LANES)
        w_mg = col(11).astype(MXU_DTYPE)

        rq = _proj_rope_call(h, w_rope, cos, sin, rope_scale, tm, lp).reshape(bsz, lp, -1)
        vt = _proj_t_call(h, w_vt, tm, "proj_val_t")
        zco = _proj_call(h, w_conv, tm, F32, C_Q + C_K, "proj_conv_gate")
        gc, gr = _gates_call(h, w_g, b_g, tm, lp, l_end)

        lamv = jnp.stack([lam_q1[l], lam_k1[l], lam_q2[l], lam_k2[l]])
        out_a = _attn_a_call(rq, vt, lamv, diff_g[l].reshape(-1, 1), lam_init, l_end, MXU_TILE, tk_attn)
        out_b = _attn_b_call(rq, sink[l].reshape(1, -1), l_end)
        qk = _conv_call(zco.reshape(bsz, lp, -1), conv_w[l], conv_b[l].reshape(1, -1), conv_scale, l_end,
                        _row_tile(lp, 768))
        h_f, h_b = _mlstm_call(qk, vt, gc, gr)

        h = _merge_call(h, out_a.reshape(m, -1), out_b.reshape(m, -1), h_f.reshape(m, -1), h_b.reshape(m, -1),
                        zco, mlstm_g[l].reshape(1, -1), w_mg, w_branch[l].astype(MXU_DTYPE),
                        w_out[l].astype(MXU_DTYPE), ln1_g[l].reshape(1, -1), ln1_b[l].reshape(1, -1),
                        alpha, _token_tile(m, 384))
        h, slot_buf = _moe(h, w_rg[l], b_rg[l], w_re[l], b_re[l], w_gate, w_up, w_down, l,
                           ln2_g[l].reshape(1, -1), ln2_b[l].reshape(1, -1), alpha, bsz, lp, l_end, tm, slot_buf)
    return h.reshape(bsz, lp, d)[:, BLOCK:l_end]
```

```python
import functools
import math

import numpy as np
import jax
import jax.numpy as jnp
from jax import lax
from jax.experimental import pallas as pl
from jax.experimental.pallas import tpu as pltpu

D_MODEL = 1024
N_META_TOK = 16
BLOCK = 128
ROW_PAD = BLOCK - N_META_TOK
ROPE_THETA = 10000.0
LN_EPS = 1e-5
NEG = -1e30

DA_HEADS = 4
DA_DIM = 64
WB_HEADS = 8
WB_KV = 2
WB_DIM = 64
WINDOW = 128
MC_HEADS = 4
MC_QK = 128
MC_V = 128
N_BRANCH = 3
BRANCH_W = 512
N_GROUPS = 4
EXP_PER_GROUP = 8
N_EXPERTS = N_GROUPS * EXP_PER_GROUP
D_EXPERT = 512

A_Q = DA_HEADS * 2 * DA_DIM
A_K = A_Q
A_V = A_Q
B_Q = WB_HEADS * WB_DIM
B_K = WB_KV * WB_DIM
B_V = B_K
C_Q = MC_HEADS * MC_QK
C_K = C_Q
C_V = MC_HEADS * MC_V
C_O = C_V
C_G = 4 * MC_HEADS
GATE_W = N_BRANCH * D_MODEL
SPLITS = (A_Q, A_K, A_V, B_Q, B_K, B_V, C_Q, C_K, C_V, C_O, C_G, GATE_W)
OFFS = tuple(int(v) for v in np.cumsum((0,) + SPLITS))

LANES = 128
MXU_TILE = 256
EXPERT_ROWS = 512
WINDOW_BLOCKS_PER_STEP = 6
MLSTM_CHUNKS_PER_STEP = 6
MLSTM_EXT = 16
DMA_ISSUE_UNROLL = 8
ATTN_Q_TILES = 3
ROW_TILE_TARGET = 768
TOKEN_TILE_TARGET = 384
ATTN_UNROLL = 4
ATTN_EXT = 16
VMEM_LIMIT = 56 * 1024 * 1024

F32 = jnp.float32
MXU_DTYPE = jnp.bfloat16


def _dot(a, b):
    return jnp.dot(a, b, preferred_element_type=F32)


def _dot_nt(a, b):
    return lax.dot_general(a, b, (((1,), (1,)), ((), ())), preferred_element_type=F32)


def _dot_tn(a, b):
    return lax.dot_general(a, b, (((0,), (0,)), ((), ())), preferred_element_type=F32)


def _params(n_axes, flags=None):
    return pltpu.CompilerParams(dimension_semantics=("arbitrary",) * n_axes,
                                vmem_limit_bytes=VMEM_LIMIT, flags=flags)


def _row_tile(n_rows, target):
    best = BLOCK
    for t in range(BLOCK, target + 1, BLOCK):
        if n_rows % t == 0:
            best = t
    return best


def _token_tile(n_rows, target):
    best = 8
    for t in range(8, target + 1, 8):
        if n_rows % t == 0:
            best = t
    return best


def _layer_norm(x, g, b):
    mu = jnp.mean(x, axis=-1, keepdims=True)
    xc = x - mu
    var = jnp.mean(xc * xc, axis=-1, keepdims=True)
    return xc * lax.rsqrt(var + LN_EPS) * g + b


def _ln_kernel(x_ref, g_ref, b_ref, o_ref):
    o_ref[...] = _layer_norm(x_ref[...], g_ref[...], b_ref[...])


def _ln_call(x, g, b, tm):
    m, d = x.shape
    return pl.pallas_call(
        _ln_kernel,
        grid=(m // tm,),
        in_specs=[pl.BlockSpec((tm, d), lambda i: (i, 0)),
                  pl.BlockSpec((1, d), lambda i: (0, 0)),
                  pl.BlockSpec((1, d), lambda i: (0, 0))],
        out_specs=pl.BlockSpec((tm, d), lambda i: (i, 0)),
        out_shape=jax.ShapeDtypeStruct((m, d), F32),
        compiler_params=_params(1),
        name="ln_in",
    )(x, g.reshape(1, d), b.reshape(1, d))


def _proj_rope_kernel(x_ref, w_ref, cos_ref, sin_ref, scale_ref, o_ref, *, n_rope):
    z = _dot(x_ref[...].astype(MXU_DTYPE), w_ref[...])
    cos = cos_ref[...]
    sin = sin_ref[...]
    lane = lax.broadcasted_iota(jnp.int32, cos.shape, 1)
    first_half = (lane % DA_DIM) < (DA_DIM // 2)
    for c in range(n_rope // LANES):
        sl = slice(c * LANES, (c + 1) * LANES)
        zc = z[:, sl]
        partner = jnp.where(first_half, pltpu.roll(zc, LANES - DA_DIM // 2, 1),
                            pltpu.roll(zc, DA_DIM // 2, 1))
        o_ref[:, sl] = ((zc * cos + partner * sin) * scale_ref[:, sl]).astype(o_ref.dtype)
    o_ref[:, n_rope:] = z[:, n_rope:].astype(o_ref.dtype)


def _proj_rope_call(h, w, cos, sin, scale, tm, lp):
    m, d = h.shape
    n = w.shape[1]
    per_batch = lp // tm
    return pl.pallas_call(
        functools.partial(_proj_rope_kernel, n_rope=scale.shape[1]),
        grid=(m // tm,),
        in_specs=[pl.BlockSpec((tm, d), lambda i: (i, 0)),
                  pl.BlockSpec((d, n), lambda i: (0, 0)),
                  pl.BlockSpec((tm, LANES), lambda i: (i % per_batch, 0)),
                  pl.BlockSpec((tm, LANES), lambda i: (i % per_batch, 0)),
                  pl.BlockSpec(scale.shape, lambda i: (0, 0))],
        out_specs=pl.BlockSpec((tm, n), lambda i: (i, 0)),
        out_shape=jax.ShapeDtypeStruct((m, n), MXU_DTYPE),
        compiler_params=_params(1),
        name="proj_rope",
    )(h, w, cos, sin, scale)


def _proj_kernel(x_ref, w_ref, o_ref, *, sigmoid_from):
    z = _dot(x_ref[...].astype(MXU_DTYPE), w_ref[...])
    if sigmoid_from is None:
        o_ref[...] = z.astype(o_ref.dtype)
    else:
        o_ref[:, :sigmoid_from] = z[:, :sigmoid_from].astype(o_ref.dtype)
        o_ref[:, sigmoid_from:] = jax.nn.sigmoid(z[:, sigmoid_from:]).astype(o_ref.dtype)


def _proj_call(h, w, tm, out_dtype, sigmoid_from, name):
    m, d = h.shape
    n = w.shape[1]
    return pl.pallas_call(
        functools.partial(_proj_kernel, sigmoid_from=sigmoid_from),
        grid=(m // tm,),
        in_specs=[pl.BlockSpec((tm, d), lambda i: (i, 0)),
                  pl.BlockSpec((d, n), lambda i: (0, 0))],
        out_specs=pl.BlockSpec((tm, n), lambda i: (i, 0)),
        out_shape=jax.ShapeDtypeStruct((m, n), out_dtype),
        compiler_params=_params(1),
        name=name,
    )(h, w)


def _proj_t_kernel(x_ref, wt_ref, o_ref):
    o_ref[...] = _dot_nt(wt_ref[...], x_ref[...].astype(MXU_DTYPE)).astype(o_ref.dtype)


def _proj_t_call(h, wt, tm, name):
    m, d = h.shape
    n = wt.shape[0]
    return pl.pallas_call(
        _proj_t_kernel,
        grid=(m // tm,),
        in_specs=[pl.BlockSpec((tm, d), lambda i: (i, 0)),
                  pl.BlockSpec((n, d), lambda i: (0, 0))],
        out_specs=pl.BlockSpec((n, tm), lambda i: (0, i)),
        out_shape=jax.ShapeDtypeStruct((n, m), MXU_DTYPE),
        compiler_params=_params(1),
        name=name,
    )(h, wt)


def _gates_kernel(x_ref, w_ref, b_ref, gc_ref, gr_ref, *, tm, lp, l_end):
    z = _dot(x_ref[...].astype(MXU_DTYPE), w_ref[...]) + b_ref[...]
    lane = lax.broadcasted_iota(jnp.int32, (tm, LANES), 1)
    kind = lane // MC_HEADS
    row = lax.broadcasted_iota(jnp.int32, (tm, LANES), 0) + pl.program_id(0) * tm
    pos = row % lp
    unused = (pos < ROW_PAD) | (pos >= l_end)
    log_f = jnp.minimum(z, 0.0) - jnp.log1p(jnp.exp(-jnp.abs(z)))
    is_forget = (kind % 2) == 1
    base = jnp.where(is_forget, jnp.where(unused, 0.0, log_f), jnp.where(unused, NEG, z))
    r128 = lax.broadcasted_iota(jnp.int32, (BLOCK, LANES), 0)
    fwd_lane = lax.broadcasted_iota(jnp.int32, (BLOCK, LANES), 1) // MC_HEADS == 1
    forget128 = (lax.broadcasted_iota(jnp.int32, (BLOCK, LANES), 1) // MC_HEADS) % 2 == 1
    for c in range(tm // BLOCK):
        x = base[c * BLOCK:(c + 1) * BLOCK]
        pre = x
        suf = x
        s = 1
        while s < BLOCK:
            pre = pre + jnp.where(r128 >= s, pltpu.roll(pre, s, 0), 0.0)
            suf = suf + jnp.where(r128 < BLOCK - s, pltpu.roll(suf, BLOCK - s, 0), 0.0)
            s *= 2
        out = jnp.where(forget128, jnp.where(fwd_lane, pre, suf), x)
        gc_ref[c * BLOCK:(c + 1) * BLOCK, :] = out
        gr_ref[:, c * BLOCK:(c + 1) * BLOCK] = out.T[0:C_G, :]


def _gates_call(h, w, b, tm, lp, l_end):
    m, d = h.shape
    return pl.pallas_call(
        functools.partial(_gates_kernel, tm=tm, lp=lp, l_end=l_end),
        grid=(m // tm,),
        in_specs=[pl.BlockSpec((tm, d), lambda i: (i, 0)),
                  pl.BlockSpec((d, LANES), lambda i: (0, 0)),
                  pl.BlockSpec((1, LANES), lambda i: (0, 0))],
        out_specs=[pl.BlockSpec((tm, LANES), lambda i: (i, 0)),
                   pl.BlockSpec((C_G, tm), lambda i: (0, i))],
        out_shape=[jax.ShapeDtypeStruct((m, LANES), F32),
                   jax.ShapeDtypeStruct((C_G, m), F32)],
        compiler_params=_params(1),
        name="mlstm_gates",
    )(h, w, b)


def _conv_kernel(z_ref, w_ref, b_ref, scale_ref, o_ref, *, lp, l_end, tr):
    w0 = w_ref[0:1, :]
    w1 = w_ref[1:2, :]
    w2 = w_ref[2:3, :]
    row = lax.broadcasted_iota(jnp.int32, (tr, LANES), 0)
    for c in range(lp // tr):
        r0 = c * tr
        zc = z_ref[0, r0:r0 + tr, :]
        before = jnp.zeros((1, LANES), F32) if r0 == 0 else z_ref[0, r0 - 1:r0, :]
        after = jnp.zeros((1, LANES), F32) if r0 + tr == lp else z_ref[0, r0 + tr:r0 + tr + 1, :]
        prev = jnp.where(row == 0, before, pltpu.roll(zc, 1, 0))
        nxt = jnp.where(row == tr - 1, after, pltpu.roll(zc, tr - 1, 0))
        if r0 <= ROW_PAD < r0 + tr:
            prev = jnp.where(row == ROW_PAD - r0, 0.0, prev)
        if r0 <= l_end - 1 < r0 + tr:
            nxt = jnp.where(row == l_end - 1 - r0, 0.0, nxt)
        y = prev * w0 + zc * w1 + nxt * w2 + b_ref[...]
        o_ref[0, r0:r0 + tr, :] = (jax.nn.silu(y) * scale_ref[...]).astype(o_ref.dtype)


def _conv_call(z, w, b, scale, l_end, tr):
    bsz, lp, _ = z.shape
    n = w.shape[1]
    return pl.pallas_call(
        functools.partial(_conv_kernel, lp=lp, l_end=l_end, tr=tr),
        grid=(bsz, n // LANES),
        in_specs=[pl.BlockSpec((1, lp, LANES), lambda b_, j: (b_, 0, j)),
                  pl.BlockSpec((3, LANES), lambda b_, j: (0, j)),
                  pl.BlockSpec((1, LANES), lambda b_, j: (0, j)),
                  pl.BlockSpec((1, LANES), lambda b_, j: (0, j))],
        out_specs=pl.BlockSpec((1, lp, LANES), lambda b_, j: (b_, 0, j)),
        out_shape=jax.ShapeDtypeStruct((bsz, lp, n), MXU_DTYPE),
        compiler_params=_params(2),
        name="mlstm_conv",
    )(z, w, b, scale)


def _attn_a_kernel(lamv_ref, g_ref, q_ref, k_ref, vt_ref, o_ref, s0_scr, s1_scr, acc_scr, *,
                   tk, n_chunks, l_end, lam_init):
    n_ch = acc_scr.shape[0]
    tq = acc_scr.shape[2]
    feat = lax.broadcasted_iota(jnp.int32, (2 * DA_DIM, tq), 0)
    key_row = lax.broadcasted_iota(jnp.int32, (tk, tq), 0)
    qz = []
    for t in range(n_ch // 2):
        qt = q_ref[0, t * tq:(t + 1) * tq, :].astype(F32).T.astype(MXU_DTYPE)
        zero = jnp.zeros_like(qt)
        qz += [jnp.where(feat < DA_DIM, qt, zero), jnp.where(feat >= DA_DIM, qt, zero)]
    s_bufs = (s0_scr, s1_scr)

    last = n_chunks - 1
    last_hi = l_end - last * tk

    def keys(j):
        return k_ref[0, pl.ds(pl.multiple_of(j * tk, tk), tk), :]

    def values(j):
        vt = vt_ref[:, pl.ds(pl.multiple_of(j * tk, tk), tk)]
        return jnp.concatenate([vt, jnp.ones((ATTN_EXT, tk), MXU_DTYPE)], axis=0)

    def score_chain(j, kj, slot, c):
        s = _dot(kj, qz[c])
        if isinstance(j, int) and j == 0:
            s = jnp.where(key_row >= ROW_PAD, s, NEG)
        if isinstance(j, int) and j == last and last_hi < tk:
            s = jnp.where(key_row < last_hi, s, NEG)
        s_bufs[slot][c] = s
        return jnp.max(s, axis=0, keepdims=True)

    def softmax_chain(vt, slot, c, m, cmax):
        m_new = jnp.maximum(m, cmax)
        alpha = jnp.exp2(m - m_new)
        p = jnp.exp2((s_bufs[slot][c] - m_new).astype(MXU_DTYPE))
        acc_scr[c] = alpha * acc_scr[c] + _dot(vt, p)
        return m_new

    def scores(j, slot):
        kj = keys(j)
        return tuple(score_chain(j, kj, slot, c) for c in range(n_ch))

    def softmax_values(j, slot, stats, cmax):
        vt = values(j)
        return tuple(softmax_chain(vt, slot, c, stats[c], cmax[c]) for c in range(n_ch))

    one = jnp.full((1, tq), NEG, F32)
    acc_scr[...] = jnp.zeros_like(acc_scr)
    cmax = scores(0, 0)
    stats = (one,) * n_ch
    if n_chunks > 1:
        def step(j, parity, state):
            stats, cmax = state
            kj, vt = keys(j + 1), values(j)
            new_stats, nxt = [], []
            for c in range(n_ch):
                nxt.append(score_chain(j + 1, kj, 1 - parity, c))
                new_stats.append(softmax_chain(vt, parity, c, stats[c], cmax[c]))
            return tuple(new_stats), tuple(nxt)

        def trip(i, st):
            for u in range(ATTN_UNROLL):
                st = step(1 + ATTN_UNROLL * i + u, (1 + u) % 2, st)
            return st

        state = step(0, 0, (stats, cmax))
        n_trips = (last - 2) // ATTN_UNROLL if last >= 2 else 0
        state = lax.fori_loop(0, n_trips, trip, state)
        for j in range(1 + n_trips * ATTN_UNROLL, last):
            state = step(j, j % 2, state)
        stats, cmax = state
    softmax_values(last, last % 2, stats, cmax)
    dv = 2 * DA_DIM
    lv = lamv_ref[...]
    lam = (jnp.exp(jnp.sum(lv[0:1] * lv[1:2], axis=-1, keepdims=True))
           - jnp.exp(jnp.sum(lv[2:3] * lv[3:4], axis=-1, keepdims=True)) + lam_init)
    for t in range(n_ch // 2):
        o0 = acc_scr[2 * t, 0:dv, :] / acc_scr[2 * t, dv:dv + 1, :]
        o1 = acc_scr[2 * t + 1, 0:dv, :] / acc_scr[2 * t + 1, dv:dv + 1, :]
        o = o0 - lam * o1
        ms = jnp.mean(o * o, axis=0, keepdims=True)
        o = o * lax.rsqrt(ms + LN_EPS) * g_ref[...] * (1.0 - lam_init)
        o_ref[0, t * tq:(t + 1) * tq, :] = o.T.astype(o_ref.dtype)


def _attn_a_call(rq, vt, lamv, g_col, lam_init, l_end, tq, tk):
    bsz, lp, _ = rq.shape
    k_blk = A_Q // LANES
    vt_blk = C_V // LANES
    n_q = ATTN_Q_TILES if (lp // tq) % ATTN_Q_TILES == 0 else 1
    n_ch = 2 * n_q
    return pl.pallas_call(
        functools.partial(_attn_a_kernel, tk=tk, n_chunks=lp // tk, l_end=l_end, lam_init=lam_init),
        grid=(bsz, DA_HEADS, lp // (n_q * tq)),
        in_specs=[pl.BlockSpec((4, DA_DIM), lambda b, h, i: (0, 0)),
                  pl.BlockSpec((2 * DA_DIM, 1), lambda b, h, i: (0, 0)),
                  pl.BlockSpec((1, n_q * tq, LANES), lambda b, h, i: (b, i, h)),
                  pl.BlockSpec((1, lp, LANES), lambda b, h, i: (b, 0, k_blk + h)),
                  pl.BlockSpec((2 * DA_DIM, lp), lambda b, h, i: (vt_blk + h, b))],
        out_specs=pl.BlockSpec((1, n_q * tq, LANES), lambda b, h, i: (b, i, h)),
        out_shape=jax.ShapeDtypeStruct((bsz, lp, A_V), MXU_DTYPE),
        scratch_shapes=[pltpu.VMEM((n_ch, tk, tq), F32),
                        pltpu.VMEM((n_ch, tk, tq), F32),
                        pltpu.VMEM((n_ch, 2 * DA_DIM + ATTN_EXT, tq), F32)],
        compiler_params=_params(3),
        name="diff_attn",
    )(lamv, g_col, rq, rq, vt)


def _attn_b_kernel(sink_ref, band_ref, q_ref, k0_ref, k1_ref, v0_ref, v1_ref, o_ref, *, lp, l_end):
    n_sub = q_ref.shape[1] // BLOCK
    for sub in range(n_sub):
        _attn_b_block(sink_ref, band_ref, q_ref, k0_ref, k1_ref, v0_ref, v1_ref, o_ref,
                      pl.program_id(1) * n_sub + sub, slice(sub * BLOCK, (sub + 1) * BLOCK), lp, l_end)


def _attn_b_block(sink_ref, band_ref, q_ref, k0_ref, k1_ref, v0_ref, v1_ref, o_ref, n, rows, lp, l_end):
    nb = lp // BLOCK
    grp = WB_HEADS // WB_KV

    def blocks(ref):
        parts = [ref[0, 0:BLOCK, :]]
        for d in (-1, 0, 1):
            idx = jnp.clip(n + d, 0, nb - 1)
            parts.append(ref[0, pl.ds(pl.multiple_of(idx * BLOCK, BLOCK), BLOCK), :])
        return jnp.concatenate(parts, axis=0)

    keys = (blocks(k0_ref), blocks(k1_ref))
    vals = (blocks(v0_ref), blocks(v1_ref))
    blk_bias = []
    for d in (-1, 0, 1):
        inside = jnp.logical_and(n + d >= 1, n + d <= l_end // BLOCK - 1)
        blk_bias.append(jnp.where(inside, 0.0, NEG))
    head_of_col = lax.broadcasted_iota(jnp.int32, (1, grp * BLOCK), 1) // BLOCK
    lane = lax.broadcasted_iota(jnp.int32, (BLOCK, LANES), 1)
    sinks, raw = [], []
    for g in range(WB_KV):
        qs = []
        sink = jnp.zeros((1, grp * BLOCK), F32)
        for j in range(grp):
            h = g * grp + j
            qt = q_ref[0, rows, (h // 2) * LANES:(h // 2 + 1) * LANES]
            keep = (lane >= WB_DIM) if h % 2 else (lane < WB_DIM)
            qs.append(jnp.where(keep, qt, jnp.zeros_like(qt)))
            sink = jnp.where(head_of_col == j, sink_ref[:, h:h + 1], sink)
        sinks.append(sink)
        raw.append(_dot_nt(keys[g], jnp.concatenate(qs, axis=0)))
    weights, dens = [], []
    for g in range(WB_KV):
        s = raw[g] + band_ref[...]
        s = jnp.concatenate([s[0:BLOCK]] + [s[(i + 1) * BLOCK:(i + 2) * BLOCK] + blk_bias[i] for i in range(3)],
                            axis=0)
        m = jnp.maximum(jnp.max(s, axis=0, keepdims=True), sinks[g])
        p = jnp.exp(s - m)
        dens.append(jnp.sum(p, axis=0, keepdims=True) + jnp.exp(sinks[g] - m))
        weights.append(p.astype(MXU_DTYPE))
    outs = [_dot_tn(vals[g], weights[g]) for g in range(WB_KV)]
    for g in range(WB_KV):
        o = (outs[g] / dens[g]).T
        for jj in range(grp // 2):
            lo = o[(2 * jj) * BLOCK:(2 * jj + 1) * BLOCK]
            hi = o[(2 * jj + 1) * BLOCK:(2 * jj + 2) * BLOCK]
            t = (g * grp) // 2 + jj
            o_ref[0, rows, t * LANES:(t + 1) * LANES] = jnp.where(lane < WB_DIM, lo, hi).astype(o_ref.dtype)


def _attn_b_call(rq, sink, l_end):
    bsz, lp, _ = rq.shape
    q_blk = (A_Q + A_K) // B_Q
    k_blk = (A_Q + A_K + B_Q) // LANES
    v_blk = k_blk + 2 * B_K // LANES
    n_sub = WINDOW_BLOCKS_PER_STEP if (lp // BLOCK) % WINDOW_BLOCKS_PER_STEP == 0 else 1
    q_rows = n_sub * BLOCK
    grp = WB_HEADS // WB_KV
    krow = np.arange(4 * BLOCK)[:, None]
    qoff = np.arange(grp * BLOCK)[None, :] % BLOCK
    ok = np.where(krow < BLOCK, krow >= ROW_PAD, np.abs(qoff + 2 * BLOCK - krow) <= WINDOW)
    band = jnp.asarray(np.where(ok, 0.0, NEG), F32)
    seq = lambda c: pl.BlockSpec((1, lp, LANES), lambda b, n: (b, 0, c))
    return pl.pallas_call(
        functools.partial(_attn_b_kernel, lp=lp, l_end=l_end),
        grid=(bsz, lp // q_rows),
        in_specs=[pl.BlockSpec((1, WB_HEADS), lambda b, n: (0, 0)),
                  pl.BlockSpec(band.shape, lambda b, n: (0, 0)),
                  pl.BlockSpec((1, q_rows, B_Q), lambda b, n: (b, n, q_blk)),
                  seq(k_blk), seq(k_blk + 1), seq(v_blk), seq(v_blk + 1)],
        out_specs=pl.BlockSpec((1, q_rows, B_Q), lambda b, n: (b, n, 0)),
        out_shape=jax.ShapeDtypeStruct((bsz, lp, B_Q), MXU_DTYPE),
        compiler_params=_params(2),
        name="window_attn",
    )(sink, band, rq, rq, rq, rq, rq)


def _mlstm_kernel(qkf_ref, vf_ref, gcf_ref, grf_ref, qkb_ref, vb_ref, gcb_ref, grb_ref,
                  hf_ref, hb_ref, c_scr, m_scr):
    t = pl.program_id(1)

    @pl.when(t == 0)
    def _():
        c_scr[...] = jnp.zeros_like(c_scr)
        m_scr[...] = jnp.zeros_like(m_scr)

    srow = lax.broadcasted_iota(jnp.int32, (BLOCK, BLOCK), 0)
    ccol = lax.broadcasted_iota(jnp.int32, (BLOCK, BLOCK), 1)
    ext_row = lax.broadcasted_iota(jnp.int32, (MLSTM_EXT, BLOCK), 0)
    ones_rows = jnp.where(ext_row == 0, 1.0, 0.0).astype(MXU_DTYPE)
    n_sub = qkf_ref.shape[1] // BLOCK
    prepared = []
    for sub in range(n_sub):
        rows_f = slice(sub * BLOCK, (sub + 1) * BLOCK)
        rows_b = slice((n_sub - 1 - sub) * BLOCK, (n_sub - sub) * BLOCK)
        dirs = ((qkf_ref, vf_ref, gcf_ref, grf_ref, hf_ref, rows_f, srow <= ccol, BLOCK - 1),
                (qkb_ref, vb_ref, gcb_ref, grb_ref, hb_ref, rows_b, srow >= ccol, 0))
        prepared.append(_mlstm_prepare(dirs, ones_rows))
    for chains in prepared:
        _mlstm_update(chains, c_scr, m_scr)


def _mlstm_prepare(dirs, ones_rows):
    chains = []
    for d, (qk_ref, vt_ref, gc_ref, gr_ref, h_ref, rows, tri, last) in enumerate(dirs):
        for hd in range(MC_HEADS):
            ci = d * MC_HEADS + hd
            j_li = (2 * d) * MC_HEADS + hd
            j_b = (2 * d + 1) * MC_HEADS + hd
            vt = vt_ref[hd * MC_V:(hd + 1) * MC_V, rows]
            b_row = gr_ref[j_b:j_b + 1, rows]
            key_col = gc_ref[rows, j_li:j_li + 1] - gc_ref[rows, j_b:j_b + 1]
            g = b_row[:, last:last + 1]
            dmat = jnp.where(tri, b_row + key_col, NEG)
            a_row = g - b_row + gr_ref[j_li:j_li + 1, rows]
            q = qk_ref[0, rows, hd * MC_QK:(hd + 1) * MC_QK]
            k = qk_ref[0, rows, C_Q + hd * MC_QK:C_Q + (hd + 1) * MC_QK]
            chains.append(dict(
                ci=ci, hd=hd, h_ref=h_ref, rows=rows, q=q, k=k, b_row=b_row, g=g, dmat=dmat, a_row=a_row,
                vext=jnp.concatenate([vt, ones_rows], axis=0),
                kq=_dot_nt(k, q),
                dmax=jnp.max(dmat, axis=0, keepdims=True),
                amax=jnp.max(a_row, axis=1, keepdims=True)))
    return chains


def _mlstm_update(chains, c_scr, m_scr):
    def read_state(ch):
        ch["m_prev"] = m_scr[ch["ci"], 0:1, 0:1]
        ch["c_prev"] = c_scr[ch["ci"]]
        ch["cq"] = _dot_nt(ch["c_prev"].astype(MXU_DTYPE), ch["q"])

    def weights(ch):
        m_t = jnp.maximum(ch["b_row"] + ch["m_prev"], ch["dmax"])
        ch["m_t"] = m_t
        ch["inter"] = jnp.exp(ch["b_row"] + ch["m_prev"] - m_t)
        ch["s"] = (ch["kq"] * jnp.exp(ch["dmat"] - m_t)).astype(MXU_DTYPE)
        m_new = jnp.maximum(ch["g"] + ch["m_prev"], ch["amax"])
        ch["m_new"] = m_new
        ch["decay"] = jnp.exp(ch["g"] + ch["m_prev"] - m_new)
        ch["vw"] = (ch["vext"].astype(F32) * jnp.exp(ch["a_row"] - m_new)).astype(MXU_DTYPE)

    def products(ch):
        ch["vs"] = _dot(ch["vext"], ch["s"])
        ch["dc"] = _dot(ch["vw"], ch["k"])

    def write_back(ch):
        nd = ch["inter"] * ch["cq"] + ch["vs"]
        den = nd[MC_V:MC_V + 1, :]
        h_t = nd[0:MC_V, :] / jnp.maximum(jnp.abs(den), jnp.exp(-ch["m_t"]))
        ch["h_ref"][0, ch["rows"], ch["hd"] * MC_V:(ch["hd"] + 1) * MC_V] = h_t.T
        c_scr[ch["ci"]] = ch["decay"] * ch["c_prev"] + ch["dc"]
        m_scr[ch["ci"]] = jnp.broadcast_to(ch["m_new"], m_scr.shape[1:])

    stages = (read_state, weights, products, write_back)
    for t in range(len(chains) + len(stages) - 1):
        for s, stage in enumerate(stages):
            if 0 <= t - s < len(chains):
                stage(chains[t - s])


def _mlstm_call(qk, vt, gc, gr):
    bsz, lp, _ = qk.shape
    n_sub = MLSTM_CHUNKS_PER_STEP if (lp // BLOCK) % MLSTM_CHUNKS_PER_STEP == 0 else 1
    rows = n_sub * BLOCK
    nch = lp // rows
    fwd = lambda b, t: (b, t, 0)
    bwd = lambda b, t: (b, nch - 1 - t, 0)
    return pl.pallas_call(
        _mlstm_kernel,
        grid=(bsz, nch),
        in_specs=[pl.BlockSpec((1, rows, C_Q + C_K), fwd),
                  pl.BlockSpec((C_V, rows), lambda b, t: (0, b * nch + t)),
                  pl.BlockSpec((rows, LANES), lambda b, t: (b * nch + t, 0)),
                  pl.BlockSpec((C_G, rows), lambda b, t: (0, b * nch + t)),
                  pl.BlockSpec((1, rows, C_Q + C_K), bwd),
                  pl.BlockSpec((C_V, rows), lambda b, t: (0, b * nch + nch - 1 - t)),
                  pl.BlockSpec((rows, LANES), lambda b, t: (b * nch + nch - 1 - t, 0)),
                  pl.BlockSpec((C_G, rows), lambda b, t: (0, b * nch + nch - 1 - t))],
        out_specs=[pl.BlockSpec((1, rows, C_V), fwd),
                   pl.BlockSpec((1, rows, C_V), bwd)],
        out_shape=[jax.ShapeDtypeStruct((bsz, lp, C_V), F32),
                   jax.ShapeDtypeStruct((bsz, lp, C_V), F32)],
        scratch_shapes=[pltpu.VMEM((2 * MC_HEADS, MC_V + MLSTM_EXT, MC_QK), F32),
                        pltpu.VMEM((2 * MC_HEADS, 8, LANES), F32)],
        compiler_params=_params(2),
        name="mlstm_scan",
    )(qk, vt, gc, gr, qk, vt, gc, gr)


def _merge_kernel(h_ref, oa_ref, ob_ref, hf_ref, hb_ref, co_ref, mg_ref, wg_ref, wb_ref, wo_ref,
                  lg_ref, lb_ref, o_ref, *, alpha):
    h = h_ref[...]
    hx = h.astype(MXU_DTYPE)
    hc = hf_ref[...] + hb_ref[...]
    parts = []
    for hd in range(MC_HEADS):
        sl = slice(hd * MC_V, (hd + 1) * MC_V)
        x = hc[:, sl]
        mu = jnp.mean(x, axis=-1, keepdims=True)
        xc = x - mu
        var = jnp.mean(xc * xc, axis=-1, keepdims=True)
        parts.append(xc * lax.rsqrt(var + LN_EPS) * mg_ref[:, sl] * co_ref[:, sl])
    oc = jnp.concatenate(parts, axis=1).astype(MXU_DTYPE)
    branches = (oa_ref[...], ob_ref[...], oc)
    merged = None
    for br in range(N_BRANCH):
        gate = jax.nn.sigmoid(_dot(hx, wg_ref[:, br * D_MODEL:(br + 1) * D_MODEL]))
        term = gate * _dot(branches[br], wb_ref[br])
        merged = term if merged is None else merged + term
    y = _dot(merged.astype(MXU_DTYPE), wo_ref[...])
    o_ref[...] = _layer_norm(alpha * h + y, lg_ref[...], lb_ref[...])


def _merge_call(h, oa, ob, hf, hb, co, mg, wg, wb, wo, lg, lb, alpha, tm):
    m, d = h.shape
    rows = lambda n: pl.BlockSpec((tm, n), lambda i: (i, 0))
    full2 = lambda a: pl.BlockSpec(a.shape, lambda i: (0, 0))
    return pl.pallas_call(
        functools.partial(_merge_kernel, alpha=alpha),
        grid=(m // tm,),
        in_specs=[rows(d), rows(A_V), rows(B_Q), rows(C_V), rows(C_V),
                  pl.BlockSpec((tm, C_O), lambda i: (i, (C_Q + C_K) // C_O)),
                  full2(mg), full2(wg), pl.BlockSpec(wb.shape, lambda i: (0, 0, 0)), full2(wo),
                  full2(lg), full2(lb)],
        out_specs=rows(d),
        out_shape=jax.ShapeDtypeStruct((m, d), F32),
        compiler_params=_params(1),
        name="merge_ln1",
    )(h, oa, ob, hf, hb, co, mg, wg, wb, wo, lg, lb)


def _split3(x):
    hi = x.astype(MXU_DTYPE)
    lo = (x - hi.astype(F32)).astype(MXU_DTYPE)
    return hi, lo


def _router_kernel(h_ref, w_ref, b_ref, rt_ref, rc_ref, *, tm, lp, l_end):
    x_hi, x_lo = _split3(h_ref[...])
    w_hi, w_lo = _split3(w_ref[...])
    logits = (_dot_nt(w_hi, x_hi) + _dot_nt(w_hi, x_lo) + _dot_nt(w_lo, x_hi)) + b_ref[...]
    none = float(N_EXPERTS)
    gl = logits[N_EXPERTS:N_EXPERTS + 8]
    grow = lax.broadcasted_iota(jnp.int32, gl.shape, 0).astype(F32)
    gmax = jnp.max(gl, axis=0, keepdims=True)
    g_sel = jnp.min(jnp.where(gl == gmax, grow, none), axis=0, keepdims=True)
    p_grp = 1.0 / jnp.sum(jnp.exp(gl - gmax), axis=0, keepdims=True)
    el = logits[0:N_EXPERTS]
    erow_i = lax.broadcasted_iota(jnp.int32, el.shape, 0)
    erow = erow_i.astype(F32)
    cand = jnp.where((erow_i // EXP_PER_GROUP).astype(F32) == g_sel, el, -jnp.inf)
    top1 = jnp.max(cand, axis=0, keepdims=True)
    i1 = jnp.min(jnp.where(cand == top1, erow, none), axis=0, keepdims=True)
    cand2 = jnp.where(erow == i1, -jnp.inf, cand)
    top2 = jnp.max(cand2, axis=0, keepdims=True)
    i2 = jnp.min(jnp.where(cand2 == top2, erow, none), axis=0, keepdims=True)
    e = jnp.exp(top2 - top1)
    w1 = (1.0 / (1.0 + e)) * p_grp
    w2 = (e / (1.0 + e)) * p_grp
    pos = lax.broadcasted_iota(jnp.int32, (1, tm), 1) + pl.program_id(0) * tm
    real = ((pos % lp) >= ROW_PAD) & ((pos % lp) < l_end)
    e1 = jnp.where(real, i1, none)
    e2 = jnp.where(real, i2, none)
    r = lax.broadcasted_iota(jnp.int32, (LANES, tm), 0)
    table = jnp.where(r == 0, e1, jnp.where(r == 1, e2, jnp.where(r == 2, w1, jnp.where(r == 3, w2, 0.0))))
    rt_ref[...] = table[0:8]
    rc_ref[...] = table.T


def _router_call(h, w, b, tm, lp, l_end):
    m, d = h.shape
    return pl.pallas_call(
        functools.partial(_router_kernel, tm=tm, lp=lp, l_end=l_end),
        grid=(m // tm,),
        in_specs=[pl.BlockSpec((tm, d), lambda i: (i, 0)),
                  pl.BlockSpec((LANES, d), lambda i: (0, 0)),
                  pl.BlockSpec((LANES, 1), lambda i: (0, 0))],
        out_specs=[pl.BlockSpec((8, tm), lambda i: (0, i)),
                   pl.BlockSpec((tm, LANES), lambda i: (i, 0))],
        out_shape=[jax.ShapeDtypeStruct((8, m), F32),
                   jax.ShapeDtypeStruct((m, LANES), F32)],
        compiler_params=_params(1),
        name="moe_router",
    )(h, w, b)


def _rank_kernel(rt_ref, rk_ref, cnt_ref, carry, *, tm):
    @pl.when(pl.program_id(0) == 0)
    def _():
        carry[...] = jnp.zeros_like(carry)

    erow = lax.broadcasted_iota(jnp.int32, (N_EXPERTS, tm), 0).astype(F32)
    oh1 = jnp.where(erow == rt_ref[0:1, :], 1.0, 0.0)
    oh2 = jnp.where(erow == rt_ref[1:2, :], 1.0, 0.0)
    oh = oh1 + oh2
    earlier = (lax.broadcasted_iota(jnp.int32, (tm, tm), 0)
               < lax.broadcasted_iota(jnp.int32, (tm, tm), 1))
    before = _dot(oh.astype(MXU_DTYPE), jnp.where(earlier, 1.0, 0.0).astype(MXU_DTYPE)) + carry[:, 0:1]
    r1 = jnp.sum(oh1 * before, axis=0, keepdims=True)
    r2 = jnp.sum(oh2 * before, axis=0, keepdims=True)
    r = lax.broadcasted_iota(jnp.int32, (8, tm), 0)
    rk_ref[...] = jnp.where(r == 0, r1, jnp.where(r == 1, r2, 0.0))
    total = carry[...] + jnp.sum(oh, axis=1, keepdims=True)
    carry[...] = total
    cnt_ref[...] = total


def _rank_call(rt, tm):
    m = rt.shape[1]
    return pl.pallas_call(
        functools.partial(_rank_kernel, tm=tm),
        grid=(m // tm,),
        in_specs=[pl.BlockSpec((8, tm), lambda i: (0, i))],
        out_specs=[pl.BlockSpec((8, tm), lambda i: (0, i)),
                   pl.BlockSpec((N_EXPERTS, LANES), lambda i: (0, 0))],
        out_shape=[jax.ShapeDtypeStruct((8, m), F32),
                   jax.ShapeDtypeStruct((N_EXPERTS, LANES), F32)],
        scratch_shapes=[pltpu.VMEM((N_EXPERTS, LANES), F32)],
        compiler_params=_params(1),
        name="moe_rank",
    )(rt)


def _dispatch_kernel(dest_ref, h_ref, xs_in_ref, xs_ref, sem, *, tm, m):
    del xs_in_ref
    base = pl.program_id(0) * tm

    def body(r, c):
        for k in range(2):
            d = dest_ref[k * m + base + r]
            pltpu.make_async_copy(h_ref.at[pl.ds(r, 1)], xs_ref.at[pl.ds(d, 1)], sem).start()
        return c

    lax.fori_loop(0, tm, body, 0, unroll=DMA_ISSUE_UNROLL)
    for _ in range(2):
        pltpu.make_async_copy(h_ref, xs_ref.at[pl.ds(0, tm)], sem).wait()


def _dispatch_call(dest, h, xs0, tm):
    m, d = h.shape
    return pl.pallas_call(
        functools.partial(_dispatch_kernel, tm=tm, m=m),
        grid_spec=pltpu.PrefetchScalarGridSpec(
            num_scalar_prefetch=1,
            grid=(m // tm,),
            in_specs=[pl.BlockSpec((tm, d), lambda i, dest_: (i, 0)),
                      pl.BlockSpec(memory_space=pl.ANY)],
            out_specs=pl.BlockSpec(memory_space=pl.ANY),
            scratch_shapes=[pltpu.SemaphoreType.DMA(())]),
        out_shape=jax.ShapeDtypeStruct(xs0.shape, xs0.dtype),
        input_output_aliases={2: 0},
        compiler_params=_params(1),
        name="moe_dispatch",
    )(dest, h, xs0)


def _ffn_kernel(be_ref, nu_ref, xs_ref, wg_ref, wu_ref, wd_ref, ys_ref, wg_s, wu_s, wd_s):
    i = pl.program_id(0)
    new_expert = jnp.logical_or(i == 0, be_ref[i] != be_ref[jnp.maximum(i - 1, 0)])

    @pl.when(new_expert)
    def _():
        wg_s[...] = wg_ref[0, 0].astype(MXU_DTYPE)
        wu_s[...] = wu_ref[0, 0].astype(MXU_DTYPE)
        wd_s[...] = wd_ref[0, 0].astype(MXU_DTYPE)

    @pl.when(i < nu_ref[0])
    def _():
        xb = xs_ref[...].astype(MXU_DTYPE)
        act = jax.nn.silu(_dot(xb, wg_s[...])) * _dot(xb, wu_s[...])
        ys_ref[...] = _dot(act.astype(MXU_DTYPE), wd_s[...])

    @pl.when(i >= nu_ref[0])
    def _():
        ys_ref[...] = jnp.zeros_like(ys_ref)


def _ffn_call(block_e, n_used, xs, wg, wu, wd, layer, n_blocks):
    d = xs.shape[1]
    br = EXPERT_ROWS
    rows = lambda i, be, nu: (jnp.minimum(i, nu[0] - 1), 0)
    return pl.pallas_call(
        _ffn_kernel,
        grid_spec=pltpu.PrefetchScalarGridSpec(
            num_scalar_prefetch=2,
            grid=(n_blocks,),
            in_specs=[pl.BlockSpec((br, d), rows),
                      pl.BlockSpec((1, 1, d, D_EXPERT), lambda i, be, nu: (layer, be[i], 0, 0)),
                      pl.BlockSpec((1, 1, d, D_EXPERT), lambda i, be, nu: (layer, be[i], 0, 0)),
                      pl.BlockSpec((1, 1, D_EXPERT, d), lambda i, be, nu: (layer, be[i], 0, 0))],
            out_specs=pl.BlockSpec((br, d), lambda i, be, nu: (i, 0)),
            scratch_shapes=[pltpu.VMEM((d, D_EXPERT), MXU_DTYPE),
                            pltpu.VMEM((d, D_EXPERT), MXU_DTYPE),
                            pltpu.VMEM((D_EXPERT, d), MXU_DTYPE)]),
        out_shape=jax.ShapeDtypeStruct((n_blocks * br, d), F32),
        compiler_params=_params(1),
        name="moe_experts",
    )(block_e, n_used, xs, wg, wu, wd)


def _combine_kernel(src_ref, h_ref, rc_ref, lg_ref, lb_ref, ys_ref, o_ref, buf, sem, *, tm, m, alpha):
    i = pl.program_id(0)

    def issue(tile, slot):
        base = tile * tm

        def body(r, c):
            for k in range(2):
                s = src_ref[k * m + base + r]
                pltpu.make_async_copy(ys_ref.at[pl.ds(s, 1)], buf.at[slot, k, pl.ds(r, 1)], sem.at[slot]).start()
            return c

        lax.fori_loop(0, tm, body, 0, unroll=DMA_ISSUE_UNROLL)

    @pl.when(i == 0)
    def _():
        issue(0, 0)

    @pl.when(i + 1 < m // tm)
    def _():
        issue(i + 1, (i + 1) % 2)

    slot = i % 2
    for k in range(2):
        pltpu.make_async_copy(ys_ref.at[pl.ds(0, tm)], buf.at[slot, k], sem.at[slot]).wait()
    rc = rc_ref[...]
    real = rc[:, 0:1] < float(N_EXPERTS)
    y = jnp.where(real, rc[:, 2:3] * buf[slot, 0] + rc[:, 3:4] * buf[slot, 1], 0.0)
    o_ref[...] = _layer_norm(alpha * h_ref[...] + y, lg_ref[...], lb_ref[...])


def _combine_call(src, h, rc, lg, lb, ys, alpha, tm):
    m, d = h.shape
    n_tiles = m // tm
    return pl.pallas_call(
        functools.partial(_combine_kernel, tm=tm, m=m, alpha=alpha),
        grid_spec=pltpu.PrefetchScalarGridSpec(
            num_scalar_prefetch=1,
            grid=(n_tiles,),
            in_specs=[pl.BlockSpec((tm, d), lambda i, s: (i, 0)),
                      pl.BlockSpec((tm, LANES), lambda i, s: (i, 0)),
                      pl.BlockSpec((1, d), lambda i, s: (0, 0)),
                      pl.BlockSpec((1, d), lambda i, s: (0, 0)),
                      pl.BlockSpec(memory_space=pl.ANY)],
            out_specs=pl.BlockSpec((tm, d), lambda i, s: (i, 0)),
            scratch_shapes=[pltpu.VMEM((2, 2, tm, d), F32),
                            pltpu.SemaphoreType.DMA((2,))]),
        out_shape=jax.ShapeDtypeStruct((m, d), F32),
        compiler_params=_params(1),
        name="moe_combine_ln2",
    )(src, h, rc, lg, lb, ys)


def _moe(h1, w_rg, b_rg, w_re, b_re, w_gate, w_up, w_down, layer, lg, lb, alpha, bsz, lp, l_end, tm, slot_buf):
    m, d = h1.shape
    wr = jnp.zeros((LANES, d), F32).at[0:N_EXPERTS].set(w_re.T).at[N_EXPERTS:N_EXPERTS + N_GROUPS].set(w_rg.T)
    br_ = jnp.zeros((LANES,), F32).at[0:N_EXPERTS].set(b_re).at[N_EXPERTS:N_EXPERTS + N_GROUPS].set(b_rg)
    br_ = br_.at[N_EXPERTS + N_GROUPS:N_EXPERTS + 8].set(NEG).reshape(LANES, 1)
    rt, rc = _router_call(h1, wr, br_, tm, lp, l_end)
    rk, cnt = _rank_call(rt, tm)

    rows = EXPERT_ROWS
    n_assign = 2 * bsz * (l_end - ROW_PAD)
    n_unused = lp - (l_end - ROW_PAD)
    n_blocks = -(-(n_assign + N_EXPERTS * (rows - 1)) // rows)
    n_slots = n_blocks * rows
    counts = cnt[:, 0].astype(jnp.int32)
    pcounts = (counts + rows - 1) // rows * rows
    pend = jnp.cumsum(pcounts)
    pstart = pend - pcounts
    e = rt[0:2].astype(jnp.int32)
    rank = rk[0:2].astype(jnp.int32)
    real = e < N_EXPERTS
    expert_ids = jnp.arange(N_EXPERTS, dtype=jnp.int32)[:, None, None]
    slot = jnp.sum(jnp.where(e[None] == expert_ids, pstart[:, None, None], 0), axis=0) + rank
    tok = jnp.arange(m, dtype=jnp.int32)
    pos = tok % lp
    unused_idx = (tok // lp) * n_unused + jnp.where(pos < ROW_PAD, pos, pos - l_end + ROW_PAD)
    spare = n_slots + 2 * unused_idx[None, :] + jnp.arange(2, dtype=jnp.int32)[:, None]
    dest = jnp.where(real, slot, spare).reshape(-1)
    src = jnp.where(real, slot, 0).reshape(-1)
    block_start = jnp.arange(n_blocks, dtype=jnp.int32) * rows
    block_e = jnp.minimum(jnp.sum((pend[None, :] <= block_start[:, None]).astype(jnp.int32), axis=1),
                          N_EXPERTS - 1)
    n_used = (pend[-1:] // rows).astype(jnp.int32)

    n_spare = -(-(2 * bsz * n_unused) // rows) * rows
    if slot_buf is None:
        slot_buf = jnp.zeros((n_slots + n_spare, d), F32)
    xs = _dispatch_call(dest, h1, slot_buf, tm)
    ys = _ffn_call(block_e, n_used, xs, w_gate, w_up, w_down, layer, n_blocks)
    return _combine_call(src, h1, rc, lg, lb, ys, alpha, _token_tile(m, TOKEN_TILE_TARGET)), xs


def _rope_tables(lp):
    pos = jnp.arange(lp, dtype=F32) - float(ROW_PAD)
    inv = 1.0 / (ROPE_THETA ** (jnp.arange(0, DA_DIM, 2, dtype=F32) / DA_DIM))
    ang = pos[:, None] * inv[None, :]
    reps = LANES // (DA_DIM // 2)
    sign = jnp.tile(jnp.concatenate([-jnp.ones((DA_DIM // 2,), F32), jnp.ones((DA_DIM // 2,), F32)]), LANES // DA_DIM)
    return jnp.tile(jnp.cos(ang), (1, reps)), jnp.tile(jnp.sin(ang), (1, reps)) * sign[None, :]


def _dup_heads(w, n_heads, dim):
    d = w.shape[0]
    return jnp.broadcast_to(w.reshape(d, n_heads, 1, dim), (d, n_heads, 2, dim)).reshape(d, n_heads * 2 * dim)


def kernel(x, meta, ln_in_g, ln_in_b, w_in, conv_w, conv_b, gate_b, lam_q1, lam_k1, lam_q2, lam_k2, diff_g, sink, mlstm_g, w_branch, w_out, ln1_g, ln1_b, ln2_g, ln2_b, w_rg, b_rg, w_re, b_re, w_gate, w_up, w_down):
    bsz, seq, d = x.shape
    depth = w_in.shape[0]
    assert seq % BLOCK == 0 and d == D_MODEL
    l_end = seq + BLOCK
    lp = -(-l_end // MXU_TILE) * MXU_TILE
    m = bsz * lp
    alpha = (2.0 * depth) ** 0.25
    tm = _row_tile(lp, ROW_TILE_TARGET)
    tk_attn = _row_tile(lp, ROW_TILE_TARGET)

    hp = jnp.concatenate([jnp.zeros((bsz, ROW_PAD, d), x.dtype),
                          jnp.broadcast_to(meta.astype(x.dtype)[None], (bsz, N_META_TOK, d)), x,
                          jnp.zeros((bsz, lp - l_end, d), x.dtype)], axis=1)
    h = _ln_call(hp.reshape(m, d), ln_in_g, ln_in_b, tm)
    cos, sin = _rope_tables(lp)
    q_scale = DA_DIM ** -0.5
    rope_scale = jnp.concatenate([jnp.full((A_Q,), q_scale * math.log2(math.e), F32), jnp.ones((A_K,), F32),
                                  jnp.full((B_Q,), q_scale, F32), jnp.ones((2 * B_K,), F32)]).reshape(1, -1)
    conv_scale = jnp.concatenate([jnp.ones((C_Q,), F32), jnp.full((C_K,), MC_QK ** -0.5, F32)]).reshape(1, -1)

    slot_buf = None
    for l in range(depth):
        lam_init = 0.8 - 0.6 * math.exp(-0.3 * l)
        wl = w_in[l]
        col = lambda i: wl[:, OFFS[i]:OFFS[i + 1]]
        w_rope = jnp.concatenate([col(0), col(1), col(3), _dup_heads(col(4), WB_KV, WB_DIM),
                                  _dup_heads(col(5), WB_KV, WB_DIM)], axis=1).astype(MXU_DTYPE)
        w_vt = jnp.concatenate([col(8), col(2)], axis=1).T.astype(MXU_DTYPE)
        w_conv = jnp.concatenate([col(6), col(7), col(9)], axis=1).astype(MXU_DTYPE)
        w_g = jnp.pad(col(10), ((0, 0), (0, LANES - C_G))).astype(MXU_DTYPE)
        b_g = jnp.pad(gate_b[l], (0, LANES - C_G)).reshape(1, LANES)
        w_mg = col(11).astype(MXU_DTYPE)

        rq = _proj_rope_call(h, w_rope, cos, sin, rope_scale, tm, lp).reshape(bsz, lp, -1)
        vt = _proj_t_call(h, w_vt, tm, "proj_val_t")
        zco = _proj_call(h, w_conv, tm, F32, C_Q + C_K, "proj_conv_gate")
        gc, gr = _gates_call(h, w_g, b_g, tm, lp, l_end)

        lamv = jnp.stack([lam_q1[l], lam_k1[l], lam_q2[l], lam_k2[l]])
        out_a = _attn_a_call(rq, vt, lamv, diff_g[l].reshape(-1, 1), lam_init, l_end, MXU_TILE, tk_attn)
        out_b = _attn_b_call(rq, sink[l].reshape(1, -1), l_end)
        qk = _conv_call(zco.reshape(bsz, lp, -1), conv_w[l], conv_b[l].reshape(1, -1), conv_scale, l_end,
                        _row_tile(lp, ROW_TILE_TARGET))
        h_f, h_b = _mlstm_call(qk, vt, gc, gr)

        h = _merge_call(h, out_a.reshape(m, -1), out_b.reshape(m, -1), h_f.reshape(m, -1), h_b.reshape(m, -1),
                        zco, mlstm_g[l].reshape(1, -1), w_mg, w_branch[l].astype(MXU_DTYPE),
                        w_out[l].astype(MXU_DTYPE), ln1_g[l].reshape(1, -1), ln1_b[l].reshape(1, -1),
                        alpha, _token_tile(m, TOKEN_TILE_TARGET))
        h, slot_buf = _moe(h, w_rg[l], b_rg[l], w_re[l], b_re[l], w_gate, w_up, w_down, l,
                           ln2_g[l].reshape(1, -1), ln2_b[l].reshape(1, -1), alpha, bsz, lp, l_end, tm, slot_buf)
    return h.reshape(bsz, lp, d)[:, BLOCK:l_end]
```

```python
import functools
import math

import numpy as np
import jax
import jax.numpy as jnp
from jax import lax
from jax.experimental import pallas as pl
from jax.experimental.pallas import tpu as pltpu

D_MODEL = 1024
N_META_TOK = 16
BLOCK = 128
ROW_PAD = BLOCK - N_META_TOK
ROPE_THETA = 10000.0
LN_EPS = 1e-5
NEG = -1e30

DA_HEADS = 4
DA_DIM = 64
WB_HEADS = 8
WB_KV = 2
WB_DIM = 64
WINDOW = 128
MC_HEADS = 4
MC_QK = 128
MC_V = 128
N_BRANCH = 3
BRANCH_W = 512
N_GROUPS = 4
EXP_PER_GROUP = 8
N_EXPERTS = N_GROUPS * EXP_PER_GROUP
D_EXPERT = 512

A_Q = DA_HEADS * 2 * DA_DIM
A_K = A_Q
A_V = A_Q
B_Q = WB_HEADS * WB_DIM
B_K = WB_KV * WB_DIM
B_V = B_K
C_Q = MC_HEADS * MC_QK
C_K = C_Q
C_V = MC_HEADS * MC_V
C_O = C_V
C_G = 4 * MC_HEADS
GATE_W = N_BRANCH * D_MODEL
SPLITS = (A_Q, A_K, A_V, B_Q, B_K, B_V, C_Q, C_K, C_V, C_O, C_G, GATE_W)
OFFS = tuple(int(v) for v in np.cumsum((0,) + SPLITS))

LANES = 128
MXU_TILE = 256
EXPERT_ROWS = 512
WINDOW_BLOCKS_PER_STEP = 6
MLSTM_CHUNKS_PER_STEP = 6
MLSTM_EXT = 16
DMA_ISSUE_UNROLL = 8
ATTN_Q_TILES = 3
ROW_TILE_TARGET = 768
TOKEN_TILE_TARGET = 384
ATTN_UNROLL = 4
ATTN_EXT = 16
VMEM_LIMIT = 56 * 1024 * 1024

F32 = jnp.float32
MXU_DTYPE = jnp.bfloat16


def _dot(a, b):
    return jnp.dot(a, b, preferred_element_type=F32)


def _dot_nt(a, b):
    return lax.dot_general(a, b, (((1,), (1,)), ((), ())), preferred_element_type=F32)


def _dot_tn(a, b):
    return lax.dot_general(a, b, (((0,), (0,)), ((), ())), preferred_element_type=F32)


def _params(n_axes, flags=None):
    return pltpu.CompilerParams(dimension_semantics=("arbitrary",) * n_axes,
                                vmem_limit_bytes=VMEM_LIMIT, flags=flags)


def _row_tile(n_rows, target):
    best = BLOCK
    for t in range(BLOCK, target + 1, BLOCK):
        if n_rows % t == 0:
            best = t
    return best


def _token_tile(n_rows, target):
    best = 8
    for t in range(8, target + 1, 8):
        if n_rows % t == 0:
            best = t
    return best


def _layer_norm(x, g, b):
    mu = jnp.mean(x, axis=-1, keepdims=True)
    xc = x - mu
    var = jnp.mean(xc * xc, axis=-1, keepdims=True)
    return xc * lax.rsqrt(var + LN_EPS) * g + b


def _ln_kernel(x_ref, g_ref, b_ref, o_ref):
    o_ref[...] = _layer_norm(x_ref[...], g_ref[...], b_ref[...])


def _ln_call(x, g, b, tm):
    m, d = x.shape
    return pl.pallas_call(
        _ln_kernel,
        grid=(m // tm,),
        in_specs=[pl.BlockSpec((tm, d), lambda i: (i, 0)),
                  pl.BlockSpec((1, d), lambda i: (0, 0)),
                  pl.BlockSpec((1, d), lambda i: (0, 0))],
        out_specs=pl.BlockSpec((tm, d), lambda i: (i, 0)),
        out_shape=jax.ShapeDtypeStruct((m, d), F32),
        compiler_params=_params(1),
        name="ln_in",
    )(x, g.reshape(1, d), b.reshape(1, d))


def _proj_rope_kernel(x_ref, w_ref, cos_ref, sin_ref, scale_ref, o_ref, *, n_rope):
    z = _dot(x_ref[...].astype(MXU_DTYPE), w_ref[...])
    cos = cos_ref[...]
    sin = sin_ref[...]
    lane = lax.broadcasted_iota(jnp.int32, cos.shape, 1)
    first_half = (lane % DA_DIM) < (DA_DIM // 2)
    for c in range(n_rope // LANES):
        sl = slice(c * LANES, (c + 1) * LANES)
        zc = z[:, sl]
        partner = jnp.where(first_half, pltpu.roll(zc, LANES - DA_DIM // 2, 1),
                            pltpu.roll(zc, DA_DIM // 2, 1))
        o_ref[:, sl] = ((zc * cos + partner * sin) * scale_ref[:, sl]).astype(o_ref.dtype)
    o_ref[:, n_rope:] = z[:, n_rope:].astype(o_ref.dtype)


def _proj_rope_call(h, w, cos, sin, scale, tm, lp):
    m, d = h.shape
    n = w.shape[1]
    per_batch = lp // tm
    return pl.pallas_call(
        functools.partial(_proj_rope_kernel, n_rope=scale.shape[1]),
        grid=(m // tm,),
        in_specs=[pl.BlockSpec((tm, d), lambda i: (i, 0)),
                  pl.BlockSpec((d, n), lambda i: (0, 0)),
                  pl.BlockSpec((tm, LANES), lambda i: (i % per_batch, 0)),
                  pl.BlockSpec((tm, LANES), lambda i: (i % per_batch, 0)),
                  pl.BlockSpec(scale.shape, lambda i: (0, 0))],
        out_specs=pl.BlockSpec((tm, n), lambda i: (i, 0)),
        out_shape=jax.ShapeDtypeStruct((m, n), MXU_DTYPE),
        compiler_params=_params(1),
        name="proj_rope",
    )(h, w, cos, sin, scale)


def _proj_kernel(x_ref, w_ref, o_ref, *, sigmoid_from):
    z = _dot(x_ref[...].astype(MXU_DTYPE), w_ref[...])
    if sigmoid_from is None:
        o_ref[...] = z.astype(o_ref.dtype)
    else:
        o_ref[:, :sigmoid_from] = z[:, :sigmoid_from].astype(o_ref.dtype)
        o_ref[:, sigmoid_from:] = jax.nn.sigmoid(z[:, sigmoid_from:]).astype(o_ref.dtype)


def _proj_call(h, w, tm, out_dtype, sigmoid_from, name):
    m, d = h.shape
    n = w.shape[1]
    return pl.pallas_call(
        functools.partial(_proj_kernel, sigmoid_from=sigmoid_from),
        grid=(m // tm,),
        in_specs=[pl.BlockSpec((tm, d), lambda i: (i, 0)),
                  pl.BlockSpec((d, n), lambda i: (0, 0))],
        out_specs=pl.BlockSpec((tm, n), lambda i: (i, 0)),
        out_shape=jax.ShapeDtypeStruct((m, n), out_dtype),
        compiler_params=_params(1),
        name=name,
    )(h, w)


def _proj_t_kernel(x_ref, wt_ref, o_ref):
    o_ref[...] = _dot_nt(wt_ref[...], x_ref[...].astype(MXU_DTYPE)).astype(o_ref.dtype)


def _proj_t_call(h, wt, tm, name):
    m, d = h.shape
    n = wt.shape[0]
    return pl.pallas_call(
        _proj_t_kernel,
        grid=(m // tm,),
        in_specs=[pl.BlockSpec((tm, d), lambda i: (i, 0)),
                  pl.BlockSpec((n, d), lambda i: (0, 0))],
        out_specs=pl.BlockSpec((n, tm), lambda i: (0, i)),
        out_shape=jax.ShapeDtypeStruct((n, m), MXU_DTYPE),
        compiler_params=_params(1),
        name=name,
    )(h, wt)


def _gates_kernel(x_ref, w_ref, b_ref, gc_ref, gr_ref, *, tm, lp, l_end):
    z = _dot(x_ref[...].astype(MXU_DTYPE), w_ref[...]) + b_ref[...]
    lane = lax.broadcasted_iota(jnp.int32, (tm, LANES), 1)
    kind = lane // MC_HEADS
    row = lax.broadcasted_iota(jnp.int32, (tm, LANES), 0) + pl.program_id(0) * tm
    pos = row % lp
    unused = (pos < ROW_PAD) | (pos >= l_end)
    log_f = jnp.minimum(z, 0.0) - jnp.log1p(jnp.exp(-jnp.abs(z)))
    is_forget = (kind % 2) == 1
    base = jnp.where(is_forget, jnp.where(unused, 0.0, log_f), jnp.where(unused, NEG, z))
    r128 = lax.broadcasted_iota(jnp.int32, (BLOCK, LANES), 0)
    fwd_lane = lax.broadcasted_iota(jnp.int32, (BLOCK, LANES), 1) // MC_HEADS == 1
    forget128 = (lax.broadcasted_iota(jnp.int32, (BLOCK, LANES), 1) // MC_HEADS) % 2 == 1
    for c in range(tm // BLOCK):
        x = base[c * BLOCK:(c + 1) * BLOCK]
        pre = x
        suf = x
        s = 1
        while s < BLOCK:
            pre = pre + jnp.where(r128 >= s, pltpu.roll(pre, s, 0), 0.0)
            suf = suf + jnp.where(r128 < BLOCK - s, pltpu.roll(suf, BLOCK - s, 0), 0.0)
            s *= 2
        out = jnp.where(forget128, jnp.where(fwd_lane, pre, suf), x)
        gc_ref[c * BLOCK:(c + 1) * BLOCK, :] = out
        gr_ref[:, c * BLOCK:(c + 1) * BLOCK] = out.T[0:C_G, :]


def _gates_call(h, w, b, tm, lp, l_end):
    m, d = h.shape
    return pl.pallas_call(
        functools.partial(_gates_kernel, tm=tm, lp=lp, l_end=l_end),
        grid=(m // tm,),
        in_specs=[pl.BlockSpec((tm, d), lambda i: (i, 0)),
                  pl.BlockSpec((d, LANES), lambda i: (0, 0)),
                  pl.BlockSpec((1, LANES), lambda i: (0, 0))],
        out_specs=[pl.BlockSpec((tm, LANES), lambda i: (i, 0)),
                   pl.BlockSpec((C_G, tm), lambda i: (0, i))],
        out_shape=[jax.ShapeDtypeStruct((m, LANES), F32),
                   jax.ShapeDtypeStruct((C_G, m), F32)],
        compiler_params=_params(1),
        name="mlstm_gates",
    )(h, w, b)


def _conv_kernel(z_ref, w_ref, b_ref, scale_ref, o_ref, *, lp, l_end, tr):
    w0 = w_ref[0:1, :]
    w1 = w_ref[1:2, :]
    w2 = w_ref[2:3, :]
    row = lax.broadcasted_iota(jnp.int32, (tr, LANES), 0)
    for c in range(lp // tr):
        r0 = c * tr
        zc = z_ref[0, r0:r0 + tr, :]
        before = jnp.zeros((1, LANES), F32) if r0 == 0 else z_ref[0, r0 - 1:r0, :]
        after = jnp.zeros((1, LANES), F32) if r0 + tr == lp else z_ref[0, r0 + tr:r0 + tr + 1, :]
        prev = jnp.where(row == 0, before, pltpu.roll(zc, 1, 0))
        nxt = jnp.where(row == tr - 1, after, pltpu.roll(zc, tr - 1, 0))
        if r0 <= ROW_PAD < r0 + tr:
            prev = jnp.where(row == ROW_PAD - r0, 0.0, prev)
        if r0 <= l_end - 1 < r0 + tr:
            nxt = jnp.where(row == l_end - 1 - r0, 0.0, nxt)
        y = prev * w0 + zc * w1 + nxt * w2 + b_ref[...]
        o_ref[0, r0:r0 + tr, :] = (jax.nn.silu(y) * scale_ref[...]).astype(o_ref.dtype)


def _conv_call(z, w, b, scale, l_end, tr):
    bsz, lp, _ = z.shape
    n = w.shape[1]
    return pl.pallas_call(
        functools.partial(_conv_kernel, lp=lp, l_end=l_end, tr=tr),
        grid=(bsz, n // LANES),
        in_specs=[pl.BlockSpec((1, lp, LANES), lambda b_, j: (b_, 0, j)),
                  pl.BlockSpec((3, LANES), lambda b_, j: (0, j)),
                  pl.BlockSpec((1, LANES), lambda b_, j: (0, j)),
                  pl.BlockSpec((1, LANES), lambda b_, j: (0, j))],
        out_specs=pl.BlockSpec((1, lp, LANES), lambda b_, j: (b_, 0, j)),
        out_shape=jax.ShapeDtypeStruct((bsz, lp, n), MXU_DTYPE),
        compiler_params=_params(2),
        name="mlstm_conv",
    )(z, w, b, scale)


def _attn_a_kernel(lamv_ref, g_ref, q_ref, k_ref, vt_ref, o_ref, s0_scr, s1_scr, acc_scr, *,
                   tk, n_chunks, l_end, lam_init):
    n_ch = acc_scr.shape[0]
    tq = acc_scr.shape[2]
    feat = lax.broadcasted_iota(jnp.int32, (2 * DA_DIM, tq), 0)
    key_row = lax.broadcasted_iota(jnp.int32, (tk, tq), 0)
    qz = []
    for t in range(n_ch // 2):
        qt = q_ref[0, t * tq:(t + 1) * tq, :].astype(F32).T.astype(MXU_DTYPE)
        zero = jnp.zeros_like(qt)
        qz += [jnp.where(feat < DA_DIM, qt, zero), jnp.where(feat >= DA_DIM, qt, zero)]
    s_bufs = (s0_scr, s1_scr)

    last = n_chunks - 1
    last_hi = l_end - last * tk

    def keys(j):
        return k_ref[0, pl.ds(pl.multiple_of(j * tk, tk), tk), :]

    def values(j):
        vt = vt_ref[:, pl.ds(pl.multiple_of(j * tk, tk), tk)]
        return jnp.concatenate([vt, jnp.ones((ATTN_EXT, tk), MXU_DTYPE)], axis=0)

    def score_chain(j, kj, slot, c):
        s = _dot(kj, qz[c])
        if isinstance(j, int) and j == 0:
            s = jnp.where(key_row >= ROW_PAD, s, NEG)
        if isinstance(j, int) and j == last and last_hi < tk:
            s = jnp.where(key_row < last_hi, s, NEG)
        s_bufs[slot][c] = s
        return jnp.max(s, axis=0, keepdims=True)

    def softmax_chain(vt, slot, c, m, cmax):
        m_new = jnp.maximum(m, cmax)
        alpha = jnp.exp2(m - m_new)
        p = jnp.exp2((s_bufs[slot][c] - m_new).astype(MXU_DTYPE))
        acc_scr[c] = alpha * acc_scr[c] + _dot(vt, p)
        return m_new

    def scores(j, slot):
        kj = keys(j)
        return tuple(score_chain(j, kj, slot, c) for c in range(n_ch))

    def softmax_values(j, slot, stats, cmax):
        vt = values(j)
        return tuple(softmax_chain(vt, slot, c, stats[c], cmax[c]) for c in range(n_ch))

    one = jnp.full((1, tq), NEG, F32)
    acc_scr[...] = jnp.zeros_like(acc_scr)
    cmax = scores(0, 0)
    stats = (one,) * n_ch
    if n_chunks > 1:
        def step(j, parity, state):
            stats, cmax = state
            kj, vt = keys(j + 1), values(j)
            new_stats, nxt = [], []
            for c in range(n_ch):
                nxt.append(score_chain(j + 1, kj, 1 - parity, c))
                new_stats.append(softmax_chain(vt, parity, c, stats[c], cmax[c]))
            return tuple(new_stats), tuple(nxt)

        def trip(i, st):
            for u in range(ATTN_UNROLL):
                st = step(1 + ATTN_UNROLL * i + u, (1 + u) % 2, st)
            return st

        state = step(0, 0, (stats, cmax))
        n_trips = (last - 2) // ATTN_UNROLL if last >= 2 else 0
        state = lax.fori_loop(0, n_trips, trip, state)
        for j in range(1 + n_trips * ATTN_UNROLL, last):
            state = step(j, j % 2, state)
        stats, cmax = state
    softmax_values(last, last % 2, stats, cmax)
    dv = 2 * DA_DIM
    lv = lamv_ref[...]
    lam = (jnp.exp(jnp.sum(lv[0:1] * lv[1:2], axis=-1, keepdims=True))
           - jnp.exp(jnp.sum(lv[2:3] * lv[3:4], axis=-1, keepdims=True)) + lam_init)
    for t in range(n_ch // 2):
        o0 = acc_scr[2 * t, 0:dv, :] / acc_scr[2 * t, dv:dv + 1, :]
        o1 = acc_scr[2 * t + 1, 0:dv, :] / acc_scr[2 * t + 1, dv:dv + 1, :]
        o = o0 - lam * o1
        ms = jnp.mean(o * o, axis=0, keepdims=True)
        o = o * lax.rsqrt(ms + LN_EPS) * g_ref[...] * (1.0 - lam_init)
        o_ref[0, t * tq:(t + 1) * tq, :] = o.T.astype(o_ref.dtype)


def _attn_a_call(rq, vt, lamv, g_col, lam_init, l_end, tq, tk):
    bsz, lp, _ = rq.shape
    k_blk = A_Q // LANES
    vt_blk = C_V // LANES
    n_q = ATTN_Q_TILES if (lp // tq) % ATTN_Q_TILES == 0 else 1
    n_ch = 2 * n_q
    return pl.pallas_call(
        functools.partial(_attn_a_kernel, tk=tk, n_chunks=lp // tk, l_end=l_end, lam_init=lam_init),
        grid=(bsz, DA_HEADS, lp // (n_q * tq)),
        in_specs=[pl.BlockSpec((4, DA_DIM), lambda b, h, i: (0, 0)),
                  pl.BlockSpec((2 * DA_DIM, 1), lambda b, h, i: (0, 0)),
                  pl.BlockSpec((1, n_q * tq, LANES), lambda b, h, i: (b, i, h)),
                  pl.BlockSpec((1, lp, LANES), lambda b, h, i: (b, 0, k_blk + h)),
                  pl.BlockSpec((2 * DA_DIM, lp), lambda b, h, i: (vt_blk + h, b))],
        out_specs=pl.BlockSpec((1, n_q * tq, LANES), lambda b, h, i: (b, i, h)),
        out_shape=jax.ShapeDtypeStruct((bsz, lp, A_V), MXU_DTYPE),
        scratch_shapes=[pltpu.VMEM((n_ch, tk, tq), F32),
                        pltpu.VMEM((n_ch, tk, tq), F32),
                        pltpu.VMEM((n_ch, 2 * DA_DIM + ATTN_EXT, tq), F32)],
        compiler_params=_params(3),
        name="diff_attn",
    )(lamv, g_col, rq, rq, vt)


def _attn_b_kernel(sink_ref, band_ref, q_ref, k0_ref, k1_ref, v0_ref, v1_ref, o_ref, *, lp, l_end):
    n_sub = q_ref.shape[1] // BLOCK
    for sub in range(n_sub):
        _attn_b_block(sink_ref, band_ref, q_ref, k0_ref, k1_ref, v0_ref, v1_ref, o_ref,
                      pl.program_id(1) * n_sub + sub, slice(sub * BLOCK, (sub + 1) * BLOCK), lp, l_end)


def _attn_b_block(sink_ref, band_ref, q_ref, k0_ref, k1_ref, v0_ref, v1_ref, o_ref, n, rows, lp, l_end):
    nb = lp // BLOCK
    grp = WB_HEADS // WB_KV

    def blocks(ref):
        parts = [ref[0, 0:BLOCK, :]]
        for d in (-1, 0, 1):
            idx = jnp.clip(n + d, 0, nb - 1)
            parts.append(ref[0, pl.ds(pl.multiple_of(idx * BLOCK, BLOCK), BLOCK), :])
        return jnp.concatenate(parts, axis=0)

    keys = (blocks(k0_ref), blocks(k1_ref))
    vals = (blocks(v0_ref), blocks(v1_ref))
    blk_bias = []
    for d in (-1, 0, 1):
        inside = jnp.logical_and(n + d >= 1, n + d <= l_end // BLOCK - 1)
        blk_bias.append(jnp.where(inside, 0.0, NEG))
    head_of_col = lax.broadcasted_iota(jnp.int32, (1, grp * BLOCK), 1) // BLOCK
    lane = lax.broadcasted_iota(jnp.int32, (BLOCK, LANES), 1)
    sinks, raw = [], []
    for g in range(WB_KV):
        qs = []
        sink = jnp.zeros((1, grp * BLOCK), F32)
        for j in range(grp):
            h = g * grp + j
            qt = q_ref[0, rows, (h // 2) * LANES:(h // 2 + 1) * LANES]
            keep = (lane >= WB_DIM) if h % 2 else (lane < WB_DIM)
            qs.append(jnp.where(keep, qt, jnp.zeros_like(qt)))
            sink = jnp.where(head_of_col == j, sink_ref[:, h:h + 1], sink)
        sinks.append(sink)
        raw.append(_dot_nt(keys[g], jnp.concatenate(qs, axis=0)))
    weights, dens = [], []
    for g in range(WB_KV):
        s = raw[g] + band_ref[...]
        s = jnp.concatenate([s[0:BLOCK]] + [s[(i + 1) * BLOCK:(i + 2) * BLOCK] + blk_bias[i] for i in range(3)],
                            axis=0)
        m = jnp.maximum(jnp.max(s, axis=0, keepdims=True), sinks[g])
        p = jnp.exp(s - m)
        dens.append(jnp.sum(p, axis=0, keepdims=True) + jnp.exp(sinks[g] - m))
        weights.append(p.astype(MXU_DTYPE))
    outs = [_dot_tn(vals[g], weights[g]) for g in range(WB_KV)]
    for g in range(WB_KV):
        o = (outs[g] / dens[g]).T
        for jj in range(grp // 2):
            lo = o[(2 * jj) * BLOCK:(2 * jj + 1) * BLOCK]
            hi = o[(2 * jj + 1) * BLOCK:(2 * jj + 2) * BLOCK]
            t = (g * grp) // 2 + jj
            o_ref[0, rows, t * LANES:(t + 1) * LANES] = jnp.where(lane < WB_DIM, lo, hi).astype(o_ref.dtype)


def _attn_b_call(rq, sink, l_end):
    bsz, lp, _ = rq.shape
    q_blk = (A_Q + A_K) // B_Q
    k_blk = (A_Q + A_K + B_Q) // LANES
    v_blk = k_blk + 2 * B_K // LANES
    n_sub = WINDOW_BLOCKS_PER_STEP if (lp // BLOCK) % WINDOW_BLOCKS_PER_STEP == 0 else 1
    q_rows = n_sub * BLOCK
    grp = WB_HEADS // WB_KV
    krow = np.arange(4 * BLOCK)[:, None]
    qoff = np.arange(grp * BLOCK)[None, :] % BLOCK
    ok = np.where(krow < BLOCK, krow >= ROW_PAD, np.abs(qoff + 2 * BLOCK - krow) <= WINDOW)
    band = jnp.asarray(np.where(ok, 0.0, NEG), F32)
    seq = lambda c: pl.BlockSpec((1, lp, LANES), lambda b, n: (b, 0, c))
    return pl.pallas_call(
        functools.partial(_attn_b_kernel, lp=lp, l_end=l_end),
        grid=(bsz, lp // q_rows),
        in_specs=[pl.BlockSpec((1, WB_HEADS), lambda b, n: (0, 0)),
                  pl.BlockSpec(band.shape, lambda b, n: (0, 0)),
                  pl.BlockSpec((1, q_rows, B_Q), lambda b, n: (b, n, q_blk)),
                  seq(k_blk), seq(k_blk + 1), seq(v_blk), seq(v_blk + 1)],
        out_specs=pl.BlockSpec((1, q_rows, B_Q), lambda b, n: (b, n, 0)),
        out_shape=jax.ShapeDtypeStruct((bsz, lp, B_Q), MXU_DTYPE),
        compiler_params=_params(2),
        name="window_attn",
    )(sink, band, rq, rq, rq, rq, rq)


def _mlstm_kernel(qkf_ref, vf_ref, gcf_ref, grf_ref, qkb_ref, vb_ref, gcb_ref, grb_ref,
                  hf_ref, hb_ref, c_scr, m_scr):
    t = pl.program_id(1)

    @pl.when(t == 0)
    def _():
        c_scr[...] = jnp.zeros_like(c_scr)
        m_scr[...] = jnp.zeros_like(m_scr)

    srow = lax.broadcasted_iota(jnp.int32, (BLOCK, BLOCK), 0)
    ccol = lax.broadcasted_iota(jnp.int32, (BLOCK, BLOCK), 1)
    ext_row = lax.broadcasted_iota(jnp.int32, (MLSTM_EXT, BLOCK), 0)
    ones_rows = jnp.where(ext_row == 0, 1.0, 0.0).astype(MXU_DTYPE)
    n_sub = qkf_ref.shape[1] // BLOCK
    prepared = []
    for sub in range(n_sub):
        rows_f = slice(sub * BLOCK, (sub + 1) * BLOCK)
        rows_b = slice((n_sub - 1 - sub) * BLOCK, (n_sub - sub) * BLOCK)
        dirs = ((qkf_ref, vf_ref, gcf_ref, grf_ref, hf_ref, rows_f, srow <= ccol, BLOCK - 1),
                (qkb_ref, vb_ref, gcb_ref, grb_ref, hb_ref, rows_b, srow >= ccol, 0))
        prepared.append(_mlstm_prepare(dirs, ones_rows))
    for chains in prepared:
        _mlstm_update(chains, c_scr, m_scr)


def _mlstm_prepare(dirs, ones_rows):
    chains = []
    for d, (qk_ref, vt_ref, gc_ref, gr_ref, h_ref, rows, tri, last) in enumerate(dirs):
        for hd in range(MC_HEADS):
            ci = d * MC_HEADS + hd
            j_li = (2 * d) * MC_HEADS + hd
            j_b = (2 * d + 1) * MC_HEADS + hd
            vt = vt_ref[hd * MC_V:(hd + 1) * MC_V, rows]
            b_row = gr_ref[j_b:j_b + 1, rows]
            key_col = gc_ref[rows, j_li:j_li + 1] - gc_ref[rows, j_b:j_b + 1]
            g = b_row[:, last:last + 1]
            dmat = jnp.where(tri, b_row + key_col, NEG)
            a_row = g - b_row + gr_ref[j_li:j_li + 1, rows]
            q = qk_ref[0, rows, hd * MC_QK:(hd + 1) * MC_QK]
            k = qk_ref[0, rows, C_Q + hd * MC_QK:C_Q + (hd + 1) * MC_QK]
            chains.append(dict(
                ci=ci, hd=hd, h_ref=h_ref, rows=rows, q=q, k=k, b_row=b_row, g=g, dmat=dmat, a_row=a_row,
                vext=jnp.concatenate([vt, ones_rows], axis=0),
                kq=_dot_nt(k, q),
                dmax=jnp.max(dmat, axis=0, keepdims=True),
                amax=jnp.max(a_row, axis=1, keepdims=True)))
    return chains


def _mlstm_update(chains, c_scr, m_scr):
    def read_state(ch):
        ch["m_prev"] = m_scr[ch["ci"], 0:1, 0:1]
        ch["c_prev"] = c_scr[ch["ci"]]
        ch["cq"] = _dot_nt(ch["c_prev"].astype(MXU_DTYPE), ch["q"])

    def weights(ch):
        m_t = jnp.maximum(ch["b_row"] + ch["m_prev"], ch["dmax"])
        ch["m_t"] = m_t
        ch["inter"] = jnp.exp(ch["b_row"] + ch["m_prev"] - m_t)
        ch["s"] = (ch["kq"] * jnp.exp(ch["dmat"] - m_t)).astype(MXU_DTYPE)
        m_new = jnp.maximum(ch["g"] + ch["m_prev"], ch["amax"])
        ch["m_new"] = m_new
        ch["decay"] = jnp.exp(ch["g"] + ch["m_prev"] - m_new)
        ch["vw"] = (ch["vext"].astype(F32) * jnp.exp(ch["a_row"] - m_new)).astype(MXU_DTYPE)

    def products(ch):
        ch["vs"] = _dot(ch["vext"], ch["s"])
        ch["dc"] = _dot(ch["vw"], ch["k"])

    def write_back(ch):
        nd = ch["inter"] * ch["cq"] + ch["vs"]
        den = nd[MC_V:MC_V + 1, :]
        h_t = nd[0:MC_V, :] / jnp.maximum(jnp.abs(den), jnp.exp(-ch["m_t"]))
        ch["h_ref"][0, ch["rows"], ch["hd"] * MC_V:(ch["hd"] + 1) * MC_V] = h_t.T
        c_scr[ch["ci"]] = ch["decay"] * ch["c_prev"] + ch["dc"]
        m_scr[ch["ci"]] = jnp.broadcast_to(ch["m_new"], m_scr.shape[1:])

    stages = (read_state, weights, products, write_back)
    for t in range(len(chains) + len(stages) - 1):
        for s, stage in enumerate(stages):
            if 0 <= t - s < len(chains):
                stage(chains[t - s])


def _mlstm_call(qk, vt, gc, gr):
    bsz, lp, _ = qk.shape
    n_sub = MLSTM_CHUNKS_PER_STEP if (lp // BLOCK) % MLSTM_CHUNKS_PER_STEP == 0 else 1
    rows = n_sub * BLOCK
    nch = lp // rows
    fwd = lambda b, t: (b, t, 0)
    bwd = lambda b, t: (b, nch - 1 - t, 0)
    return pl.pallas_call(
        _mlstm_kernel,
        grid=(bsz, nch),
        in_specs=[pl.BlockSpec((1, rows, C_Q + C_K), fwd),
                  pl.BlockSpec((C_V, rows), lambda b, t: (0, b * nch + t)),
                  pl.BlockSpec((rows, LANES), lambda b, t: (b * nch + t, 0)),
                  pl.BlockSpec((C_G, rows), lambda b, t: (0, b * nch + t)),
                  pl.BlockSpec((1, rows, C_Q + C_K), bwd),
                  pl.BlockSpec((C_V, rows), lambda b, t: (0, b * nch + nch - 1 - t)),
                  pl.BlockSpec((rows, LANES), lambda b, t: (b * nch + nch - 1 - t, 0)),
                  pl.BlockSpec((C_G, rows), lambda b, t: (0, b * nch + nch - 1 - t))],
        out_specs=[pl.BlockSpec((1, rows, C_V), fwd),
                   pl.BlockSpec((1, rows, C_V), bwd)],
        out_shape=[jax.ShapeDtypeStruct((bsz, lp, C_V), F32),
                   jax.ShapeDtypeStruct((bsz, lp, C_V), F32)],
        scratch_shapes=[pltpu.VMEM((2 * MC_HEADS, MC_V + MLSTM_EXT, MC_QK), F32),
                        pltpu.VMEM((2 * MC_HEADS, 8, LANES), F32)],
        compiler_params=_params(2),
        name="mlstm_scan",
    )(qk, vt, gc, gr, qk, vt, gc, gr)


def _merge_kernel(h_ref, oa_ref, ob_ref, hf_ref, hb_ref, co_ref, mg_ref, wg_ref, wb_ref, wo_ref,
                  lg_ref, lb_ref, o_ref, *, alpha):
    h = h_ref[...]
    hx = h.astype(MXU_DTYPE)
    hc = hf_ref[...] + hb_ref[...]
    parts = []
    for hd in range(MC_HEADS):
        sl = slice(hd * MC_V, (hd + 1) * MC_V)
        x = hc[:, sl]
        mu = jnp.mean(x, axis=-1, keepdims=True)
        xc = x - mu
        var = jnp.mean(xc * xc, axis=-1, keepdims=True)
        parts.append(xc * lax.rsqrt(var + LN_EPS) * mg_ref[:, sl] * co_ref[:, sl])
    oc = jnp.concatenate(parts, axis=1).astype(MXU_DTYPE)
    branches = (oa_ref[...], ob_ref[...], oc)
    merged = None
    for br in range(N_BRANCH):
        gate = jax.nn.sigmoid(_dot(hx, wg_ref[:, br * D_MODEL:(br + 1) * D_MODEL]))
        term = gate * _dot(branches[br], wb_ref[br])
        merged = term if merged is None else merged + term
    y = _dot(merged.astype(MXU_DTYPE), wo_ref[...])
    o_ref[...] = _layer_norm(alpha * h + y, lg_ref[...], lb_ref[...])


def _merge_call(h, oa, ob, hf, hb, co, mg, wg, wb, wo, lg, lb, alpha, tm):
    m, d = h.shape
    rows = lambda n: pl.BlockSpec((tm, n), lambda i: (i, 0))
    full2 = lambda a: pl.BlockSpec(a.shape, lambda i: (0, 0))
    return pl.pallas_call(
        functools.partial(_merge_kernel, alpha=alpha),
        grid=(m // tm,),
        in_specs=[rows(d), rows(A_V), rows(B_Q), rows(C_V), rows(C_V),
                  pl.BlockSpec((tm, C_O), lambda i: (i, (C_Q + C_K) // C_O)),
                  full2(mg), full2(wg), pl.BlockSpec(wb.shape, lambda i: (0, 0, 0)), full2(wo),
                  full2(lg), full2(lb)],
        out_specs=rows(d),
        out_shape=jax.ShapeDtypeStruct((m, d), F32),
        compiler_params=_params(1),
        name="merge_ln1",
    )(h, oa, ob, hf, hb, co, mg, wg, wb, wo, lg, lb)


def _split3(x):
    hi = x.astype(MXU_DTYPE)
    lo = (x - hi.astype(F32)).astype(MXU_DTYPE)
    return hi, lo


def _router_kernel(h_ref, w_ref, b_ref, rt_ref, rc_ref, *, tm, lp, l_end):
    x_hi, x_lo = _split3(h_ref[...])
    w_hi, w_lo = _split3(w_ref[...])
    logits = (_dot_nt(w_hi, x_hi) + _dot_nt(w_hi, x_lo) + _dot_nt(w_lo, x_hi)) + b_ref[...]
    none = float(N_EXPERTS)
    gl = logits[N_EXPERTS:N_EXPERTS + 8]
    grow = lax.broadcasted_iota(jnp.int32, gl.shape, 0).astype(F32)
    gmax = jnp.max(gl, axis=0, keepdims=True)
    g_sel = jnp.min(jnp.where(gl == gmax, grow, none), axis=0, keepdims=True)
    p_grp = 1.0 / jnp.sum(jnp.exp(gl - gmax), axis=0, keepdims=True)
    el = logits[0:N_EXPERTS]
    erow_i = lax.broadcasted_iota(jnp.int32, el.shape, 0)
    erow = erow_i.astype(F32)
    cand = jnp.where((erow_i // EXP_PER_GROUP).astype(F32) == g_sel, el, -jnp.inf)
    top1 = jnp.max(cand, axis=0, keepdims=True)
    i1 = jnp.min(jnp.where(cand == top1, erow, none), axis=0, keepdims=True)
    cand2 = jnp.where(erow == i1, -jnp.inf, cand)
    top2 = jnp.max(cand2, axis=0, keepdims=True)
    i2 = jnp.min(jnp.where(cand2 == top2, erow, none), axis=0, keepdims=True)
    e = jnp.exp(top2 - top1)
    w1 = (1.0 / (1.0 + e)) * p_grp
    w2 = (e / (1.0 + e)) * p_grp
    pos = lax.broadcasted_iota(jnp.int32, (1, tm), 1) + pl.program_id(0) * tm
    real = ((pos % lp) >= ROW_PAD) & ((pos % lp) < l_end)
    e1 = jnp.where(real, i1, none)
    e2 = jnp.where(real, i2, none)
    r = lax.broadcasted_iota(jnp.int32, (LANES, tm), 0)
    table = jnp.where(r == 0, e1, jnp.where(r == 1, e2, jnp.where(r == 2, w1, jnp.where(r == 3, w2, 0.0))))
    rt_ref[...] = table[0:8]
    rc_ref[...] = table.T


def _router_call(h, w, b, tm, lp, l_end):
    m, d = h.shape
    return pl.pallas_call(
        functools.partial(_router_kernel, tm=tm, lp=lp, l_end=l_end),
        grid=(m // tm,),
        in_specs=[pl.BlockSpec((tm, d), lambda i: (i, 0)),
                  pl.BlockSpec((LANES, d), lambda i: (0, 0)),
                  pl.BlockSpec((LANES, 1), lambda i: (0, 0))],
        out_specs=[pl.BlockSpec((8, tm), lambda i: (0, i)),
                   pl.BlockSpec((tm, LANES), lambda i: (i, 0))],
        out_shape=[jax.ShapeDtypeStruct((8, m), F32),
                   jax.ShapeDtypeStruct((m, LANES), F32)],
        compiler_params=_params(1),
        name="moe_router",
    )(h, w, b)


def _rank_kernel(rt_ref, rk_ref, cnt_ref, carry, *, tm):
    @pl.when(pl.program_id(0) == 0)
    def _():
        carry[...] = jnp.zeros_like(carry)

    erow = lax.broadcasted_iota(jnp.int32, (N_EXPERTS, tm), 0).astype(F32)
    oh1 = jnp.where(erow == rt_ref[0:1, :], 1.0, 0.0)
    oh2 = jnp.where(erow == rt_ref[1:2, :], 1.0, 0.0)
    oh = oh1 + oh2
    earlier = (lax.broadcasted_iota(jnp.int32, (tm, tm), 0)
               < lax.broadcasted_iota(jnp.int32, (tm, tm), 1))
    before = _dot(oh.astype(MXU_DTYPE), jnp.where(earlier, 1.0, 0.0).astype(MXU_DTYPE)) + carry[:, 0:1]
    r1 = jnp.sum(oh1 * before, axis=0, keepdims=True)
    r2 = jnp.sum(oh2 * before, axis=0, keepdims=True)
    r = lax.broadcasted_iota(jnp.int32, (8, tm), 0)
    rk_ref[...] = jnp.where(r == 0, r1, jnp.where(r == 1, r2, 0.0))
    total = carry[...] + jnp.sum(oh, axis=1, keepdims=True)
    carry[...] = total
    cnt_ref[...] = total


def _rank_call(rt, tm):
    m = rt.shape[1]
    return pl.pallas_call(
        functools.partial(_rank_kernel, tm=tm),
        grid=(m // tm,),
        in_specs=[pl.BlockSpec((8, tm), lambda i: (0, i))],
        out_specs=[pl.BlockSpec((8, tm), lambda i: (0, i)),
                   pl.BlockSpec((N_EXPERTS, LANES), lambda i: (0, 0))],
        out_shape=[jax.ShapeDtypeStruct((8, m), F32),
                   jax.ShapeDtypeStruct((N_EXPERTS, LANES), F32)],
        scratch_shapes=[pltpu.VMEM((N_EXPERTS, LANES), F32)],
        compiler_params=_params(1),
        name="moe_rank",
    )(rt)


def _dispatch_kernel(dest_ref, h_ref, xs_in_ref, xs_ref, sem, *, tm, m):
    del xs_in_ref
    base = pl.program_id(0) * tm

    def body(r, c):
        for k in range(2):
            d = dest_ref[k * m + base + r]
            pltpu.make_async_copy(h_ref.at[pl.ds(r, 1)], xs_ref.at[pl.ds(d, 1)], sem).start(priority=k)
        return c

    lax.fori_loop(0, tm, body, 0, unroll=DMA_ISSUE_UNROLL)
    for _ in range(2):
        pltpu.make_async_copy(h_ref, xs_ref.at[pl.ds(0, tm)], sem).wait()


def _dispatch_call(dest, h, xs0, tm):
    m, d = h.shape
    return pl.pallas_call(
        functools.partial(_dispatch_kernel, tm=tm, m=m),
        grid_spec=pltpu.PrefetchScalarGridSpec(
            num_scalar_prefetch=1,
            grid=(m // tm,),
            in_specs=[pl.BlockSpec((tm, d), lambda i, dest_: (i, 0)),
                      pl.BlockSpec(memory_space=pl.ANY)],
            out_specs=pl.BlockSpec(memory_space=pl.ANY),
            scratch_shapes=[pltpu.SemaphoreType.DMA(())]),
        out_shape=jax.ShapeDtypeStruct(xs0.shape, xs0.dtype),
        input_output_aliases={2: 0},
        compiler_params=_params(1),
        name="moe_dispatch",
    )(dest, h, xs0)


def _ffn_kernel(be_ref, nu_ref, xs_ref, wg_ref, wu_ref, wd_ref, ys_ref, wg_s, wu_s, wd_s):
    i = pl.program_id(0)
    new_expert = jnp.logical_or(i == 0, be_ref[i] != be_ref[jnp.maximum(i - 1, 0)])

    @pl.when(new_expert)
    def _():
        wg_s[...] = wg_ref[0, 0].astype(MXU_DTYPE)
        wu_s[...] = wu_ref[0, 0].astype(MXU_DTYPE)
        wd_s[...] = wd_ref[0, 0].astype(MXU_DTYPE)

    @pl.when(i < nu_ref[0])
    def _():
        xb = xs_ref[...].astype(MXU_DTYPE)
        act = jax.nn.silu(_dot(xb, wg_s[...])) * _dot(xb, wu_s[...])
        ys_ref[...] = _dot(act.astype(MXU_DTYPE), wd_s[...])

    @pl.when(i >= nu_ref[0])
    def _():
        ys_ref[...] = jnp.zeros_like(ys_ref)


def _ffn_call(block_e, n_used, xs, wg, wu, wd, layer, n_blocks):
    d = xs.shape[1]
    br = EXPERT_ROWS
    rows = lambda i, be, nu: (jnp.minimum(i, nu[0] - 1), 0)
    return pl.pallas_call(
        _ffn_kernel,
        grid_spec=pltpu.PrefetchScalarGridSpec(
            num_scalar_prefetch=2,
            grid=(n_blocks,),
            in_specs=[pl.BlockSpec((br, d), rows),
                      pl.BlockSpec((1, 1, d, D_EXPERT), lambda i, be, nu: (layer, be[i], 0, 0)),
                      pl.BlockSpec((1, 1, d, D_EXPERT), lambda i, be, nu: (layer, be[i], 0, 0)),
                      pl.BlockSpec((1, 1, D_EXPERT, d), lambda i, be, nu: (layer, be[i], 0, 0))],
            out_specs=pl.BlockSpec((br, d), lambda i, be, nu: (i, 0)),
            scratch_shapes=[pltpu.VMEM((d, D_EXPERT), MXU_DTYPE),
                            pltpu.VMEM((d, D_EXPERT), MXU_DTYPE),
                            pltpu.VMEM((D_EXPERT, d), MXU_DTYPE)]),
        out_shape=jax.ShapeDtypeStruct((n_blocks * br, d), F32),
        compiler_params=_params(1),
        name="moe_experts",
    )(block_e, n_used, xs, wg, wu, wd)


def _combine_kernel(src_ref, h_ref, rc_ref, lg_ref, lb_ref, ys_ref, o_ref, buf, sem, *, tm, m, alpha):
    i = pl.program_id(0)

    def issue(tile, slot):
        base = tile * tm

        def body(r, c):
            for k in range(2):
                s = src_ref[k * m + base + r]
                pltpu.make_async_copy(ys_ref.at[pl.ds(s, 1)], buf.at[slot, k, pl.ds(r, 1)], sem.at[slot]).start(priority=k)
            return c

        lax.fori_loop(0, tm, body, 0, unroll=DMA_ISSUE_UNROLL)

    @pl.when(i == 0)
    def _():
        issue(0, 0)

    @pl.when(i + 1 < m // tm)
    def _():
        issue(i + 1, (i + 1) % 2)

    slot = i % 2
    for k in range(2):
        pltpu.make_async_copy(ys_ref.at[pl.ds(0, tm)], buf.at[slot, k], sem.at[slot]).wait()
    rc = rc_ref[...]
    real = rc[:, 0:1] < float(N_EXPERTS)
    y = jnp.where(real, rc[:, 2:3] * buf[slot, 0] + rc[:, 3:4] * buf[slot, 1], 0.0)
    o_ref[...] = _layer_norm(alpha * h_ref[...] + y, lg_ref[...], lb_ref[...])


def _combine_call(src, h, rc, lg, lb, ys, alpha, tm):
    m, d = h.shape
    n_tiles = m // tm
    return pl.pallas_call(
        functools.partial(_combine_kernel, tm=tm, m=m, alpha=alpha),
        grid_spec=pltpu.PrefetchScalarGridSpec(
            num_scalar_prefetch=1,
            grid=(n_tiles,),
            in_specs=[pl.BlockSpec((tm, d), lambda i, s: (i, 0)),
                      pl.BlockSpec((tm, LANES), lambda i, s: (i, 0)),
                      pl.BlockSpec((1, d), lambda i, s: (0, 0)),
                      pl.BlockSpec((1, d), lambda i, s: (0, 0)),
                      pl.BlockSpec(memory_space=pl.ANY)],
            out_specs=pl.BlockSpec((tm, d), lambda i, s: (i, 0)),
            scratch_shapes=[pltpu.VMEM((2, 2, tm, d), F32),
                            pltpu.SemaphoreType.DMA((2,))]),
        out_shape=jax.ShapeDtypeStruct((m, d), F32),
        compiler_params=_params(1),
        name="moe_combine_ln2",
    )(src, h, rc, lg, lb, ys)


def _moe(h1, w_rg, b_rg, w_re, b_re, w_gate, w_up, w_down, layer, lg, lb, alpha, bsz, lp, l_end, tm, slot_buf):
    m, d = h1.shape
    wr = jnp.zeros((LANES, d), F32).at[0:N_EXPERTS].set(w_re.T).at[N_EXPERTS:N_EXPERTS + N_GROUPS].set(w_rg.T)
    br_ = jnp.zeros((LANES,), F32).at[0:N_EXPERTS].set(b_re).at[N_EXPERTS:N_EXPERTS + N_GROUPS].set(b_rg)
    br_ = br_.at[N_EXPERTS + N_GROUPS:N_EXPERTS + 8].set(NEG).reshape(LANES, 1)
    rt, rc = _router_call(h1, wr, br_, tm, lp, l_end)
    rk, cnt = _rank_call(rt, tm)

    rows = EXPERT_ROWS
    n_assign = 2 * bsz * (l_end - ROW_PAD)
    n_unused = lp - (l_end - ROW_PAD)
    n_blocks = -(-(n_assign + N_EXPERTS * (rows - 1)) // rows)
    n_slots = n_blocks * rows
    counts = cnt[:, 0].astype(jnp.int32)
    pcounts = (counts + rows - 1) // rows * rows
    pend = jnp.cumsum(pcounts)
    pstart = pend - pcounts
    e = rt[0:2].astype(jnp.int32)
    rank = rk[0:2].astype(jnp.int32)
    real = e < N_EXPERTS
    expert_ids = jnp.arange(N_EXPERTS, dtype=jnp.int32)[:, None, None]
    slot = jnp.sum(jnp.where(e[None] == expert_ids, pstart[:, None, None], 0), axis=0) + rank
    tok = jnp.arange(m, dtype=jnp.int32)
    pos = tok % lp
    unused_idx = (tok // lp) * n_unused + jnp.where(pos < ROW_PAD, pos, pos - l_end + ROW_PAD)
    spare = n_slots + 2 * unused_idx[None, :] + jnp.arange(2, dtype=jnp.int32)[:, None]
    dest = jnp.where(real, slot, spare).reshape(-1)
    src = jnp.where(real, slot, 0).reshape(-1)
    block_start = jnp.arange(n_blocks, dtype=jnp.int32) * rows
    block_e = jnp.minimum(jnp.sum((pend[None, :] <= block_start[:, None]).astype(jnp.int32), axis=1),
                          N_EXPERTS - 1)
    n_used = (pend[-1:] // rows).astype(jnp.int32)

    n_spare = -(-(2 * bsz * n_unused) // rows) * rows
    if slot_buf is None:
        slot_buf = jnp.zeros((n_slots + n_spare, d), F32)
    xs = _dispatch_call(dest, h1, slot_buf, tm)
    ys = _ffn_call(block_e, n_used, xs, w_gate, w_up, w_down, layer, n_blocks)
    return _combine_call(src, h1, rc, lg, lb, ys, alpha, _token_tile(m, TOKEN_TILE_TARGET)), xs


def _rope_tables(lp):
    pos = jnp.arange(lp, dtype=F32) - float(ROW_PAD)
    inv = 1.0 / (ROPE_THETA ** (jnp.arange(0, DA_DIM, 2, dtype=F32) / DA_DIM))
    ang = pos[:, None] * inv[None, :]
    reps = LANES // (DA_DIM // 2)
    sign = jnp.tile(jnp.concatenate([-jnp.ones((DA_DIM // 2,), F32), jnp.ones((DA_DIM // 2,), F32)]), LANES // DA_DIM)
    return jnp.tile(jnp.cos(ang), (1, reps)), jnp.tile(jnp.sin(ang), (1, reps)) * sign[None, :]


def _dup_heads(w, n_heads, dim):
    d = w.shape[0]
    return jnp.broadcast_to(w.reshape(d, n_heads, 1, dim), (d, n_heads, 2, dim)).reshape(d, n_heads * 2 * dim)


def kernel(x, meta, ln_in_g, ln_in_b, w_in, conv_w, conv_b, gate_b, lam_q1, lam_k1, lam_q2, lam_k2, diff_g, sink, mlstm_g, w_branch, w_out, ln1_g, ln1_b, ln2_g, ln2_b, w_rg, b_rg, w_re, b_re, w_gate, w_up, w_down):
    bsz, seq, d = x.shape
    depth = w_in.shape[0]
    assert seq % BLOCK == 0 and d == D_MODEL
    l_end = seq + BLOCK
    lp = -(-l_end // MXU_TILE) * MXU_TILE
    m = bsz * lp
    alpha = (2.0 * depth) ** 0.25
    tm = _row_tile(lp, ROW_TILE_TARGET)
    tk_attn = _row_tile(lp, ROW_TILE_TARGET)

    hp = jnp.concatenate([jnp.zeros((bsz, ROW_PAD, d), x.dtype),
                          jnp.broadcast_to(meta.astype(x.dtype)[None], (bsz, N_META_TOK, d)), x,
                          jnp.zeros((bsz, lp - l_end, d), x.dtype)], axis=1)
    h = _ln_call(hp.reshape(m, d), ln_in_g, ln_in_b, tm)
    cos, sin = _rope_tables(lp)
    q_scale = DA_DIM ** -0.5
    rope_scale = jnp.concatenate([jnp.full((A_Q,), q_scale * math.log2(math.e), F32), jnp.ones((A_K,), F32),
                                  jnp.full((B_Q,), q_scale, F32), jnp.ones((2 * B_K,), F32)]).reshape(1, -1)
    conv_scale = jnp.concatenate([jnp.ones((C_Q,), F32), jnp.full((C_K,), MC_QK ** -0.5, F32)]).reshape(1, -1)

    slot_buf = None
    for l in range(depth):
        lam_init = 0.8 - 0.6 * math.exp(-0.3 * l)
        wl = w_in[l]
        col = lambda i: wl[:, OFFS[i]:OFFS[i + 1]]
        w_rope = jnp.concatenate([col(0), col(1), col(3), _dup_heads(col(4), WB_KV, WB_DIM),
                                  _dup_heads(col(5), WB_KV, WB_DIM)], axis=1).astype(MXU_DTYPE)
        w_vt = jnp.concatenate([col(8), col(2)], axis=1).T.astype(MXU_DTYPE)
        w_conv = jnp.concatenate([col(6), col(7), col(9)], axis=1).astype(MXU_DTYPE)
        w_g = jnp.pad(col(10), ((0, 0), (0, LANES - C_G))).astype(MXU_DTYPE)
        b_g = jnp.pad(gate_b[l], (0, LANES - C_G)).reshape(1, LANES)
        w_mg = col(11).astype(MXU_DTYPE)

        rq = _proj_rope_call(h, w_rope, cos, sin, rope_scale, tm, lp).reshape(bsz, lp, -1)
        vt = _proj_t_call(h, w_vt, tm, "proj_val_t")
        zco = _proj_call(h, w_conv, tm, F32, C_Q + C_K, "proj_conv_gate")
        gc, gr = _gates_call(h, w_g, b_g, tm, lp, l_end)

        lamv = jnp.stack([lam_q1[l], lam_k1[l], lam_q2[l], lam_k2[l]])
        out_a = _attn_a_call(rq, vt, lamv, diff_g[l].reshape(-1, 1), lam_init, l_end, MXU_TILE, tk_attn)
        out_b = _attn_b_call(rq, sink[l].reshape(1, -1), l_end)
        qk = _conv_call(zco.reshape(bsz, lp, -1), conv_w[l], conv_b[l].reshape(1, -1), conv_scale, l_end,
                        _row_tile(lp, ROW_TILE_TARGET))
        h_f, h_b = _mlstm_call(qk, vt, gc, gr)

        h = _merge_call(h, out_a.reshape(m, -1), out_b.reshape(m, -1), h_f.reshape(m, -1), h_b.reshape(m, -1),
                        zco, mlstm_g[l].reshape(1, -1), w_mg, w_branch[l].astype(MXU_DTYPE),
                        w_out[l].astype(MXU_DTYPE), ln1_g[l].reshape(1, -1), ln1_b[l].reshape(1, -1),
                        alpha, _token_tile(m, TOKEN_TILE_TARGET))
        h, slot_buf = _moe(h, w_rg[l], b_rg[l], w_re[l], b_re[l], w_gate, w_up, w_down, l,
                           ln2_g[l].reshape(1, -1), ln2_b[l].reshape(1, -1), alpha, bsz, lp, l_end, tm, slot_buf)
    return h.reshape(bsz, lp, d)[:, BLOCK:l_end]
```
